```python
import math
import jax
import jax.numpy as jnp
from jax import lax
import numpy as np


D_MODEL = 4096
BATCH = 1
SEQ = 8192
DEPTH = 4

GRID_W = 64
CTX_LEN = 256
N_MIXERS = 4
EPS = 1e-6

ADA_RANK = 1024
N_MOD = 6

DA_HEADS = 16
DA_HEAD_DIM = 128
DA_V_DIM = 2 * DA_HEAD_DIM
ROPE_THETA = 10000.0
ROPE_PAIRS_PER_AXIS = DA_HEAD_DIM // 4
Q_BLOCK = 128

HY_SHORT = 3
HY_EMB = 33
HY_HIDDEN = 64
HY_INNER = 2
HY_DIRS = 2
HY_DECAY_TARGET = 1e-2
HY_FAST_DECAY = 0.3
HY_SLOW_DECAY = 1.5

POOL_SIZES = (2, 4, 8, 16)
POOL_GROUP = D_MODEL // len(POOL_SIZES)

FN_GROUPS = 4
FN_GROUP = D_MODEL // FN_GROUPS

FFN_DIM = 11008
N_EXPERTS = 8
TOP_K = 2
EXPERT_DIM = 1792
N_DENSE = (DEPTH + 1) // 2
N_MOE = DEPTH // 2

kernel_name = 'hybrid_diffusion_interleaved_block'


def _rmsnorm(x, g):
    x32 = x.astype(jnp.float32)
    y = x32 * lax.rsqrt(jnp.mean(x32 * x32, axis=-1, keepdims=True) + EPS)
    return (y * g.astype(jnp.float32)).astype(x.dtype)


def _modulate(h, shift, scale):
    return h * (1 + scale[:, None, :]) + shift[:, None, :]


def _adaln(cvec, down, up, b, n_chunks):
    width = n_chunks * D_MODEL
    mod = (jax.nn.silu(cvec) @ down) @ up[:, :width] + b[:width]
    return jnp.split(mod, n_chunks, axis=-1)


def _axial_rope(n_tokens):
    rows = n_tokens // GRID_W
    row = jnp.repeat(jnp.arange(rows, dtype=jnp.float32), GRID_W)
    col = jnp.tile(jnp.arange(GRID_W, dtype=jnp.float32), rows)
    inv = ROPE_THETA ** (-jnp.arange(ROPE_PAIRS_PER_AXIS, dtype=jnp.float32) / ROPE_PAIRS_PER_AXIS)
    ang = jnp.concatenate([row[:, None] * inv, col[:, None] * inv], axis=-1)
    return jnp.cos(ang)[None, :, None, :], jnp.sin(ang)[None, :, None, :]


def _apply_rope(x, cos, sin):
    x32 = x.astype(jnp.float32)
    x1, x2 = jnp.split(x32, 2, axis=-1)
    return jnp.concatenate([x1 * cos - x2 * sin, x2 * cos + x1 * sin], axis=-1).astype(x.dtype)


def _diff_attention(h, hc, w_qkv, w_o, lam, subln_g, lambda_init):
    B, L, D = h.shape
    C = hc.shape[1]
    q, k, v = jnp.split(h @ w_qkv, 3, axis=-1)
    kc, vc = jnp.split(hc @ w_qkv[:, D:], 2, axis=-1)
    cos, sin = _axial_rope(L)
    q = _apply_rope(q.reshape(B, L, 2 * DA_HEADS, DA_HEAD_DIM), cos, sin) * (DA_HEAD_DIM ** -0.5)
    k = _apply_rope(k.reshape(B, L, 2 * DA_HEADS, DA_HEAD_DIM), cos, sin)
    k_all = jnp.concatenate([k, kc.reshape(B, C, 2 * DA_HEADS, DA_HEAD_DIM)], axis=1)
    v_all = jnp.concatenate([v.reshape(B, L, DA_HEADS, DA_V_DIM),
                             vc.reshape(B, C, DA_HEADS, DA_V_DIM)], axis=1)
    lam32 = lam.astype(jnp.float32)
    lambda_full = (jnp.exp(jnp.sum(lam32[0] * lam32[1])) - jnp.exp(jnp.sum(lam32[2] * lam32[3]))
                   + lambda_init)
    n_blocks = L // Q_BLOCK
    q_blocks = q.reshape(B, n_blocks, Q_BLOCK, 2 * DA_HEADS, DA_HEAD_DIM).transpose(1, 0, 2, 3, 4)

    def attend(qb):
        s = jnp.einsum('bqhd,bkhd->bhqk', qb, k_all).astype(jnp.float32)
        p = jax.nn.softmax(s, axis=-1).reshape(B, DA_HEADS, 2, Q_BLOCK, -1)
        w = (p[:, :, 0] - lambda_full * p[:, :, 1]).astype(v_all.dtype)
        return jnp.einsum('bhqk,bkhe->bqhe', w, v_all)

    o = lax.map(attend, q_blocks)
    o = o.transpose(1, 0, 2, 3, 4).reshape(B, L, DA_HEADS, DA_V_DIM)
    o = _rmsnorm(o, subln_g) * (1.0 - lambda_init)
    return o.reshape(B, L, D) @ w_o


def _short_conv(u, w, b):
    pad = HY_SHORT // 2
    y = lax.conv_general_dilated(u, w[:, None, :].astype(u.dtype), window_strides=(1,),
                                 padding=((pad, HY_SHORT - 1 - pad),),
                                 dimension_numbers=('NWC', 'WIO', 'NWC'),
                                 feature_group_count=u.shape[-1])
    return y + b


def _hyena_filter(L, w_in, b_in, w_mid, b_mid, w_out, freq):
    f32 = jnp.float32
    t = jnp.linspace(0.0, 1.0, L, dtype=f32)[:, None]
    bands = (HY_EMB - 1) // 2
    fr = jnp.linspace(1e-4, bands - 1, bands, dtype=f32)
    wpos = 2.0 * math.pi * jnp.arange(L, dtype=f32)[:, None] / L
    z = jnp.concatenate([t, jnp.cos(fr * wpos), -jnp.sin(fr * wpos)], axis=-1)
    fq = freq.astype(f32)
    hdn = jnp.sin(fq * (z @ w_in.astype(f32) + b_in.astype(f32)))
    for j in range(HY_INNER):
        hdn = jnp.sin(fq * (hdn @ w_mid[j].astype(f32) + b_mid[j].astype(f32)))
    filt = (hdn @ w_out.astype(f32)).reshape(L, HY_DIRS, D_MODEL)
    max_decay = math.log(HY_DECAY_TARGET) / HY_FAST_DECAY
    min_decay = math.log(HY_DECAY_TARGET) / HY_SLOW_DECAY
    deltas = jnp.abs(jnp.linspace(min_decay, max_decay, D_MODEL, dtype=f32))
    return filt * jnp.exp(-t[:, :, None] * deltas)


def _bidir_long_conv(u, filt, bias):
    L = u.shape[1]
    kern = jnp.concatenate([filt[:, 0], jnp.zeros((1, D_MODEL), jnp.float32), filt[:0:-1, 1]], axis=0)
    kern = kern * lax.rsqrt(jnp.sum(kern * kern, axis=0, keepdims=True) + EPS)
    u_f = jnp.fft.rfft(u, n=2 * L, axis=1)
    k_f = jnp.fft.rfft(kern, n=2 * L, axis=0)
    y = jnp.fft.irfft(u_f * k_f[None], n=2 * L, axis=1)[:, :L]
    return y + u * bias.astype(jnp.float32)


def _hyena(h, w_in, conv_w, conv_b, pe_w_in, pe_b_in, pe_w_mid, pe_b_mid, pe_w_out, sin_freq, bias, w_out):
    B, L, D = h.shape
    u = _short_conv(h @ w_in, conv_w, conv_b)
    x0, x1, v = jnp.split(u, 3, axis=-1)
    filt = _hyena_filter(L, pe_w_in, pe_b_in, pe_w_mid, pe_b_mid, pe_w_out, sin_freq)
    z = _bidir_long_conv((v * x1).astype(jnp.float32), filt, bias)
    y = (x0.astype(jnp.float32) * z).astype(h.dtype)
    return y @ w_out


def _pool_mixer(h, w_groups, scale):
    B, L, D = h.shape
    h32 = h.astype(jnp.float32)
    csum = jnp.concatenate([jnp.zeros((B, 1, D), jnp.float32), jnp.cumsum(h32, axis=1)], axis=1)
    t = jnp.arange(L)
    groups = []
    for g, w in enumerate(POOL_SIZES):
        lo = jnp.clip(t - w // 2, 0, L)
        hi = jnp.clip(t + w - w // 2, 0, L)
        sl = slice(g * POOL_GROUP, (g + 1) * POOL_GROUP)
        cs = csum[..., sl]
        mean = (cs[:, hi] - cs[:, lo]) / (hi - lo).astype(jnp.float32)[None, :, None]
        groups.append(mean - h32[..., sl])
    pooled = jnp.stack(groups, axis=2).astype(h.dtype)
    y = jnp.einsum('blgc,gce->blge', pooled, w_groups).reshape(B, L, D)
    return y * scale


def _fourier_mixer(h, w_out):
    B, L, D = h.shape
    hg = h.astype(jnp.float32).reshape(B, L, FN_GROUPS, FN_GROUP)
    mixed = jnp.fft.fft2(hg, axes=(1, 3), norm='ortho').real
    return mixed.reshape(B, L, D).astype(h.dtype) @ w_out


def _swiglu(h, w_gate, w_up, w_down):
    return (jax.nn.silu(h @ w_gate) * (h @ w_up)) @ w_down


def _moe(h, router, router_b, w_gate, w_up, w_down):
    logits = (h @ router + router_b).astype(jnp.float32)
    top_val, top_idx = lax.top_k(logits, TOP_K)
    top_w = jax.nn.softmax(top_val, axis=-1)
    gate = jnp.einsum('blk,blke->ble', top_w,
                      jax.nn.one_hot(top_idx, N_EXPERTS, dtype=jnp.float32)).astype(h.dtype)
    a = jnp.einsum('bld,edf->blef', h, w_gate)
    u = jnp.einsum('bld,edf->blef', h, w_up)
    hid = jax.nn.silu(a) * u * gate[..., None]
    return jnp.einsum('blef,efd->bld', hid, w_down)


def setup_inputs(seed: int = 0) -> dict:
    key = jax.random.key(seed)
    ks = iter(jax.random.split(key, 48))
    D = D_MODEL

    def nrm(shape, scale):
        return scale * jax.random.normal(next(ks), shape, jnp.float32)

    return {
        'x': nrm((BATCH, SEQ, D), 1.0),
        'c': nrm((BATCH, D), 1.0),
        'ctx': nrm((BATCH, CTX_LEN, D), 1.0),
        'c_ctx': nrm((D,), 1.0),
        'ada_down': nrm((DEPTH, D, ADA_RANK), D ** -0.5),
        'ada_up': nrm((DEPTH, ADA_RANK, N_MOD * D), 0.5 * ADA_RANK ** -0.5),
        'ada_b': nrm((DEPTH, N_MOD * D), 0.02),
        'norm_g': 1.0 + nrm((DEPTH, 2, D), 0.05),
        'final_g': 1.0 + nrm((D,), 0.05),
        'da_w_qkv': nrm((D, 3 * D), D ** -0.5),
        'da_w_o': nrm((D, D), D ** -0.5),
        'da_lambda': nrm((4, DA_HEAD_DIM), 0.1),
        'da_subln_g': 1.0 + nrm((DA_V_DIM,), 0.05),
        'hy_w_in': nrm((D, 3 * D), D ** -0.5),
        'hy_conv_w': nrm((HY_SHORT, 3 * D), HY_SHORT ** -0.5),
        'hy_conv_b': nrm((3 * D,), 0.02),
        'hy_pe_w_in': nrm((HY_EMB, HY_HIDDEN), HY_EMB ** -0.5),
        'hy_pe_b_in': nrm((HY_HIDDEN,), 0.02),
        'hy_pe_w_mid': nrm((HY_INNER, HY_HIDDEN, HY_HIDDEN), HY_HIDDEN ** -0.5),
        'hy_pe_b_mid': nrm((HY_INNER, HY_HIDDEN), 0.02),
        'hy_pe_w_out': nrm((HY_HIDDEN, HY_DIRS * D), HY_HIDDEN ** -0.5),
        'hy_sin_freq': 1.0 + nrm((HY_HIDDEN,), 0.05),
        'hy_bias': nrm((D,), 0.5),
        'hy_w_out': nrm((D, D), D ** -0.5),
        'pool_w': nrm((len(POOL_SIZES), POOL_GROUP, POOL_GROUP), POOL_GROUP ** -0.5),
        'pool_scale': 1.0 + nrm((D,), 0.1),
        'fn_w_out': nrm((D, D), D ** -0.5),
        'ffn_w_gate': nrm((N_DENSE, D, FFN_DIM), D ** -0.5),
        'ffn_w_up': nrm((N_DENSE, D, FFN_DIM), D ** -0.5),
        'ffn_w_down': nrm((N_DENSE, FFN_DIM, D), FFN_DIM ** -0.5),
        'moe_router': nrm((N_MOE, D, N_EXPERTS), D ** -0.5),
        'moe_router_b': nrm((N_MOE, N_EXPERTS), 0.01),
        'moe_w_gate': nrm((N_MOE, N_EXPERTS, D, EXPERT_DIM), D ** -0.5),
        'moe_w_up': nrm((N_MOE, N_EXPERTS, D, EXPERT_DIM), D ** -0.5),
        'moe_w_down': nrm((N_MOE, N_EXPERTS, EXPERT_DIM, D), EXPERT_DIM ** -0.5),
    }


def reference(x, c, ctx, c_ctx, ada_down, ada_up, ada_b, norm_g, final_g,
              da_w_qkv, da_w_o, da_lambda, da_subln_g,
              hy_w_in, hy_conv_w, hy_conv_b, hy_pe_w_in, hy_pe_b_in, hy_pe_w_mid, hy_pe_b_mid,
              hy_pe_w_out, hy_sin_freq, hy_bias, hy_w_out,
              pool_w, pool_scale, fn_w_out,
              ffn_w_gate, ffn_w_up, ffn_w_down,
              moe_router, moe_router_b, moe_w_gate, moe_w_up, moe_w_down):
    for i in range(DEPTH):
        shift_a, scale_a, gate_a, shift_f, scale_f, gate_f = _adaln(c, ada_down[i], ada_up[i], ada_b[i], N_MOD)
        h = _modulate(_rmsnorm(x, norm_g[i, 0]), shift_a, scale_a)
        mixer = i % N_MIXERS
        if mixer == 0:
            shift_c, scale_c = _adaln(c_ctx[None], ada_down[i], ada_up[i], ada_b[i], 2)
            hc = _modulate(_rmsnorm(ctx, norm_g[i, 0]), shift_c, scale_c)
            y = _diff_attention(h, hc, da_w_qkv, da_w_o, da_lambda, da_subln_g,
                                0.8 - 0.6 * math.exp(-0.3 * i))
        elif mixer == 1:
            y = _hyena(h, hy_w_in, hy_conv_w, hy_conv_b, hy_pe_w_in, hy_pe_b_in, hy_pe_w_mid,
                       hy_pe_b_mid, hy_pe_w_out, hy_sin_freq, hy_bias, hy_w_out)
        elif mixer == 2:
            y = _pool_mixer(h, pool_w, pool_scale)
        else:
            y = _fourier_mixer(h, fn_w_out)
        x = x + gate_a[:, None, :] * y
        h = _modulate(_rmsnorm(x, norm_g[i, 1]), shift_f, scale_f)
        if i % 2 == 0:
            y = _swiglu(h, ffn_w_gate[i // 2], ffn_w_up[i // 2], ffn_w_down[i // 2])
        else:
            j = i // 2
            y = _moe(h, moe_router[j], moe_router_b[j], moe_w_gate[j], moe_w_up[j], moe_w_down[j])
        x = x + gate_f[:, None, :] * y
    return _rmsnorm(x, final_g)
```

```python
import functools
import math

import jax
import jax.numpy as jnp
from jax import lax
from jax.experimental import pallas as pl
from jax.experimental.pallas import tpu as pltpu

F32 = jnp.float32
BF16 = jnp.bfloat16
HIGHEST = lax.Precision.HIGHEST

EPS = 1e-6
GRID_W = 64
N_MOD = 6
DA_HEADS = 16
DA_HEAD_DIM = 128
ROPE_THETA = 10000.0
HY_EMB = 33
HY_DECAY_TARGET = 1e-2
HY_FAST_DECAY = 0.3
HY_SLOW_DECAY = 1.5
POOL_SIZES = (2, 4, 8, 16)
FN_GROUPS = 4
TOP_K = 2

LANES = 128
SUBLANES = 8
DFT_F = 128
VMEM_LIMIT = 56 * 1024 * 1024


def _params(*sem):
    return pltpu.CompilerParams(dimension_semantics=sem, vmem_limit_bytes=VMEM_LIMIT)


def _pick(dim, pref, align=LANES):
    if dim <= pref:
        return dim
    t = (pref // align) * align
    while t >= align:
        if dim % t == 0:
            return t
        t -= align
    raise ValueError(f"no {align}-aligned tile of {dim} below {pref}")


def _mm_body(*refs, n_b, n_extra, nk, epilogue):
    a_ref = refs[0]
    b_refs = refs[1:1 + n_b]
    extra_refs = refs[1 + n_b:1 + n_b + n_extra]
    o_ref = refs[1 + n_b + n_extra]
    acc_refs = refs[2 + n_b + n_extra:]
    a = a_ref[...].astype(BF16)
    if nk == 1:
        accs = [jnp.dot(a, b[...], preferred_element_type=F32) for b in b_refs]
        o_ref[...] = epilogue(accs, extra_refs).reshape(o_ref.shape).astype(o_ref.dtype)
        return
    k = pl.program_id(2)

    @pl.when(k == 0)
    def _():
        for acc in acc_refs:
            acc[...] = jnp.zeros_like(acc)

    for acc, b in zip(acc_refs, b_refs):
        acc[...] += jnp.dot(a, b[...], preferred_element_type=F32)

    @pl.when(k == nk - 1)
    def _():
        accs = [acc[...] for acc in acc_refs]
        o_ref[...] = epilogue(accs, extra_refs).reshape(o_ref.shape).astype(o_ref.dtype)


def _matmul(a, bs, *, m, n, k, tm, tn, tk, out_shape, out_spec, a_spec=None, b_specs=None,
            extras=(), extra_specs=(), epilogue=None):
    nk = k // tk
    if epilogue is None:
        epilogue = lambda accs, ex: accs[0]
    if a_spec is None:
        a_spec = pl.BlockSpec((tm, tk), lambda i, j, kk: (i, kk))
    if b_specs is None:
        b_specs = [pl.BlockSpec((tk, tn), lambda i, j, kk: (kk, j)) for _ in bs]
    scratch = [pltpu.VMEM((tm, tn), F32) for _ in bs] if nk > 1 else []
    body = functools.partial(_mm_body, n_b=len(bs), n_extra=len(extras), nk=nk, epilogue=epilogue)
    return pl.pallas_call(
        body,
        grid=(m // tm, n // tn, nk),
        in_specs=[a_spec, *b_specs, *extra_specs],
        out_specs=out_spec,
        out_shape=out_shape,
        scratch_shapes=scratch,
        compiler_params=_params("parallel", "parallel", "arbitrary"),
    )(a, *bs, *extras)


def _residual_epilogue(accs, ex):
    return ex[0][...] + ex[1][...] * accs[0]


def _proj_residual(a, w, x, gate, *, tm=1024, tn=512):
    m, k = a.shape
    n = w.shape[1]
    tm, tn = _pick(m, tm), _pick(n, tn)
    return _matmul(
        a, [w], m=m, n=n, k=k, tm=tm, tn=tn, tk=k,
        out_shape=jax.ShapeDtypeStruct((m, n), F32),
        out_spec=pl.BlockSpec((tm, tn), lambda i, j, kk: (i, j)),
        extras=[x, gate],
        extra_specs=[pl.BlockSpec((tm, tn), lambda i, j, kk: (i, j)),
                     pl.BlockSpec((1, tn), lambda i, j, kk: (0, j))],
        epilogue=_residual_epilogue)


def _ada_body(a_ref, w_ref, b_ref, o_ref, *, silu):
    a = a_ref[...]
    if silu:
        a = a * jax.nn.sigmoid(a)
    o_ref[...] = jnp.dot(a, w_ref[...], precision=HIGHEST, preferred_element_type=F32) + b_ref[...]


def _adaln_all(c, c_ctx, ada_down, ada_up, ada_b):
    depth, d, r = ada_down.shape
    width = ada_up.shape[2]
    rows = jnp.zeros((SUBLANES, d), F32).at[0].set(c[0]).at[1].set(c_ctx)
    tn1 = _pick(r, 256)
    t = pl.pallas_call(
        functools.partial(_ada_body, silu=True),
        grid=(depth, r // tn1),
        in_specs=[pl.BlockSpec((SUBLANES, d), lambda l, j: (0, 0)),
                  pl.BlockSpec((None, d, tn1), lambda l, j: (l, 0, j)),
                  pl.BlockSpec((1, tn1), lambda l, j: (0, j))],
        out_specs=pl.BlockSpec((None, SUBLANES, tn1), lambda l, j: (l, 0, j)),
        out_shape=jax.ShapeDtypeStruct((depth, SUBLANES, r), F32),
        compiler_params=_params("parallel", "parallel"),
    )(rows, ada_down, jnp.zeros((1, r), F32))
    tn2 = _pick(width, 2048)
    return pl.pallas_call(
        functools.partial(_ada_body, silu=False),
        grid=(depth, width // tn2),
        in_specs=[pl.BlockSpec((None, SUBLANES, r), lambda l, j: (l, 0, 0)),
                  pl.BlockSpec((None, r, tn2), lambda l, j: (l, 0, j)),
                  pl.BlockSpec((None, 1, tn2), lambda l, j: (l, 0, j))],
        out_specs=pl.BlockSpec((None, SUBLANES, tn2), lambda l, j: (l, 0, j)),
        out_shape=jax.ShapeDtypeStruct((depth, SUBLANES, width), F32),
        compiler_params=_params("parallel", "parallel"),
    )(t, ada_up, ada_b[:, None, :])


def _norm_rows(x, g):
    return x * lax.rsqrt(jnp.mean(x * x, axis=-1, keepdims=True) + EPS) * g


def _norm_mod_body(x_ref, g_ref, shift_ref, scale_ref, o_ref):
    y = _norm_rows(x_ref[...], g_ref[...])
    o_ref[...] = (y * (1.0 + scale_ref[...]) + shift_ref[...]).astype(o_ref.dtype)


def _norm_mod(x, g, shift, scale, out_dtype, *, tm=256):
    m, d = x.shape
    tm = _pick(m, tm, SUBLANES)
    row = pl.BlockSpec((1, d), lambda i: (0, 0))
    return pl.pallas_call(
        _norm_mod_body,
        grid=(m // tm,),
        in_specs=[pl.BlockSpec((tm, d), lambda i: (i, 0)), row, row, row],
        out_specs=pl.BlockSpec((tm, d), lambda i: (i, 0)),
        out_shape=jax.ShapeDtypeStruct((m, d), out_dtype),
        compiler_params=_params("parallel"),
    )(x, g, shift, scale)


def _rmsnorm_body(x_ref, g_ref, o_ref):
    o_ref[...] = _norm_rows(x_ref[...], g_ref[...])


def _rmsnorm(x, g, *, tm=256):
    m, d = x.shape
    tm = _pick(m, tm, SUBLANES)
    return pl.pallas_call(
        _rmsnorm_body,
        grid=(m // tm,),
        in_specs=[pl.BlockSpec((tm, d), lambda i: (i, 0)), pl.BlockSpec((1, d), lambda i: (0, 0))],
        out_specs=pl.BlockSpec((tm, d), lambda i: (i, 0)),
        out_shape=jax.ShapeDtypeStruct((m, d), F32),
        compiler_params=_params("parallel"),
    )(x, g)


def _norm_router_body(x_ref, g_ref, shift_ref, scale_ref, rw_ref, rb_ref, h_ref, gate_ref, *, n_experts):
    h = _norm_rows(x_ref[...], g_ref[...]) * (1.0 + scale_ref[...]) + shift_ref[...]
    h_ref[...] = h.astype(h_ref.dtype)
    logits = jnp.dot(h, rw_ref[...], precision=HIGHEST, preferred_element_type=F32) + rb_ref[...]
    lane = lax.broadcasted_iota(jnp.int32, logits.shape, 1)
    m1 = jnp.max(logits, axis=-1, keepdims=True)
    i1 = jnp.min(jnp.where(logits == m1, lane, LANES), axis=-1, keepdims=True)
    rest = jnp.where(lane == i1, -jnp.inf, logits)
    m2 = jnp.max(rest, axis=-1, keepdims=True)
    i2 = jnp.min(jnp.where(rest == m2, lane, LANES), axis=-1, keepdims=True)
    e = jnp.exp(m2 - m1)
    w1 = 1.0 / (1.0 + e)
    w2 = e * w1
    for ex in range(n_experts):
        col = jnp.where(i1 == ex, w1, 0.0) + jnp.where(i2 == ex, w2, 0.0)
        gate_ref[ex] = jnp.broadcast_to(col, gate_ref.shape[1:])


def _norm_router(x, g, shift, scale, router, router_b, *, tm=256):
    m, d = x.shape
    n_e = router.shape[1]
    tm = _pick(m, tm, SUBLANES)
    rw = jnp.zeros((d, LANES), F32).at[:, :n_e].set(router)
    rb = jnp.full((1, LANES), -1e30, F32).at[0, :n_e].set(router_b)
    row = pl.BlockSpec((1, d), lambda i: (0, 0))
    return pl.pallas_call(
        functools.partial(_norm_router_body, n_experts=n_e),
        grid=(m // tm,),
        in_specs=[pl.BlockSpec((tm, d), lambda i: (i, 0)), row, row, row,
                  pl.BlockSpec((d, LANES), lambda i: (0, 0)),
                  pl.BlockSpec((1, LANES), lambda i: (0, 0))],
        out_specs=[pl.BlockSpec((tm, d), lambda i: (i, 0)),
                   pl.BlockSpec((n_e, tm, LANES), lambda i: (0, i, 0))],
        out_shape=[jax.ShapeDtypeStruct((m, d), BF16),
                   jax.ShapeDtypeStruct((n_e, m, LANES), F32)],
        compiler_params=_params("parallel"),
    )(x, g, shift, scale, rw, rb)


def _silu(a):
    return a * jax.nn.sigmoid(a)


def _swiglu(h, w_gate, w_up, w_down, x, gate_f, *, tm=1024):
    m, d = h.shape
    f = w_gate.shape[1]
    tm = _pick(m, tm)
    tn = _pick(f, 512)
    hid = _matmul(
        h, [w_gate, w_up], m=m, n=f, k=d, tm=tm, tn=tn, tk=d,
        out_shape=jax.ShapeDtypeStruct((m, f), BF16),
        out_spec=pl.BlockSpec((tm, tn), lambda i, j, kk: (i, j)),
        epilogue=lambda accs, ex: _silu(accs[0]) * accs[1])
    tk = _pick(f, f // 2) if f > 4096 else f
    tn2 = _pick(d, 512)
    return _matmul(
        hid, [w_down], m=m, n=d, k=f, tm=tm, tn=tn2, tk=tk,
        out_shape=jax.ShapeDtypeStruct((m, d), F32),
        out_spec=pl.BlockSpec((tm, tn2), lambda i, j, kk: (i, j)),
        extras=[x, gate_f],
        extra_specs=[pl.BlockSpec((tm, tn2), lambda i, j, kk: (i, j)),
                     pl.BlockSpec((1, tn2), lambda i, j, kk: (0, j))],
        epilogue=_residual_epilogue)


def _moe(h, gate_rep, w_gate, w_up, w_down, x, gate_f, *, tm=1024):
    m, d = h.shape
    n_e, _, fe = w_gate.shape
    tm = _pick(m, tm)
    tn = _pick(fe, 256)
    per = fe // tn

    def epi(accs, ex):
        g = ex[0][...]
        return _silu(accs[0]) * accs[1] * jnp.tile(g, (1, tn // LANES))

    wspec = pl.BlockSpec((None, d, tn), lambda i, j, kk: (j // per, 0, j % per))
    hid = _matmul(
        h, [w_gate, w_up], m=m, n=n_e * fe, k=d, tm=tm, tn=tn, tk=d,
        out_shape=jax.ShapeDtypeStruct((m, n_e * fe), BF16),
        out_spec=pl.BlockSpec((tm, tn), lambda i, j, kk: (i, j)),
        b_specs=[wspec, wspec],
        extras=[gate_rep],
        extra_specs=[pl.BlockSpec((None, tm, LANES), lambda i, j, kk: (j // per, i, 0))],
        epilogue=epi)
    tn2 = _pick(d, 512)
    return _matmul(
        hid, [w_down], m=m, n=d, k=n_e * fe, tm=tm, tn=tn2, tk=fe,
        out_shape=jax.ShapeDtypeStruct((m, d), F32),
        out_spec=pl.BlockSpec((tm, tn2), lambda i, j, kk: (i, j)),
        b_specs=[pl.BlockSpec((None, fe, tn2), lambda i, j, kk: (kk, 0, j))],
        extras=[x, gate_f],
        extra_specs=[pl.BlockSpec((tm, tn2), lambda i, j, kk: (i, j)),
                     pl.BlockSpec((1, tn2), lambda i, j, kk: (0, j))],
        epilogue=_residual_epilogue)


def _rope_tables(n_tokens):
    pairs = DA_HEAD_DIM // 4
    rows = n_tokens // GRID_W
    row = jnp.repeat(jnp.arange(rows, dtype=F32), GRID_W)
    col = jnp.tile(jnp.arange(GRID_W, dtype=F32), rows)
    inv = ROPE_THETA ** (-jnp.arange(pairs, dtype=F32) / pairs)
    ang = jnp.concatenate([row[:, None] * inv, col[:, None] * inv], axis=-1)
    cos, sin = jnp.cos(ang), jnp.sin(ang)
    return jnp.concatenate([cos, cos], axis=-1), jnp.concatenate([-sin, sin], axis=-1)


def _qkv_rope(h, w_qkv, cos2, sin2, *, tm=1024, tn=512):
    m, d = h.shape
    n = w_qkv.shape[1]
    tm, tn = _pick(m, tm), _pick(d, tn)
    n_q = d // tn
    half = DA_HEAD_DIM // 2

    def epi(accs, ex):
        acc = accs[0]
        j = pl.program_id(1)
        cos, sin = ex[0][...], ex[1][...]
        heads = []
        for t in range(tn // DA_HEAD_DIM):
            xh = acc[:, t * DA_HEAD_DIM:(t + 1) * DA_HEAD_DIM]
            heads.append(xh * cos + pltpu.roll(xh, half, 1) * sin)
        roped = jnp.concatenate(heads, axis=1)
        qscale = jnp.where(j < n_q, DA_HEAD_DIM ** -0.5, 1.0).astype(F32)
        return jnp.where(j < 2 * n_q, roped * qscale, acc)

    tab = pl.BlockSpec((tm, DA_HEAD_DIM), lambda i, j, kk: (i, 0))
    return _matmul(
        h, [w_qkv], m=m, n=n, k=d, tm=tm, tn=tn, tk=d,
        out_shape=jax.ShapeDtypeStruct((m, n), BF16),
        out_spec=pl.BlockSpec((tm, tn), lambda i, j, kk: (i, j)),
        extras=[cos2, sin2], extra_specs=[tab, tab], epilogue=epi)


def _flash_body(lam_ref, g_ref, q_ref, k_ref, v_ref, o_ref, m_sc, l_sc, acc_sc, *, nkv, lambda_init):
    kv = pl.program_id(2)

    @pl.when(kv == 0)
    def _():
        m_sc[...] = jnp.full_like(m_sc, -jnp.inf)
        l_sc[...] = jnp.zeros_like(l_sc)
        acc_sc[...] = jnp.zeros_like(acc_sc)

    q = q_ref[...]
    k = k_ref[...]
    v = v_ref[...]
    for c in range(2):
        qc = q[:, c * DA_HEAD_DIM:(c + 1) * DA_HEAD_DIM]
        kc = k[:, c * DA_HEAD_DIM:(c + 1) * DA_HEAD_DIM]
        s = lax.dot_general(qc, kc, (((1,), (1,)), ((), ())), preferred_element_type=F32)
        m_prev = m_sc[c]
        m_next = jnp.maximum(m_prev, jnp.max(s, axis=1, keepdims=True))
        alpha = jnp.exp(m_prev - m_next)
        p = jnp.exp(s - m_next[:, :1])
        l_sc[c] = alpha * l_sc[c] + jnp.sum(p, axis=1, keepdims=True)
        m_sc[c] = m_next
        acc_sc[c] = acc_sc[c] * alpha[:, :1] + jnp.dot(p.astype(BF16), v, preferred_element_type=F32)

    @pl.when(kv == nkv - 1)
    def _():
        lam = lam_ref[...]
        lam_full = (jnp.exp(jnp.sum(lam[0:1] * lam[1:2], axis=-1, keepdims=True))
                    - jnp.exp(jnp.sum(lam[2:3] * lam[3:4], axis=-1, keepdims=True)) + lambda_init)
        o = acc_sc[0] / l_sc[0][:, :1] - lam_full * (acc_sc[1] / l_sc[1][:, :1])
        o_ref[...] = (_norm_rows(o, g_ref[...]) * (1.0 - lambda_init)).astype(o_ref.dtype)


def _diff_flash(q, k_all, v_all, lam, subln_g, lambda_init, *, tq=512, tk=768):
    l = q.shape[0]
    t, d = k_all.shape
    hw = 2 * DA_HEAD_DIM
    n_heads = d // hw
    tq = _pick(l, tq)
    tk = _pick(t, tk)
    nkv = t // tk
    return pl.pallas_call(
        functools.partial(_flash_body, nkv=nkv, lambda_init=lambda_init),
        grid=(n_heads, l // tq, nkv),
        in_specs=[pl.BlockSpec(lam.shape, lambda h, i, j: (0, 0)),
                  pl.BlockSpec((1, hw), lambda h, i, j: (0, 0)),
                  pl.BlockSpec((tq, hw), lambda h, i, j: (i, h)),
                  pl.BlockSpec((tk, hw), lambda h, i, j: (j, h)),
                  pl.BlockSpec((tk, hw), lambda h, i, j: (j, h))],
        out_specs=pl.BlockSpec((tq, hw), lambda h, i, j: (i, h)),
        out_shape=jax.ShapeDtypeStruct((l, d), BF16),
        scratch_shapes=[pltpu.VMEM((2, tq, LANES), F32), pltpu.VMEM((2, tq, LANES), F32),
                        pltpu.VMEM((2, tq, hw), F32)],
        compiler_params=_params("parallel", "parallel", "arbitrary"),
    )(lam, subln_g, q, k_all, v_all)


def _diff_attention(h, hc, w_qkv, w_o, lam, subln_g, lambda_init, x, gate_a):
    l, d = h.shape
    cos2, sin2 = _rope_tables(l)
    qkv = _qkv_rope(h, w_qkv, cos2, sin2)
    c_len = hc.shape[0]
    tmc = _pick(c_len, 256)
    kvc = _matmul(
        hc, [w_qkv], m=c_len, n=2 * d, k=d, tm=tmc, tn=512, tk=d,
        out_shape=jax.ShapeDtypeStruct((c_len, 2 * d), BF16),
        out_spec=pl.BlockSpec((tmc, 512), lambda i, j, kk: (i, j)),
        b_specs=[pl.BlockSpec((d, 512), lambda i, j, kk: (0, j + d // 512))])
    k_all = jnp.concatenate([qkv[:, d:2 * d], kvc[:, :d]], axis=0)
    v_all = jnp.concatenate([qkv[:, 2 * d:], kvc[:, d:]], axis=0)
    o = _diff_flash(qkv, k_all, v_all, lam, subln_g, lambda_init)
    return _proj_residual(o, w_o, x, gate_a)


def _angles(num, den):
    return (2.0 * math.pi / den) * (num % den).astype(F32)


def _page_matrix(base):
    ko, po, s, pi = base.shape
    eye = jnp.eye(SUBLANES, dtype=F32)
    full = jnp.einsum("abcd,jk->abjcdk", base, eye)
    return full.reshape(ko * po * SUBLANES, s * pi * SUBLANES).astype(BF16)


def _page_mm_body(m_ref, x_ref, o_ref):
    x = x_ref[...]
    x2 = x.reshape(-1, x.shape[-1]).astype(BF16)
    y = jnp.dot(m_ref[...], x2, preferred_element_type=F32)
    o_ref[...] = y.reshape(o_ref.shape).astype(o_ref.dtype)


def _page_mm(mat, x4, ko, po, *, dc=512):
    s, pi, f, d = x4.shape
    dc = _pick(d, dc)
    return pl.pallas_call(
        _page_mm_body,
        grid=(f // SUBLANES, d // dc),
        in_specs=[pl.BlockSpec(mat.shape, lambda a, b: (0, 0)),
                  pl.BlockSpec((s, pi, SUBLANES, dc), lambda a, b: (0, 0, a, b))],
        out_specs=pl.BlockSpec((ko, po, SUBLANES, dc), lambda a, b: (0, 0, a, b)),
        out_shape=jax.ShapeDtypeStruct((ko, po, f, d), F32),
        compiler_params=_params("parallel", "parallel"),
    )(mat, x4)


def _short_conv_body(u0_ref, u1_ref, u2_ref, w0_ref, w1_ref, w2_ref, b0_ref, b1_ref, b2_ref,
                     x0_ref, p_ref, *, rows):
    n_rows = u0_ref.shape[0]
    n_chunks = n_rows // rows

    def conv(u_ref, w_ref, b_ref, r0, c):
        cur = u_ref[pl.ds(r0, rows), :]
        prev = u_ref[pl.ds(jnp.maximum(r0 - 1, 0), 1), :] * (c > 0).astype(F32)
        nxt = u_ref[pl.ds(jnp.minimum(r0 + rows, n_rows - 1), 1), :] * (c < n_chunks - 1).astype(F32)
        ridx = lax.broadcasted_iota(jnp.int32, cur.shape, 0)
        up = jnp.where(ridx == 0, prev, pltpu.roll(cur, 1, 0))
        down = jnp.where(ridx == rows - 1, nxt, pltpu.roll(cur, rows - 1, 0))
        w = w_ref[...]
        return w[0:1] * up + w[1:2] * cur + w[2:3] * down + b_ref[...]

    def step(c, carry):
        r0 = pl.multiple_of(c * rows, rows)
        x0_ref[pl.ds(r0, rows), :] = conv(u0_ref, w0_ref, b0_ref, r0, c)
        p_ref[pl.ds(r0, rows), :] = conv(u2_ref, w2_ref, b2_ref, r0, c) * conv(u1_ref, w1_ref, b1_ref, r0, c)
        return carry

    lax.fori_loop(0, n_chunks, step, 0)


def _short_conv_gate(u, conv_w, conv_b, *, dc=128, rows=512):
    l, d3 = u.shape
    d = d3 // 3
    nb = d // dc
    rows = _pick(l, rows, SUBLANES)
    us = [pl.BlockSpec((l, dc), lambda j, s=s: (0, j + s * nb)) for s in range(3)]
    ws = [pl.BlockSpec((3, dc), lambda j, s=s: (0, j + s * nb)) for s in range(3)]
    bs = [pl.BlockSpec((1, dc), lambda j, s=s: (0, j + s * nb)) for s in range(3)]
    out = pl.BlockSpec((l, dc), lambda j: (0, j))
    return pl.pallas_call(
        functools.partial(_short_conv_body, rows=rows),
        grid=(nb,),
        in_specs=[*us, *ws, *bs],
        out_specs=[out, out],
        out_shape=[jax.ShapeDtypeStruct((l, d), F32), jax.ShapeDtypeStruct((l, d), F32)],
        compiler_params=_params("parallel"),
    )(u, u, u, conv_w, conv_w, conv_w, conv_b[None], conv_b[None], conv_b[None])


def _filter_body(z_ref, w_in_ref, b_in_ref, w_mid_ref, b_mid_ref, fq_ref, w_out_ref, dl_ref,
                 k_ref, ss_ref, *, n_inner):
    first = jnp.logical_and(pl.program_id(0) == 0, pl.program_id(1) == 0)

    @pl.when(first)
    def _():
        ss_ref[...] = jnp.zeros_like(ss_ref)

    z = z_ref[...]
    fq = fq_ref[...]
    hdn = jnp.sin(fq * (jnp.dot(z, w_in_ref[...], precision=HIGHEST, preferred_element_type=F32)
                        + b_in_ref[...]))
    for j in range(n_inner):
        hdn = jnp.sin(fq * (jnp.dot(hdn, w_mid_ref[j], precision=HIGHEST, preferred_element_type=F32)
                            + b_mid_ref[j:j + 1]))
    filt = jnp.dot(hdn, w_out_ref[...], precision=HIGHEST, preferred_element_type=F32)
    t = z[:, 0:1]
    valid = z[:, LANES - 1:LANES]
    kern = filt * jnp.exp(-t * dl_ref[...]) * valid
    k_ref[...] = kern
    ss_ref[...] += jnp.sum(kern * kern, axis=0, keepdims=True)


def _hyena_kernel(l, d, pe_w_in, pe_b_in, pe_w_mid, pe_b_mid, pe_w_out, sin_freq, *, tl=256):
    bands = (HY_EMB - 1) // 2
    hidden = pe_w_in.shape[1]
    n_inner = pe_w_mid.shape[0]
    pos = jnp.concatenate([jnp.arange(l), l - jnp.arange(l)]).astype(F32)
    valid = jnp.ones((2 * l,), F32).at[l].set(0.0)
    tt = pos / (l - 1)
    fr = jnp.linspace(1e-4, bands - 1, bands, dtype=F32)
    wpos = 2.0 * math.pi * pos[:, None] / l
    z = jnp.concatenate([tt[:, None], jnp.cos(fr * wpos), -jnp.sin(fr * wpos)], axis=-1)
    z = jnp.concatenate([z, jnp.zeros((2 * l, LANES - HY_EMB - 1), F32), valid[:, None]], axis=-1)
    w_in = jnp.zeros((LANES, hidden), F32).at[:HY_EMB].set(pe_w_in)
    max_decay = math.log(HY_DECAY_TARGET) / HY_FAST_DECAY
    min_decay = math.log(HY_DECAY_TARGET) / HY_SLOW_DECAY
    deltas = jnp.abs(jnp.linspace(min_decay, max_decay, d, dtype=F32))[None]
    tl = _pick(l, tl, SUBLANES)
    nt = l // tl
    const = lambda shape: pl.BlockSpec(shape, lambda a, i: tuple(0 for _ in shape))
    return pl.pallas_call(
        functools.partial(_filter_body, n_inner=n_inner),
        grid=(2, nt),
        in_specs=[pl.BlockSpec((tl, LANES), lambda a, i: (a * nt + i, 0)),
                  const((LANES, hidden)), const((1, hidden)),
                  const((n_inner, hidden, hidden)), const((n_inner, hidden)), const((1, hidden)),
                  pl.BlockSpec((hidden, d), lambda a, i: (0, a)),
                  const((1, d))],
        out_specs=[pl.BlockSpec((tl, d), lambda a, i: (a * nt + i, 0)),
                   pl.BlockSpec((1, d), lambda a, i: (0, 0))],
        out_shape=[jax.ShapeDtypeStruct((2 * l, d), F32), jax.ShapeDtypeStruct((1, d), F32)],
        compiler_params=_params("arbitrary", "arbitrary"),
    )(z, w_in, pe_b_in[None], pe_w_mid, pe_b_mid, sin_freq[None], pe_w_out, deltas)


def _conv_tables(l):
    n = 2 * l
    f = DFT_F
    s = n // f
    ar = jnp.arange
    th = _angles(ar(s)[:, None] * ar(s)[None, :], s)
    c, sn = jnp.cos(th), jnp.sin(th)
    fwd = jnp.stack([c, -sn], axis=1)[:, :, :, None]
    p1_full = _page_matrix(fwd)
    p1_half = _page_matrix(fwd[:, :, :s // 2])
    inv = jnp.stack([c.T, -sn.T], axis=-1)[: s // 2, None] / n
    p2 = _page_matrix(inv)
    k = ar(s)[:, None, None] + s * ar(f)[None, :, None]
    ph = _angles(k * ar(f)[None, None, :], n)
    gr, gi = jnp.cos(ph), -jnp.sin(ph)
    g1 = jnp.concatenate([jnp.concatenate([gr, -gi], axis=2),
                          jnp.concatenate([gi, gr], axis=2)], axis=1).astype(BF16)
    hr, hi = jnp.swapaxes(gr, 1, 2), -jnp.swapaxes(gi, 1, 2)
    g2 = jnp.concatenate([jnp.concatenate([hr, -hi], axis=2),
                          jnp.concatenate([hi, hr], axis=2)], axis=1).astype(BF16)
    return p1_half, p1_full, p2, g1, g2


def _spectrum_body(g_ref, a_ref, o_ref):
    o_ref[...] = jnp.dot(g_ref[...], a_ref[...].astype(BF16), preferred_element_type=F32)


def _spectral_conv_body(g1_ref, g2_ref, a_ref, kf_ref, o_ref):
    x = jnp.dot(g1_ref[...], a_ref[...].astype(BF16), preferred_element_type=F32)
    kf = kf_ref[...]
    f = x.shape[0] // 2
    xr, xi, kr, ki = x[:f], x[f:], kf[:f], kf[f:]
    y = jnp.concatenate([xr * kr - xi * ki, xr * ki + xi * kr], axis=0).astype(BF16)
    o_ref[...] = jnp.dot(g2_ref[...], y, preferred_element_type=F32)


def _batched_stage(body, mats, arrays, *, dc=2048):
    nb, rows, d = arrays[0].shape
    dc = _pick(d, dc)
    mspecs = [pl.BlockSpec((None,) + m.shape[1:], lambda b, j: (b, 0, 0)) for m in mats]
    aspecs = [pl.BlockSpec((None, rows, dc), lambda b, j: (b, 0, j)) for _ in arrays]
    return pl.pallas_call(
        body,
        grid=(nb, d // dc),
        in_specs=[*mspecs, *aspecs],
        out_specs=pl.BlockSpec((None, mats[-1].shape[1], dc), lambda b, j: (b, 0, j)),
        out_shape=jax.ShapeDtypeStruct((nb, mats[-1].shape[1], d), F32),
        compiler_params=_params("parallel", "parallel"),
    )(*mats, *arrays)


def _conv_out_body(m_ref, b_ref, p_ref, x0_ref, ss_ref, bias_ref, o_ref):
    b = b_ref[...]
    b2 = b.reshape(-1, b.shape[-1]).astype(BF16)
    y = jnp.dot(m_ref[...], b2, preferred_element_type=F32).reshape(p_ref.shape)
    z = y * lax.rsqrt(ss_ref[...] + EPS) + p_ref[...] * bias_ref[...]
    o_ref[...] = (x0_ref[...] * z).astype(o_ref.dtype)


def _hyena(h, w_in, conv_w, conv_b, pe_w_in, pe_b_in, pe_w_mid, pe_b_mid, pe_w_out, sin_freq, bias,
           w_out, x, gate_a):
    l, d = h.shape
    f = DFT_F
    s = 2 * l // f
    tm, tn = _pick(l, 1024), _pick(3 * d, 512)
    u = _matmul(h, [w_in], m=l, n=3 * d, k=d, tm=tm, tn=tn, tk=d,
                out_shape=jax.ShapeDtypeStruct((l, 3 * d), F32),
                out_spec=pl.BlockSpec((tm, tn), lambda i, j, kk: (i, j)))
    x0c, p = _short_conv_gate(u, conv_w, conv_b)
    kern, ss = _hyena_kernel(l, d, pe_w_in, pe_b_in, pe_w_mid, pe_b_mid, pe_w_out, sin_freq)
    p1_half, p1_full, p2, g1, g2 = _conv_tables(l)
    ka = _page_mm(p1_full, kern.reshape(s, 1, f, d), s, 2)
    kf = _batched_stage(_spectrum_body, [g1], [ka.reshape(s, 2 * f, d)])
    pa = _page_mm(p1_half, p.reshape(s // 2, 1, f, d), s, 2)
    pb = _batched_stage(_spectral_conv_body, [g1, g2], [pa.reshape(s, 2 * f, d), kf])
    dc = _pick(d, 512)
    page = pl.BlockSpec((s // 2, SUBLANES, dc), lambda a, b: (0, a, b))
    chan = pl.BlockSpec((1, dc), lambda a, b: (0, b))
    y = pl.pallas_call(
        _conv_out_body,
        grid=(f // SUBLANES, d // dc),
        in_specs=[pl.BlockSpec(p2.shape, lambda a, b: (0, 0)),
                  pl.BlockSpec((s, 2, SUBLANES, dc), lambda a, b: (0, 0, a, b)),
                  page, page, chan, chan],
        out_specs=page,
        out_shape=jax.ShapeDtypeStruct((s // 2, f, d), BF16),
        compiler_params=_params("parallel", "parallel"),
    )(p2, pb.reshape(s, 2, f, d), p.reshape(s // 2, f, d), x0c.reshape(s // 2, f, d), ss, bias[None])
    return _proj_residual(y.reshape(l, d), w_out, x, gate_a)


def _pool_body(h_ref, o_ref, pad_ref, *, rows, per_group, halo):
    n_rows = h_ref.shape[0]
    dc = h_ref.shape[1]
    pad_ref[pl.ds(0, halo), :] = jnp.zeros((halo, dc), F32)
    pad_ref[pl.ds(halo + n_rows, halo), :] = jnp.zeros((halo, dc), F32)
    pad_ref[pl.ds(halo, n_rows), :] = h_ref[...]
    group = pl.program_id(0) // per_group

    for g, w in enumerate(POOL_SIZES):
        @pl.when(group == g)
        def _(w=w):
            before, after = w // 2, w - w // 2

            def step(c, carry):
                r0 = pl.multiple_of(c * rows, rows)
                ext = pad_ref[pl.ds(r0, rows + 2 * halo), :]
                tot = ext[halo - before:halo - before + rows]
                for o in range(1 - before, after):
                    tot = tot + ext[halo + o:halo + o + rows]
                t = r0 + lax.broadcasted_iota(jnp.int32, (rows, dc), 0)
                cnt = jnp.minimum(t + after, n_rows) - jnp.maximum(t - before, 0)
                cur = ext[halo:halo + rows]
                o_ref[pl.ds(r0, rows), :] = (tot / cnt.astype(F32) - cur).astype(o_ref.dtype)
                return carry

            lax.fori_loop(0, n_rows // rows, step, 0)


def _pool_features(h, *, dc=256, rows=256):
    l, d = h.shape
    group = d // len(POOL_SIZES)
    dc = _pick(group, dc)
    rows = _pick(l, rows, SUBLANES)
    halo = max(POOL_SIZES) // 2
    return pl.pallas_call(
        functools.partial(_pool_body, rows=rows, per_group=group // dc, halo=halo),
        grid=(d // dc,),
        in_specs=[pl.BlockSpec((l, dc), lambda j: (0, j))],
        out_specs=pl.BlockSpec((l, dc), lambda j: (0, j)),
        out_shape=jax.ShapeDtypeStruct((l, d), BF16),
        scratch_shapes=[pltpu.VMEM((l + 2 * halo, dc), F32)],
        compiler_params=_params("parallel"),
    )(h)


def _pool_mixer(h, w_groups, scale, x, gate_a, *, tm=1024, tn=512):
    l, d = h.shape
    n_g, group, _ = w_groups.shape
    pooled = _pool_features(h)
    tm, tn = _pick(l, tm), _pick(group, tn)
    per = group // tn
    return _matmul(
        pooled, [w_groups], m=l, n=d, k=group, tm=tm, tn=tn, tk=group,
        out_shape=jax.ShapeDtypeStruct((l, d), F32),
        out_spec=pl.BlockSpec((tm, tn), lambda i, j, kk: (i, j)),
        a_spec=pl.BlockSpec((tm, group), lambda i, j, kk: (i, j // per)),
        b_specs=[pl.BlockSpec((None, group, tn), lambda i, j, kk: (j // per, 0, j % per))],
        extras=[x, gate_a, scale],
        extra_specs=[pl.BlockSpec((tm, tn), lambda i, j, kk: (i, j)),
                     pl.BlockSpec((1, tn), lambda i, j, kk: (0, j)),
                     pl.BlockSpec((1, tn), lambda i, j, kk: (0, j))],
        epilogue=lambda accs, ex: ex[0][...] + ex[1][...] * (accs[0] * ex[2][...]))


def _fourier_tables(l, group):
    f = DFT_F
    s = l // f
    ar = jnp.arange
    thc = _angles(ar(group)[:, None] * ar(group)[None, :], group)
    wc = jnp.concatenate([jnp.cos(thc), -jnp.sin(thc)], axis=1).astype(BF16)
    th = _angles(ar(s)[:, None] * ar(s)[None, :], s)
    c, sn = jnp.cos(th), jnp.sin(th)
    base = jnp.stack([jnp.stack([c, sn], axis=-1), jnp.stack([-sn, c], axis=-1)], axis=1)
    p1 = _page_matrix(base)
    k = ar(s)[:, None, None] + s * ar(f)[None, :, None]
    ph = _angles(k * ar(f)[None, None, :], l)
    g = jnp.concatenate([jnp.cos(ph), jnp.sin(ph)], axis=2)
    g = g.reshape(s // SUBLANES, SUBLANES, f, 2 * f)
    eye = jnp.eye(SUBLANES, dtype=F32)
    scat = jnp.einsum("qjkc,ji->qkjic", g, eye).reshape(s // SUBLANES, f * SUBLANES, SUBLANES * 2 * f)
    return wc, p1, scat.astype(BF16)


def _fourier_out_body(m_ref, a_ref, o_ref, *, scale):
    y = jnp.dot(m_ref[...], a_ref[...].astype(BF16), preferred_element_type=F32) * scale
    o_ref[...] = y.reshape(o_ref.shape)


def _fourier_mixer(h, w_out, x, gate_a, *, tm=1024, tn=512, dc=1024):
    l, d = h.shape
    group = d // FN_GROUPS
    f = DFT_F
    s = l // f
    wc, p1, scat = _fourier_tables(l, group)
    tm, tn = _pick(l, tm, f), _pick(group, tn)
    per = group // tn
    z = _matmul(
        h, [wc], m=l, n=2 * d, k=group, tm=tm, tn=tn, tk=group,
        out_shape=jax.ShapeDtypeStruct((s, 2, f, d), F32),
        out_spec=pl.BlockSpec((tm // f, None, f, tn),
                              lambda i, j, kk: (i, (j // per) % 2, 0, (j // (2 * per)) * per + j % per)),
        a_spec=pl.BlockSpec((tm, group), lambda i, j, kk: (i, j // (2 * per))),
        b_specs=[pl.BlockSpec((group, tn), lambda i, j, kk: (0, j % (2 * per)))])
    a2 = _page_mm(p1, z, s, 2)
    dc = _pick(d, dc)
    q = s // SUBLANES
    mixed = pl.pallas_call(
        functools.partial(_fourier_out_body, scale=1.0 / math.sqrt(l * group)),
        grid=(q, d // dc),
        in_specs=[pl.BlockSpec((None,) + scat.shape[1:], lambda a, b: (a, 0, 0)),
                  pl.BlockSpec((None, SUBLANES * 2 * f, dc), lambda a, b: (a, 0, b))],
        out_specs=pl.BlockSpec((f, None, SUBLANES, dc), lambda a, b: (0, a, 0, b)),
        out_shape=jax.ShapeDtypeStruct((f, q, SUBLANES, d), F32),
        compiler_params=_params("parallel", "parallel"),
    )(scat, a2.reshape(q, SUBLANES * 2 * f, d))
    return _proj_residual(mixed.reshape(l, d), w_out, x, gate_a, tm=512)


def kernel(x, c, ctx, c_ctx, ada_down, ada_up, ada_b, norm_g, final_g, da_w_qkv, da_w_o, da_lambda, da_subln_g, hy_w_in, hy_conv_w, hy_conv_b, hy_pe_w_in, hy_pe_b_in, hy_pe_w_mid, hy_pe_b_mid, hy_pe_w_out, hy_sin_freq, hy_bias, hy_w_out, pool_w, pool_scale, fn_w_out, ffn_w_gate, ffn_w_up, ffn_w_down, moe_router, moe_router_b, moe_w_gate, moe_w_up, moe_w_down):
    batch, l, d = x.shape
    depth = ada_down.shape[0]
    assert batch == 1
    xs = x[0]
    mod = _adaln_all(c, c_ctx, ada_down, ada_up, ada_b)
    bf = lambda w: w.astype(BF16)
    for i in range(depth):
        shift_a, scale_a, gate_a, shift_f, scale_f, gate_f = [
            mod[i, 0:1, n * d:(n + 1) * d] for n in range(N_MOD)]
        g_a, g_f = norm_g[i, 0][None], norm_g[i, 1][None]
        mixer = i % 4
        if mixer == 0:
            h = _norm_mod(xs, g_a, shift_a, scale_a, BF16)
            hc = _norm_mod(ctx[0], g_a, mod[i, 1:2, 0:d], mod[i, 1:2, d:2 * d], BF16)
            xs = _diff_attention(h, hc, bf(da_w_qkv), bf(da_w_o), da_lambda, da_subln_g[None],
                                 0.8 - 0.6 * math.exp(-0.3 * i), xs, gate_a)
        elif mixer == 1:
            h = _norm_mod(xs, g_a, shift_a, scale_a, BF16)
            xs = _hyena(h, bf(hy_w_in), hy_conv_w, hy_conv_b, hy_pe_w_in, hy_pe_b_in, hy_pe_w_mid,
                        hy_pe_b_mid, hy_pe_w_out, hy_sin_freq, hy_bias, bf(hy_w_out), xs, gate_a)
        elif mixer == 2:
            h = _norm_mod(xs, g_a, shift_a, scale_a, F32)
            xs = _pool_mixer(h, bf(pool_w), pool_scale[None], xs, gate_a)
        else:
            h = _norm_mod(xs, g_a, shift_a, scale_a, BF16)
            xs = _fourier_mixer(h, bf(fn_w_out), xs, gate_a)
        if i % 2 == 0:
            j = i // 2
            h = _norm_mod(xs, g_f, shift_f, scale_f, BF16)
            xs = _swiglu(h, bf(ffn_w_gate[j]), bf(ffn_w_up[j]), bf(ffn_w_down[j]), xs, gate_f)
        else:
            j = i // 2
            h, gate_rep = _norm_router(xs, g_f, shift_f, scale_f, moe_router[j], moe_router_b[j])
            xs = _moe(h, gate_rep, bf(moe_w_gate[j]), bf(moe_w_up[j]), bf(moe_w_down[j]), xs, gate_f)
    return _rmsnorm(xs, final_g[None])[None]
```

```python
import functools
import math

import jax
import jax.numpy as jnp
from jax import lax
from jax.experimental import pallas as pl
from jax.experimental.pallas import tpu as pltpu

F32 = jnp.float32
BF16 = jnp.bfloat16
HIGHEST = lax.Precision.HIGHEST

EPS = 1e-6
GRID_W = 64
N_MOD = 6
DA_HEADS = 16
DA_HEAD_DIM = 128
ROPE_THETA = 10000.0
HY_EMB = 33
HY_DECAY_TARGET = 1e-2
HY_FAST_DECAY = 0.3
HY_SLOW_DECAY = 1.5
POOL_SIZES = (2, 4, 8, 16)
FN_GROUPS = 4
TOP_K = 2

LANES = 128
SUBLANES = 8
DFT_F = 128
VMEM_LIMIT = 56 * 1024 * 1024


def _params(*sem):
    return pltpu.CompilerParams(dimension_semantics=sem, vmem_limit_bytes=VMEM_LIMIT)


def _pick(dim, pref, align=LANES):
    if dim <= pref:
        return dim
    t = (pref // align) * align
    while t >= align:
        if dim % t == 0:
            return t
        t -= align
    raise ValueError(f"no {align}-aligned tile of {dim} below {pref}")


def _mm_body(*refs, n_b, n_extra, nk, epilogue):
    a_ref = refs[0]
    b_refs = refs[1:1 + n_b]
    extra_refs = refs[1 + n_b:1 + n_b + n_extra]
    o_ref = refs[1 + n_b + n_extra]
    acc_refs = refs[2 + n_b + n_extra:]
    a = a_ref[...].astype(BF16)
    if nk == 1:
        accs = [jnp.dot(a, b[...].astype(BF16), preferred_element_type=F32) for b in b_refs]
        o_ref[...] = epilogue(accs, extra_refs).reshape(o_ref.shape).astype(o_ref.dtype)
        return
    k = pl.program_id(2)

    @pl.when(k == 0)
    def _():
        for acc in acc_refs:
            acc[...] = jnp.zeros_like(acc)

    for acc, b in zip(acc_refs, b_refs):
        acc[...] += jnp.dot(a, b[...].astype(BF16), preferred_element_type=F32)

    @pl.when(k == nk - 1)
    def _():
        accs = [acc[...] for acc in acc_refs]
        o_ref[...] = epilogue(accs, extra_refs).reshape(o_ref.shape).astype(o_ref.dtype)


def _matmul(name, a, bs, *, m, n, k, tm, tn, tk, out_shape, out_spec, a_spec=None, b_specs=None,
            extras=(), extra_specs=(), epilogue=None):
    nk = k // tk
    if epilogue is None:
        epilogue = lambda accs, ex: accs[0]
    if a_spec is None:
        a_spec = pl.BlockSpec((tm, tk), lambda i, j, kk: (i, kk))
    if b_specs is None:
        b_specs = [pl.BlockSpec((tk, tn), lambda i, j, kk: (kk, j)) for _ in bs]
    scratch = [pltpu.VMEM((tm, tn), F32) for _ in bs] if nk > 1 else []
    body = functools.partial(_mm_body, n_b=len(bs), n_extra=len(extras), nk=nk, epilogue=epilogue)
    return pl.pallas_call(
        body,
        grid=(m // tm, n // tn, nk),
        in_specs=[a_spec, *b_specs, *extra_specs],
        out_specs=out_spec,
        out_shape=out_shape,
        scratch_shapes=scratch,
        compiler_params=_params("parallel", "parallel", "arbitrary"),
        name=name,
    )(a, *bs, *extras)


def _residual_epilogue(accs, ex):
    return ex[0][...] + ex[1][...] * accs[0]


def _proj_residual(name, a, w, x, gate, *, tm=1024, tn=512):
    m, k = a.shape
    n = w.shape[1]
    tm, tn = _pick(m, tm), _pick(n, tn)
    return _matmul(
        name, a, [w], m=m, n=n, k=k, tm=tm, tn=tn, tk=k,
        out_shape=jax.ShapeDtypeStruct((m, n), F32),
        out_spec=pl.BlockSpec((tm, tn), lambda i, j, kk: (i, j)),
        extras=[x, gate],
        extra_specs=[pl.BlockSpec((tm, tn), lambda i, j, kk: (i, j)),
                     pl.BlockSpec((1, tn), lambda i, j, kk: (0, j))],
        epilogue=_residual_epilogue)


def _ada_body(a_ref, w_ref, b_ref, o_ref, *, silu):
    a = a_ref[...]
    if silu:
        a = a * jax.nn.sigmoid(a)
    o_ref[...] = jnp.dot(a, w_ref[...], precision=HIGHEST, preferred_element_type=F32) + b_ref[...]


def _adaln_all(c, c_ctx, ada_down, ada_up, ada_b):
    depth, d, r = ada_down.shape
    width = ada_up.shape[2]
    rows = jnp.zeros((SUBLANES, d), F32).at[0].set(c[0]).at[1].set(c_ctx)
    tn1 = _pick(r, 256)
    t = pl.pallas_call(
        functools.partial(_ada_body, silu=True),
        grid=(depth, r // tn1),
        in_specs=[pl.BlockSpec((SUBLANES, d), lambda l, j: (0, 0)),
                  pl.BlockSpec((None, d, tn1), lambda l, j: (l, 0, j)),
                  pl.BlockSpec((1, tn1), lambda l, j: (0, j))],
        out_specs=pl.BlockSpec((None, SUBLANES, tn1), lambda l, j: (l, 0, j)),
        out_shape=jax.ShapeDtypeStruct((depth, SUBLANES, r), F32),
        compiler_params=_params("parallel", "parallel"),
        name="adaln_down",
    )(rows, ada_down, jnp.zeros((1, r), F32))
    tn2 = _pick(width, 2048)
    return pl.pallas_call(
        functools.partial(_ada_body, silu=False),
        grid=(depth, width // tn2),
        in_specs=[pl.BlockSpec((None, SUBLANES, r), lambda l, j: (l, 0, 0)),
                  pl.BlockSpec((None, r, tn2), lambda l, j: (l, 0, j)),
                  pl.BlockSpec((None, 1, tn2), lambda l, j: (l, 0, j))],
        out_specs=pl.BlockSpec((None, SUBLANES, tn2), lambda l, j: (l, 0, j)),
        out_shape=jax.ShapeDtypeStruct((depth, SUBLANES, width), F32),
        compiler_params=_params("parallel", "parallel"),
        name="adaln_up",
    )(t, ada_up, ada_b[:, None, :])


def _norm_rows(x, g):
    return x * lax.rsqrt(jnp.mean(x * x, axis=-1, keepdims=True) + EPS) * g


def _norm_mod_body(x_ref, g_ref, shift_ref, scale_ref, o_ref):
    y = _norm_rows(x_ref[...], g_ref[...])
    o_ref[...] = (y * (1.0 + scale_ref[...]) + shift_ref[...]).astype(o_ref.dtype)


def _norm_mod(x, g, shift, scale, out_dtype, *, tm=256):
    m, d = x.shape
    tm = _pick(m, tm, SUBLANES)
    row = pl.BlockSpec((1, d), lambda i: (0, 0))
    return pl.pallas_call(
        _norm_mod_body,
        grid=(m // tm,),
        in_specs=[pl.BlockSpec((tm, d), lambda i: (i, 0)), row, row, row],
        out_specs=pl.BlockSpec((tm, d), lambda i: (i, 0)),
        out_shape=jax.ShapeDtypeStruct((m, d), out_dtype),
        compiler_params=_params("parallel"),
        name="norm_mod",
    )(x, g, shift, scale)


def _rmsnorm_body(x_ref, g_ref, o_ref):
    o_ref[...] = _norm_rows(x_ref[...], g_ref[...])


def _rmsnorm(x, g, *, tm=256):
    m, d = x.shape
    tm = _pick(m, tm, SUBLANES)
    return pl.pallas_call(
        _rmsnorm_body,
        grid=(m // tm,),
        in_specs=[pl.BlockSpec((tm, d), lambda i: (i, 0)), pl.BlockSpec((1, d), lambda i: (0, 0))],
        out_specs=pl.BlockSpec((tm, d), lambda i: (i, 0)),
        out_shape=jax.ShapeDtypeStruct((m, d), F32),
        compiler_params=_params("parallel"),
        name="final_norm",
    )(x, g)


def _norm_router_body(x_ref, g_ref, shift_ref, scale_ref, rw_ref, rb_ref, h_ref, sel_ref):
    h = _norm_rows(x_ref[...], g_ref[...]) * (1.0 + scale_ref[...]) + shift_ref[...]
    h_ref[...] = h
    logits = jnp.dot(h, rw_ref[...], precision=HIGHEST, preferred_element_type=F32) + rb_ref[...]
    lane = lax.broadcasted_iota(jnp.int32, logits.shape, 1)
    m1 = jnp.max(logits, axis=-1, keepdims=True)
    i1 = jnp.min(jnp.where(logits == m1, lane, LANES), axis=-1, keepdims=True)
    rest = jnp.where(lane == i1, -jnp.inf, logits)
    m2 = jnp.max(rest, axis=-1, keepdims=True)
    i2 = jnp.min(jnp.where(rest == m2, lane, LANES), axis=-1, keepdims=True)
    e = jnp.exp(m2 - m1)
    w1 = 1.0 / (1.0 + e)
    w2 = e * w1
    sel_ref[...] = jnp.where(lane == 0, i1.astype(F32),
                             jnp.where(lane == 1, i2.astype(F32),
                                       jnp.where(lane == 2, w1, jnp.where(lane == 3, w2, 0.0))))


def _norm_router(x, g, shift, scale, router, router_b, *, tm=256):
    m, d = x.shape
    n_e = router.shape[1]
    tm = _pick(m, tm, SUBLANES)
    rw = jnp.zeros((d, LANES), F32).at[:, :n_e].set(router)
    rb = jnp.full((1, LANES), -1e30, F32).at[0, :n_e].set(router_b)
    row = pl.BlockSpec((1, d), lambda i: (0, 0))
    return pl.pallas_call(
        _norm_router_body,
        grid=(m // tm,),
        in_specs=[pl.BlockSpec((tm, d), lambda i: (i, 0)), row, row, row,
                  pl.BlockSpec((d, LANES), lambda i: (0, 0)),
                  pl.BlockSpec((1, LANES), lambda i: (0, 0))],
        out_specs=[pl.BlockSpec((tm, d), lambda i: (i, 0)),
                   pl.BlockSpec((tm, LANES), lambda i: (i, 0))],
        out_shape=[jax.ShapeDtypeStruct((m, d), F32),
                   jax.ShapeDtypeStruct((m, LANES), F32)],
        compiler_params=_params("parallel"),
        name="norm_router",
    )(x, g, shift, scale, rw, rb)


def _silu(a):
    return a * jax.nn.sigmoid(a)


def _swiglu(h, w_gate, w_up, w_down, layer, x, gate_f):
    m, d = h.shape
    f = w_gate.shape[2]
    tm = _pick(m, 2048)
    tn = _pick(f, 256)
    wspec = pl.BlockSpec((None, d, tn), lambda i, j, kk: (layer, 0, j))
    hid = _matmul(
        "swiglu_up", h, [w_gate, w_up], m=m, n=f, k=d, tm=tm, tn=tn, tk=d,
        out_shape=jax.ShapeDtypeStruct((m, f), BF16),
        out_spec=pl.BlockSpec((tm, tn), lambda i, j, kk: (i, j)),
        a_spec=pl.BlockSpec((tm, d), lambda i, j, kk: (i, 0), pipeline_mode=pl.Buffered(1)),
        b_specs=[wspec, wspec],
        epilogue=lambda accs, ex: _silu(accs[0]) * accs[1])
    tm2 = _pick(m, 1024)
    tn2 = _pick(d, 256)
    return _matmul(
        "swiglu_down", hid, [w_down], m=m, n=d, k=f, tm=tm2, tn=tn2, tk=f,
        out_shape=jax.ShapeDtypeStruct((m, d), F32),
        out_spec=pl.BlockSpec((tm2, tn2), lambda i, j, kk: (i, j)),
        a_spec=pl.BlockSpec((tm2, f), lambda i, j, kk: (i, 0), pipeline_mode=pl.Buffered(1)),
        b_specs=[pl.BlockSpec((None, f, tn2), lambda i, j, kk: (layer, 0, j))],
        extras=[x, gate_f],
        extra_specs=[pl.BlockSpec((tm2, tn2), lambda i, j, kk: (i, j)),
                     pl.BlockSpec((1, tn2), lambda i, j, kk: (0, j))],
        epilogue=_residual_epilogue)


MOE_TILE = 512


def _moe_plan(sel, n_e, tile):
    l = sel.shape[0]
    e = jnp.concatenate([sel[:, 0], sel[:, 1]]).astype(jnp.int32)
    w = jnp.concatenate([sel[:, 2], sel[:, 3]])
    onehot = (e[:, None] == jnp.arange(n_e, dtype=jnp.int32)[None, :]).astype(jnp.int32)
    incl = jnp.cumsum(onehot, axis=0)
    counts = incl[-1]
    padded = ((counts + tile - 1) // tile) * tile
    ends = jnp.cumsum(padded)
    starts = ends - padded
    dest = jnp.sum(onehot * (starts[None, :] + incl - 1), axis=1)
    n_rows = 2 * l + n_e * tile
    n_tiles = n_rows // tile
    token = jnp.tile(jnp.arange(l, dtype=jnp.int32), 2)
    tok = jnp.zeros((n_rows,), jnp.int32).at[dest].set(token)
    wgt = jnp.zeros((n_rows,), F32).at[dest].set(w)
    n_active = ends[-1] // tile
    t_clamped = jnp.minimum(jnp.arange(n_tiles, dtype=jnp.int32), n_active - 1)
    tile_expert = jnp.sum((t_clamped[:, None] * tile >= ends[None, :]).astype(jnp.int32), axis=1)
    return dest, tok, wgt, tile_expert, n_active.reshape(1).astype(jnp.int32)


def _row_gather_body(idx_ref, src_hbm, o_ref, buf, sem, *, rows):
    r0 = pl.program_id(0) * rows

    def row_copy(r):
        return pltpu.make_async_copy(src_hbm.at[pl.ds(idx_ref[r0 + r], 1), :], buf.at[pl.ds(r, 1), :], sem)

    def issue(r, carry):
        row_copy(r).start()
        return carry

    def drain(r, carry):
        row_copy(r).wait()
        return carry

    lax.fori_loop(0, rows, issue, 0)
    lax.fori_loop(0, rows, drain, 0)
    o_ref[...] = buf[...].astype(o_ref.dtype)


def _row_gather(src, idx, out_dtype, *, rows=256):
    n, d = idx.shape[0], src.shape[1]
    rows = _pick(n, rows, SUBLANES)
    return pl.pallas_call(
        functools.partial(_row_gather_body, rows=rows),
        grid_spec=pltpu.PrefetchScalarGridSpec(
            num_scalar_prefetch=1,
            grid=(n // rows,),
            in_specs=[pl.BlockSpec(memory_space=pl.ANY)],
            out_specs=pl.BlockSpec((rows, d), lambda i, idx_ref: (i, 0)),
            scratch_shapes=[pltpu.VMEM((rows, d), src.dtype), pltpu.SemaphoreType.DMA(())]),
        out_shape=jax.ShapeDtypeStruct((n, d), out_dtype),
        compiler_params=_params("arbitrary"),
        name="moe_gather",
    )(idx, src)


def _grouped_body(te_ref, na_ref, a_ref, *refs, n_b, n_extra, epilogue):
    b_refs = refs[:n_b]
    extra_refs = refs[n_b:n_b + n_extra]
    o_ref = refs[n_b + n_extra]
    bf_refs = refs[n_b + n_extra + 1:]
    t = pl.program_id(1)
    active = t < na_ref[0]
    fresh = jnp.logical_or(t == 0, te_ref[t] != te_ref[jnp.maximum(t - 1, 0)])

    @pl.when(jnp.logical_and(active, fresh))
    def _():
        for b, bf in zip(b_refs, bf_refs):
            bf[...] = b[...].astype(BF16)

    @pl.when(active)
    def _():
        a = a_ref[...]
        accs = [jnp.dot(a, bf[...], preferred_element_type=F32) for bf in bf_refs]
        o_ref[...] = epilogue(accs, extra_refs).astype(o_ref.dtype)

    @pl.when(jnp.logical_not(active))
    def _():
        o_ref[...] = jnp.zeros_like(o_ref)


def _grouped_matmul(name, a, ws, layer, tile_expert, n_active, *, tile, tn, out_dtype, extras=(), epilogue=None):
    p, k = a.shape
    n = ws[0].shape[3]
    if epilogue is None:
        epilogue = lambda accs, ex: accs[0]
    row = lambda j, t, te, na: jnp.minimum(t, na[0] - 1)
    return pl.pallas_call(
        functools.partial(_grouped_body, n_b=len(ws), n_extra=len(extras), epilogue=epilogue),
        grid_spec=pltpu.PrefetchScalarGridSpec(
            num_scalar_prefetch=2,
            grid=(n // tn, p // tile),
            in_specs=[pl.BlockSpec((tile, k), lambda j, t, te, na: (row(j, t, te, na), 0)),
                      *[pl.BlockSpec((None, None, k, tn), lambda j, t, te, na: (layer, te[t], 0, j)) for _ in ws],
                      *[pl.BlockSpec((tile, e.shape[1]), lambda j, t, te, na: (row(j, t, te, na), 0))
                        for e in extras]],
            out_specs=pl.BlockSpec((tile, tn), lambda j, t, te, na: (t, j)),
            scratch_shapes=[pltpu.VMEM((k, tn), BF16) for _ in ws]),
        out_shape=jax.ShapeDtypeStruct((p, n), out_dtype),
        compiler_params=_params("arbitrary", "arbitrary"),
        name=name,
    )(tile_expert, n_active, a, *ws, *extras)


def _combine_body(pos_ref, y_hbm, x_ref, g_ref, o_ref, buf, sem, *, rows, n_tok):
    t0 = pl.program_id(0) * rows

    def row_copy(r, slot):
        return pltpu.make_async_copy(y_hbm.at[pl.ds(pos_ref[slot * n_tok + t0 + r], 1), :],
                                     buf.at[slot, pl.ds(r, 1), :], sem)

    def issue(r, carry):
        row_copy(r, 0).start()
        row_copy(r, 1).start()
        return carry

    def drain(r, carry):
        row_copy(r, 0).wait()
        row_copy(r, 1).wait()
        return carry

    lax.fori_loop(0, rows, issue, 0)
    lax.fori_loop(0, rows, drain, 0)
    o_ref[...] = x_ref[...] + g_ref[...] * (buf[0] + buf[1])


def _moe_combine(y, dest, x, gate_f, *, rows=256):
    l, d = x.shape
    rows = _pick(l, rows, SUBLANES)
    return pl.pallas_call(
        functools.partial(_combine_body, rows=rows, n_tok=l),
        grid_spec=pltpu.PrefetchScalarGridSpec(
            num_scalar_prefetch=1,
            grid=(l // rows,),
            in_specs=[pl.BlockSpec(memory_space=pl.ANY),
                      pl.BlockSpec((rows, d), lambda i, pos: (i, 0)),
                      pl.BlockSpec((1, d), lambda i, pos: (0, 0))],
            out_specs=pl.BlockSpec((rows, d), lambda i, pos: (i, 0)),
            scratch_shapes=[pltpu.VMEM((2, rows, d), F32), pltpu.SemaphoreType.DMA(())]),
        out_shape=jax.ShapeDtypeStruct((l, d), F32),
        compiler_params=_params("arbitrary"),
        name="moe_combine",
    )(dest, y, x, gate_f)


def _moe(h, sel, w_gate, w_up, w_down, layer, x, gate_f):
    _, n_e, d, fe = w_gate.shape
    tile = MOE_TILE
    dest, tok, wgt, tile_expert, n_active = _moe_plan(sel, n_e, tile)
    hs = _row_gather(h, tok, BF16)
    wrep = jnp.broadcast_to(wgt[:, None], (wgt.shape[0], LANES))
    tn = _pick(fe, 256)
    hid = _grouped_matmul(
        "moe_up", hs, [w_gate, w_up], layer, tile_expert, n_active, tile=tile, tn=tn, out_dtype=BF16,
        extras=[wrep],
        epilogue=lambda accs, ex: _silu(accs[0]) * accs[1] * jnp.concatenate([ex[0][...]] * (tn // LANES), axis=1))
    y = _grouped_matmul("moe_down", hid, [w_down], layer, tile_expert, n_active, tile=tile, tn=_pick(d, 1024),
                        out_dtype=F32)
    return _moe_combine(y, dest, x, gate_f)


def _rope_tables(n_tokens):
    pairs = DA_HEAD_DIM // 4
    rows = n_tokens // GRID_W
    row = jnp.repeat(jnp.arange(rows, dtype=F32), GRID_W)
    col = jnp.tile(jnp.arange(GRID_W, dtype=F32), rows)
    inv = ROPE_THETA ** (-jnp.arange(pairs, dtype=F32) / pairs)
    ang = jnp.concatenate([row[:, None] * inv, col[:, None] * inv], axis=-1)
    cos, sin = jnp.cos(ang), jnp.sin(ang)
    return jnp.concatenate([cos, cos], axis=-1), jnp.concatenate([-sin, sin], axis=-1)


def _qkv_rope(h, w_qkv, cos2, sin2, *, tm=1024, tn=512):
    m, d = h.shape
    n = w_qkv.shape[1]
    tm, tn = _pick(m, tm), _pick(d, tn)
    n_q = d // tn
    half = DA_HEAD_DIM // 2

    def epi(accs, ex):
        acc = accs[0]
        j = pl.program_id(1)
        cos, sin = ex[0][...], ex[1][...]
        heads = []
        for t in range(tn // DA_HEAD_DIM):
            xh = acc[:, t * DA_HEAD_DIM:(t + 1) * DA_HEAD_DIM]
            heads.append(xh * cos + pltpu.roll(xh, half, 1) * sin)
        roped = jnp.concatenate(heads, axis=1)
        qscale = jnp.where(j < n_q, DA_HEAD_DIM ** -0.5 * math.log2(math.e), 1.0).astype(F32)
        return jnp.where(j < 2 * n_q, roped * qscale, acc)

    tab = pl.BlockSpec((tm, DA_HEAD_DIM), lambda i, j, kk: (i, 0))
    return _matmul(
        "qkv_rope", h, [w_qkv], m=m, n=n, k=d, tm=tm, tn=tn, tk=d,
        out_shape=jax.ShapeDtypeStruct((m, n), BF16),
        out_spec=pl.BlockSpec((tm, tn), lambda i, j, kk: (i, j)),
        extras=[cos2, sin2], extra_specs=[tab, tab], epilogue=epi)


def _flash_body(lam_ref, g_ref, q_ref, kt_ref, vp_ref, vc_ref, o_ref,
                m_sc, l_sc, acc_sc, p0_sc, p1_sc, a0_sc, a1_sc, *, nkv, lambda_init):
    kv = pl.program_id(2)
    n_val = vp_ref.shape[1] // LANES

    @pl.when(kv == 0)
    def _():
        m_sc[...] = jnp.full_like(m_sc, -jnp.inf)
        l_sc[...] = jnp.zeros_like(l_sc)
        acc_sc[...] = jnp.zeros_like(acc_sc)
        p1_sc[...] = jnp.zeros_like(p1_sc)
        a1_sc[...] = jnp.ones_like(a1_sc)

    def accumulate(p_sc, a_sc, v):
        for c in range(2):
            alpha = jnp.concatenate([a_sc[c]] * n_val, axis=1)
            acc_sc[c] = acc_sc[c] * alpha + jnp.dot(p_sc[c], v, preferred_element_type=F32)

    def step(p_w, a_w, p_r, a_r):
        q = q_ref[...]
        kt = kt_ref[...]
        n_lane_tiles = kt.shape[1] // LANES
        for c in range(2):
            qc = q[:, c * DA_HEAD_DIM:(c + 1) * DA_HEAD_DIM]
            kc = kt[c * DA_HEAD_DIM:(c + 1) * DA_HEAD_DIM, :]
            s = jnp.dot(qc, kc, preferred_element_type=F32)
            tiles = [s[:, t * LANES:(t + 1) * LANES] for t in range(n_lane_tiles)]
            m_prev = m_sc[c]
            m_next = jnp.maximum(m_prev, jnp.max(functools.reduce(jnp.maximum, tiles), axis=1, keepdims=True))
            alpha = jnp.exp2(m_prev - m_next)
            ps = [jnp.exp2(t - m_next) for t in tiles]
            l_sc[c] = alpha * l_sc[c] + functools.reduce(jnp.add, ps)
            m_sc[c] = m_next
            a_w[c] = alpha
            p_w[c] = jnp.concatenate(ps, axis=1).astype(BF16)
        accumulate(p_r, a_r, vp_ref[...])

    @pl.when(kv % 2 == 0)
    def _():
        step(p0_sc, a0_sc, p1_sc, a1_sc)

    @pl.when(kv % 2 == 1)
    def _():
        step(p1_sc, a1_sc, p0_sc, a0_sc)

    @pl.when(kv == nkv - 1)
    def _():
        if (nkv - 1) % 2 == 0:
            accumulate(p0_sc, a0_sc, vc_ref[...])
        else:
            accumulate(p1_sc, a1_sc, vc_ref[...])
        lam = lam_ref[...]
        lam_full = (jnp.exp(jnp.sum(lam[0:1] * lam[1:2], axis=-1, keepdims=True))
                    - jnp.exp(jnp.sum(lam[2:3] * lam[3:4], axis=-1, keepdims=True)) + lambda_init)
        l0 = jnp.sum(l_sc[0], axis=1, keepdims=True)
        l1 = jnp.sum(l_sc[1], axis=1, keepdims=True)
        o = acc_sc[0] / l0 - lam_full * (acc_sc[1] / l1)
        o_ref[...] = (_norm_rows(o, g_ref[...]) * (1.0 - lambda_init)).astype(o_ref.dtype)


def _diff_flash(q, kt_all, v_all, lam, subln_g, lambda_init, *, tq=512, tk=768):
    l = q.shape[0]
    t, d = v_all.shape
    hw = 2 * DA_HEAD_DIM
    n_heads = d // hw
    tq = _pick(l, tq)
    tk = _pick(t, tk)
    nkv = t // tk
    return pl.pallas_call(
        functools.partial(_flash_body, nkv=nkv, lambda_init=lambda_init),
        grid=(n_heads, l // tq, nkv),
        in_specs=[pl.BlockSpec(lam.shape, lambda h, i, j: (0, 0)),
                  pl.BlockSpec((1, hw), lambda h, i, j: (0, 0)),
                  pl.BlockSpec((tq, hw), lambda h, i, j: (i, h)),
                  pl.BlockSpec((hw, tk), lambda h, i, j: (h, j)),
                  pl.BlockSpec((tk, hw), lambda h, i, j: (jnp.maximum(j - 1, 0), h)),
                  pl.BlockSpec((tk, hw), lambda h, i, j: (j, h))],
        out_specs=pl.BlockSpec((tq, hw), lambda h, i, j: (i, h)),
        out_shape=jax.ShapeDtypeStruct((l, d), BF16),
        scratch_shapes=[pltpu.VMEM((2, tq, LANES), F32), pltpu.VMEM((2, tq, LANES), F32),
                        pltpu.VMEM((2, tq, hw), F32),
                        pltpu.VMEM((2, tq, tk), BF16), pltpu.VMEM((2, tq, tk), BF16),
                        pltpu.VMEM((2, tq, LANES), F32), pltpu.VMEM((2, tq, LANES), F32)],
        compiler_params=_params("parallel", "parallel", "arbitrary"),
        name="diff_flash",
    )(lam, subln_g, q, kt_all, v_all, v_all)


def _diff_attention(h, hc, w_qkv, w_o, lam, subln_g, lambda_init, x, gate_a):
    l, d = h.shape
    cos2, sin2 = _rope_tables(l)
    qkv = _qkv_rope(h, w_qkv, cos2, sin2)
    c_len = hc.shape[0]
    tmc = _pick(c_len, 256)
    kvc = _matmul(
        "ctx_kv", hc, [w_qkv], m=c_len, n=2 * d, k=d, tm=tmc, tn=512, tk=d,
        out_shape=jax.ShapeDtypeStruct((c_len, 2 * d), BF16),
        out_spec=pl.BlockSpec((tmc, 512), lambda i, j, kk: (i, j)),
        b_specs=[pl.BlockSpec((d, 512), lambda i, j, kk: (0, j + d // 512))])
    kt_all = jnp.concatenate([qkv[:, d:2 * d], kvc[:, :d]], axis=0).T
    v_all = jnp.concatenate([qkv[:, 2 * d:], kvc[:, d:]], axis=0)
    o = _diff_flash(qkv, kt_all, v_all, lam, subln_g, lambda_init)
    return _proj_residual("attn_out", o, w_o, x, gate_a)


def _angles(num, den):
    return (2.0 * math.pi / den) * (num % den).astype(F32)


def _page_matrix(base):
    ko, po, s, pi = base.shape
    eye = jnp.eye(SUBLANES, dtype=F32)
    full = jnp.einsum("abcd,jk->abjcdk", base, eye)
    return full.reshape(ko * po * SUBLANES, s * pi * SUBLANES).astype(BF16)


def _page_mm_body(m_ref, x_ref, o_ref):
    x = x_ref[...]
    x2 = x.reshape(-1, x.shape[-1]).astype(BF16)
    y = jnp.dot(m_ref[...], x2, preferred_element_type=F32)
    o_ref[...] = y.reshape(o_ref.shape).astype(o_ref.dtype)


def _page_mm(mat, x4, ko, po, *, dc=512):
    s, pi, f, d = x4.shape
    dc = _pick(d, dc)
    return pl.pallas_call(
        _page_mm_body,
        grid=(f // SUBLANES, d // dc),
        in_specs=[pl.BlockSpec(mat.shape, lambda a, b: (0, 0)),
                  pl.BlockSpec((s, pi, SUBLANES, dc), lambda a, b: (0, 0, a, b))],
        out_specs=pl.BlockSpec((ko, po, SUBLANES, dc), lambda a, b: (0, 0, a, b)),
        out_shape=jax.ShapeDtypeStruct((ko, po, f, d), F32),
        compiler_params=_params("parallel", "parallel"),
        name="dft_lead",
    )(mat, x4)


def _short_conv_body(u0_ref, u1_ref, u2_ref, w0_ref, w1_ref, w2_ref, b0_ref, b1_ref, b2_ref,
                     x0_ref, p_ref, *, rows):
    n_rows = u0_ref.shape[0]
    n_chunks = n_rows // rows

    def conv(u_ref, w_ref, b_ref, r0, c):
        cur = u_ref[pl.ds(r0, rows), :]
        prev = u_ref[pl.ds(jnp.maximum(r0 - 1, 0), 1), :] * jnp.where(c > 0, 1.0, 0.0)
        nxt = u_ref[pl.ds(jnp.minimum(r0 + rows, n_rows - 1), 1), :] * jnp.where(c < n_chunks - 1, 1.0, 0.0)
        ridx = lax.broadcasted_iota(jnp.int32, cur.shape, 0)
        up = jnp.where(ridx == 0, prev, pltpu.roll(cur, 1, 0))
        down = jnp.where(ridx == rows - 1, nxt, pltpu.roll(cur, rows - 1, 0))
        w = w_ref[...]
        return w[0:1] * up + w[1:2] * cur + w[2:3] * down + b_ref[...]

    def step(c, carry):
        r0 = pl.multiple_of(c * rows, rows)
        x0_ref[pl.ds(r0, rows), :] = conv(u0_ref, w0_ref, b0_ref, r0, c)
        p_ref[pl.ds(r0, rows), :] = conv(u2_ref, w2_ref, b2_ref, r0, c) * conv(u1_ref, w1_ref, b1_ref, r0, c)
        return carry

    lax.fori_loop(0, n_chunks, step, 0)


def _short_conv_gate(u, conv_w, conv_b, *, dc=128, rows=512):
    l, d3 = u.shape
    d = d3 // 3
    nb = d // dc
    rows = _pick(l, rows, SUBLANES)
    us = [pl.BlockSpec((l, dc), lambda j, s=s: (0, j + s * nb)) for s in range(3)]
    ws = [pl.BlockSpec((3, dc), lambda j, s=s: (0, j + s * nb)) for s in range(3)]
    bs = [pl.BlockSpec((1, dc), lambda j, s=s: (0, j + s * nb)) for s in range(3)]
    out = pl.BlockSpec((l, dc), lambda j: (0, j))
    return pl.pallas_call(
        functools.partial(_short_conv_body, rows=rows),
        grid=(nb,),
        in_specs=[*us, *ws, *bs],
        out_specs=[out, out],
        out_shape=[jax.ShapeDtypeStruct((l, d), F32), jax.ShapeDtypeStruct((l, d), F32)],
        compiler_params=_params("parallel"),
        name="hyena_short_conv",
    )(u, u, u, conv_w, conv_w, conv_w, conv_b[None], conv_b[None], conv_b[None])


def _filter_body(z_ref, w_in_ref, b_in_ref, w_mid_ref, b_mid_ref, fq_ref, w_out_ref, dl_ref,
                 k_ref, ss_ref, *, n_inner):
    first = jnp.logical_and(pl.program_id(0) == 0, pl.program_id(1) == 0)

    @pl.when(first)
    def _():
        ss_ref[...] = jnp.zeros_like(ss_ref)

    z = z_ref[...]
    fq = fq_ref[...]
    hdn = jnp.sin(fq * (jnp.dot(z, w_in_ref[...], precision=HIGHEST, preferred_element_type=F32)
                        + b_in_ref[...]))
    for j in range(n_inner):
        hdn = jnp.sin(fq * (jnp.dot(hdn, w_mid_ref[j], precision=HIGHEST, preferred_element_type=F32)
                            + b_mid_ref[j:j + 1]))
    filt = jnp.dot(hdn, w_out_ref[...], precision=HIGHEST, preferred_element_type=F32)
    t = z[:, 0:1]
    valid = z[:, LANES - 1:LANES]
    kern = filt * jnp.exp(-t * dl_ref[...]) * valid
    k_ref[...] = kern
    ss_ref[...] += jnp.sum(kern * kern, axis=0, keepdims=True)


def _hyena_kernel(l, d, pe_w_in, pe_b_in, pe_w_mid, pe_b_mid, pe_w_out, sin_freq, *, tl=256):
    bands = (HY_EMB - 1) // 2
    hidden = pe_w_in.shape[1]
    n_inner = pe_w_mid.shape[0]
    pos = jnp.concatenate([jnp.arange(l), l - jnp.arange(l)]).astype(F32)
    valid = jnp.ones((2 * l,), F32).at[l].set(0.0)
    tt = pos / (l - 1)
    fr = jnp.linspace(1e-4, bands - 1, bands, dtype=F32)
    wpos = 2.0 * math.pi * pos[:, None] / l
    z = jnp.concatenate([tt[:, None], jnp.cos(fr * wpos), -jnp.sin(fr * wpos)], axis=-1)
    z = jnp.concatenate([z, jnp.zeros((2 * l, LANES - HY_EMB - 1), F32), valid[:, None]], axis=-1)
    w_in = jnp.zeros((LANES, hidden), F32).at[:HY_EMB].set(pe_w_in)
    max_decay = math.log(HY_DECAY_TARGET) / HY_FAST_DECAY
    min_decay = math.log(HY_DECAY_TARGET) / HY_SLOW_DECAY
    deltas = jnp.abs(jnp.linspace(min_decay, max_decay, d, dtype=F32))[None]
    tl = _pick(l, tl, SUBLANES)
    nt = l // tl
    const = lambda shape: pl.BlockSpec(shape, lambda a, i: tuple(0 for _ in shape))
    return pl.pallas_call(
        functools.partial(_filter_body, n_inner=n_inner),
        grid=(2, nt),
        in_specs=[pl.BlockSpec((tl, LANES), lambda a, i: (a * nt + i, 0)),
                  const((LANES, hidden)), const((1, hidden)),
                  const((n_inner, hidden, hidden)), const((n_inner, hidden)), const((1, hidden)),
                  pl.BlockSpec((hidden, d), lambda a, i: (0, a)),
                  const((1, d))],
        out_specs=[pl.BlockSpec((tl, d), lambda a, i: (a * nt + i, 0)),
                   pl.BlockSpec((1, d), lambda a, i: (0, 0))],
        out_shape=[jax.ShapeDtypeStruct((2 * l, d), F32), jax.ShapeDtypeStruct((1, d), F32)],
        compiler_params=_params("arbitrary", "arbitrary"),
        name="hyena_filter",
    )(z, w_in, pe_b_in[None], pe_w_mid, pe_b_mid, sin_freq[None], pe_w_out, deltas)


def _conv_tables(l):
    n = 2 * l
    f = DFT_F
    s = n // f
    ar = jnp.arange
    th = _angles(ar(s)[:, None] * ar(s)[None, :], s)
    c, sn = jnp.cos(th), jnp.sin(th)
    fwd = jnp.stack([c, -sn], axis=1)[:, :, :, None]
    p1_full = _page_matrix(fwd)
    p1_half = _page_matrix(fwd[:, :, :s // 2])
    inv = jnp.stack([c.T, -sn.T], axis=-1)[: s // 2, None] / n
    p2 = _page_matrix(inv)
    k = ar(s)[:, None, None] + s * ar(f)[None, :, None]
    ph = _angles(k * ar(f)[None, None, :], n)
    gr, gi = jnp.cos(ph), -jnp.sin(ph)
    g1 = jnp.concatenate([jnp.concatenate([gr, -gi], axis=2),
                          jnp.concatenate([gi, gr], axis=2)], axis=1).astype(BF16)
    hr, hi = jnp.swapaxes(gr, 1, 2), -jnp.swapaxes(gi, 1, 2)
    g2 = jnp.concatenate([jnp.concatenate([hr, -hi], axis=2),
                          jnp.concatenate([hi, hr], axis=2)], axis=1).astype(BF16)
    return p1_half, p1_full, p2, g1, g2


def _spectrum_body(g_ref, a_ref, o_ref):
    o_ref[...] = jnp.dot(g_ref[...], a_ref[...].astype(BF16), preferred_element_type=F32)


def _spectral_conv_body(g1_ref, g2_ref, a_ref, kf_ref, o_ref):
    x = jnp.dot(g1_ref[...], a_ref[...].astype(BF16), preferred_element_type=F32)
    kf = kf_ref[...]
    f = x.shape[0] // 2
    xr, xi, kr, ki = x[:f], x[f:], kf[:f], kf[f:]
    y = jnp.concatenate([xr * kr - xi * ki, xr * ki + xi * kr], axis=0).astype(BF16)
    o_ref[...] = jnp.dot(g2_ref[...], y, preferred_element_type=F32)


def _batched_stage(name, body, mats, arrays, *, dc=2048):
    nb, rows, d = arrays[0].shape
    dc = _pick(d, dc)
    mspecs = [pl.BlockSpec((None,) + m.shape[1:], lambda b, j: (b, 0, 0)) for m in mats]
    aspecs = [pl.BlockSpec((None, rows, dc), lambda b, j: (b, 0, j)) for _ in arrays]
    return pl.pallas_call(
        body,
        grid=(nb, d // dc),
        in_specs=[*mspecs, *aspecs],
        out_specs=pl.BlockSpec((None, mats[-1].shape[1], dc), lambda b, j: (b, 0, j)),
        out_shape=jax.ShapeDtypeStruct((nb, mats[-1].shape[1], d), F32),
        compiler_params=_params("parallel", "parallel"),
        name=name,
    )(*mats, *arrays)


def _conv_out_body(m_ref, b_ref, p_ref, x0_ref, ss_ref, bias_ref, o_ref):
    b = b_ref[...]
    b2 = b.reshape(-1, b.shape[-1]).astype(BF16)
    y = jnp.dot(m_ref[...], b2, preferred_element_type=F32).reshape(p_ref.shape)
    z = y * lax.rsqrt(ss_ref[...] + EPS) + p_ref[...] * bias_ref[...]
    o_ref[...] = (x0_ref[...] * z).astype(o_ref.dtype)


def _hyena(h, w_in, conv_w, conv_b, pe_w_in, pe_b_in, pe_w_mid, pe_b_mid, pe_w_out, sin_freq, bias,
           w_out, x, gate_a):
    l, d = h.shape
    f = DFT_F
    s = 2 * l // f
    tm, tn = _pick(l, 1024), _pick(3 * d, 512)
    u = _matmul("hyena_in", h, [w_in], m=l, n=3 * d, k=d, tm=tm, tn=tn, tk=d,
                out_shape=jax.ShapeDtypeStruct((l, 3 * d), F32),
                out_spec=pl.BlockSpec((tm, tn), lambda i, j, kk: (i, j)))
    x0c, p = _short_conv_gate(u, conv_w, conv_b)
    kern, ss = _hyena_kernel(l, d, pe_w_in, pe_b_in, pe_w_mid, pe_b_mid, pe_w_out, sin_freq)
    p1_half, p1_full, p2, g1, g2 = _conv_tables(l)
    ka = _page_mm(p1_full, kern.reshape(s, 1, f, d), s, 2)
    kf = _batched_stage("hyena_kernel_spectrum", _spectrum_body, [g1], [ka.reshape(s, 2 * f, d)])
    pa = _page_mm(p1_half, p.reshape(s // 2, 1, f, d), s, 2)
    pb = _batched_stage("hyena_spectral_conv", _spectral_conv_body, [g1, g2], [pa.reshape(s, 2 * f, d), kf])
    dc = _pick(d, 512)
    page = pl.BlockSpec((s // 2, SUBLANES, dc), lambda a, b: (0, a, b))
    chan = pl.BlockSpec((1, dc), lambda a, b: (0, b))
    y = pl.pallas_call(
        _conv_out_body,
        grid=(f // SUBLANES, d // dc),
        in_specs=[pl.BlockSpec(p2.shape, lambda a, b: (0, 0)),
                  pl.BlockSpec((s, 2, SUBLANES, dc), lambda a, b: (0, 0, a, b)),
                  page, page, chan, chan],
        out_specs=page,
        out_shape=jax.ShapeDtypeStruct((s // 2, f, d), BF16),
        compiler_params=_params("parallel", "parallel"),
        name="hyena_conv_out",
    )(p2, pb.reshape(s, 2, f, d), p.reshape(s // 2, f, d), x0c.reshape(s // 2, f, d), ss, bias[None])
    return _proj_residual("hyena_out", y.reshape(l, d), w_out, x, gate_a)


def _pool_body(h_ref, o_ref, pad_ref, *, rows, per_group, halo):
    n_rows = h_ref.shape[0]
    dc = h_ref.shape[1]
    pad_ref[pl.ds(0, halo), :] = jnp.zeros((halo, dc), F32)
    pad_ref[pl.ds(halo + n_rows, halo), :] = jnp.zeros((halo, dc), F32)
    pad_ref[pl.ds(halo, n_rows), :] = h_ref[...]
    group = pl.program_id(0) // per_group

    for g, w in enumerate(POOL_SIZES):
        @pl.when(group == g)
        def _(w=w):
            before, after = w // 2, w - w // 2

            def step(c, carry):
                r0 = pl.multiple_of(c * rows, rows)
                ext = pad_ref[pl.ds(r0, rows + 2 * halo), :]
                tot = ext[halo - before:halo - before + rows]
                for o in range(1 - before, after):
                    tot = tot + ext[halo + o:halo + o + rows]
                t = r0 + lax.broadcasted_iota(jnp.int32, (rows, dc), 0)
                cnt = jnp.minimum(t + after, n_rows) - jnp.maximum(t - before, 0)
                cur = ext[halo:halo + rows]
                o_ref[pl.ds(r0, rows), :] = (tot / cnt.astype(F32) - cur).astype(o_ref.dtype)
                return carry

            lax.fori_loop(0, n_rows // rows, step, 0)


def _pool_features(h, *, dc=256, rows=256):
    l, d = h.shape
    group = d // len(POOL_SIZES)
    dc = _pick(group, dc)
    rows = _pick(l, rows, SUBLANES)
    halo = max(POOL_SIZES) // 2
    return pl.pallas_call(
        functools.partial(_pool_body, rows=rows, per_group=group // dc, halo=halo),
        grid=(d // dc,),
        in_specs=[pl.BlockSpec((l, dc), lambda j: (0, j))],
        out_specs=pl.BlockSpec((l, dc), lambda j: (0, j)),
        out_shape=jax.ShapeDtypeStruct((l, d), BF16),
        scratch_shapes=[pltpu.VMEM((l + 2 * halo, dc), F32)],
        compiler_params=_params("parallel"),
        name="pool_features",
    )(h)


def _pool_mixer(h, w_groups, scale, x, gate_a, *, tm=1024, tn=512):
    l, d = h.shape
    n_g, group, _ = w_groups.shape
    pooled = _pool_features(h)
    tm, tn = _pick(l, tm), _pick(group, tn)
    per = group // tn
    return _matmul(
        "pool_proj", pooled, [w_groups], m=l, n=d, k=group, tm=tm, tn=tn, tk=group,
        out_shape=jax.ShapeDtypeStruct((l, d), F32),
        out_spec=pl.BlockSpec((tm, tn), lambda i, j, kk: (i, j)),
        a_spec=pl.BlockSpec((tm, group), lambda i, j, kk: (i, j // per)),
        b_specs=[pl.BlockSpec((None, group, tn), lambda i, j, kk: (j // per, 0, j % per))],
        extras=[x, gate_a, scale],
        extra_specs=[pl.BlockSpec((tm, tn), lambda i, j, kk: (i, j)),
                     pl.BlockSpec((1, tn), lambda i, j, kk: (0, j)),
                     pl.BlockSpec((1, tn), lambda i, j, kk: (0, j))],
        epilogue=lambda accs, ex: ex[0][...] + ex[1][...] * (accs[0] * ex[2][...]))


def _fourier_tables(l, group):
    f = DFT_F
    s = l // f
    ar = jnp.arange
    thc = _angles(ar(group)[:, None] * ar(group)[None, :], group)
    wc = jnp.concatenate([jnp.cos(thc), -jnp.sin(thc)], axis=1).astype(BF16)
    th = _angles(ar(s)[:, None] * ar(s)[None, :], s)
    c, sn = jnp.cos(th), jnp.sin(th)
    base = jnp.stack([jnp.stack([c, sn], axis=-1), jnp.stack([-sn, c], axis=-1)], axis=1)
    p1 = _page_matrix(base)
    k = ar(s)[:, None, None] + s * ar(f)[None, :, None]
    ph = _angles(k * ar(f)[None, None, :], l)
    g = jnp.concatenate([jnp.cos(ph), jnp.sin(ph)], axis=2)
    g = g.reshape(s // SUBLANES, SUBLANES, f, 2 * f)
    eye = jnp.eye(SUBLANES, dtype=F32)
    scat = jnp.einsum("qjkc,ji->qkjic", g, eye).reshape(s // SUBLANES, f * SUBLANES, SUBLANES * 2 * f)
    return wc, p1, scat.astype(BF16)


def _fourier_out_body(m_ref, a_ref, o_ref, *, scale):
    y = jnp.dot(m_ref[...], a_ref[...].astype(BF16), preferred_element_type=F32) * scale
    o_ref[...] = y.reshape(o_ref.shape)


def _fourier_mixer(h, w_out, x, gate_a, *, tm=1024, tn=512, dc=1024):
    l, d = h.shape
    group = d // FN_GROUPS
    f = DFT_F
    s = l // f
    wc, p1, scat = _fourier_tables(l, group)
    tm, tn = _pick(l, tm, f), _pick(group, tn)
    per = group // tn
    z = _matmul(
        "fourier_chan", h, [wc], m=l, n=2 * d, k=group, tm=tm, tn=tn, tk=group,
        out_shape=jax.ShapeDtypeStruct((s, 2, f, d), F32),
        out_spec=pl.BlockSpec((tm // f, None, f, tn),
                              lambda i, j, kk: (i, (j // per) % 2, 0, (j // (2 * per)) * per + j % per)),
        a_spec=pl.BlockSpec((tm, group), lambda i, j, kk: (i, j // (2 * per))),
        b_specs=[pl.BlockSpec((group, tn), lambda i, j, kk: (0, j % (2 * per)))])
    a2 = _page_mm(p1, z, s, 2)
    dc = _pick(d, dc)
    q = s // SUBLANES
    mixed = pl.pallas_call(
        functools.partial(_fourier_out_body, scale=1.0 / math.sqrt(l * group)),
        grid=(q, d // dc),
        in_specs=[pl.BlockSpec((None,) + scat.shape[1:], lambda a, b: (a, 0, 0)),
                  pl.BlockSpec((None, SUBLANES * 2 * f, dc), lambda a, b: (a, 0, b))],
        out_specs=pl.BlockSpec((f, None, SUBLANES, dc), lambda a, b: (0, a, 0, b)),
        out_shape=jax.ShapeDtypeStruct((f, q, SUBLANES, d), F32),
        compiler_params=_params("parallel", "parallel"),
        name="fourier_seq_out",
    )(scat, a2.reshape(q, SUBLANES * 2 * f, d))
    return _proj_residual("fourier_out", mixed.reshape(l, d), w_out, x, gate_a, tm=512)


def kernel(x, c, ctx, c_ctx, ada_down, ada_up, ada_b, norm_g, final_g, da_w_qkv, da_w_o, da_lambda, da_subln_g, hy_w_in, hy_conv_w, hy_conv_b, hy_pe_w_in, hy_pe_b_in, hy_pe_w_mid, hy_pe_b_mid, hy_pe_w_out, hy_sin_freq, hy_bias, hy_w_out, pool_w, pool_scale, fn_w_out, ffn_w_gate, ffn_w_up, ffn_w_down, moe_router, moe_router_b, moe_w_gate, moe_w_up, moe_w_down):
    batch, l, d = x.shape
    depth = ada_down.shape[0]
    assert batch == 1
    xs = x[0]
    mod = _adaln_all(c, c_ctx, ada_down, ada_up, ada_b)
    ffn_w_down_bf = ffn_w_down.astype(BF16)
    for i in range(depth):
        shift_a, scale_a, gate_a, shift_f, scale_f, gate_f = [
            mod[i, 0:1, n * d:(n + 1) * d] for n in range(N_MOD)]
        g_a, g_f = norm_g[i, 0][None], norm_g[i, 1][None]
        mixer = i % 4
        if mixer == 0:
            h = _norm_mod(xs, g_a, shift_a, scale_a, BF16)
            hc = _norm_mod(ctx[0], g_a, mod[i, 1:2, 0:d], mod[i, 1:2, d:2 * d], BF16)
            xs = _diff_attention(h, hc, da_w_qkv, da_w_o, da_lambda, da_subln_g[None],
                                 0.8 - 0.6 * math.exp(-0.3 * i), xs, gate_a)
        elif mixer == 1:
            h = _norm_mod(xs, g_a, shift_a, scale_a, BF16)
            xs = _hyena(h, hy_w_in, hy_conv_w, hy_conv_b, hy_pe_w_in, hy_pe_b_in, hy_pe_w_mid,
                        hy_pe_b_mid, hy_pe_w_out, hy_sin_freq, hy_bias, hy_w_out, xs, gate_a)
        elif mixer == 2:
            h = _norm_mod(xs, g_a, shift_a, scale_a, F32)
            xs = _pool_mixer(h, pool_w, pool_scale[None], xs, gate_a)
        else:
            h = _norm_mod(xs, g_a, shift_a, scale_a, BF16)
            xs = _fourier_mixer(h, fn_w_out, xs, gate_a)
        if i % 2 == 0:
            h = _norm_mod(xs, g_f, shift_f, scale_f, BF16)
            xs = _swiglu(h, ffn_w_gate, ffn_w_up, ffn_w_down_bf, i // 2, xs, gate_f)
        else:
            j = i // 2
            h, sel = _norm_router(xs, g_f, shift_f, scale_f, moe_router[j], moe_router_b[j])
            xs = _moe(h, sel, moe_w_gate, moe_w_up, moe_w_down, j, xs, gate_f)
    return _rmsnorm(xs, final_g[None])[None]
```

```python
import functools
import math

import jax
import jax.numpy as jnp
from jax import lax
from jax.experimental import pallas as pl
from jax.experimental.pallas import tpu as pltpu

F32 = jnp.float32
BF16 = jnp.bfloat16
HIGHEST = lax.Precision.HIGHEST

EPS = 1e-6
GRID_W = 64
N_MOD = 6
DA_HEADS = 16
DA_HEAD_DIM = 128
ROPE_THETA = 10000.0
HY_EMB = 33
HY_DECAY_TARGET = 1e-2
HY_FAST_DECAY = 0.3
HY_SLOW_DECAY = 1.5
POOL_SIZES = (2, 4, 8, 16)
FN_GROUPS = 4
TOP_K = 2

LANES = 128
SUBLANES = 8
DFT_F = 128
VMEM_LIMIT = 56 * 1024 * 1024


def _params(*sem):
    return pltpu.CompilerParams(dimension_semantics=sem, vmem_limit_bytes=VMEM_LIMIT)


def _pick(dim, pref, align=LANES):
    if dim <= pref:
        return dim
    t = (pref // align) * align
    while t >= align:
        if dim % t == 0:
            return t
        t -= align
    raise ValueError(f"no {align}-aligned tile of {dim} below {pref}")


def _mm_body(*refs, n_b, n_extra, nk, epilogue):
    a_ref = refs[0]
    b_refs = refs[1:1 + n_b]
    extra_refs = refs[1 + n_b:1 + n_b + n_extra]
    o_ref = refs[1 + n_b + n_extra]
    acc_refs = refs[2 + n_b + n_extra:]
    a = a_ref[...].astype(BF16)
    if nk == 1:
        accs = [jnp.dot(a, b[...].astype(BF16), preferred_element_type=F32) for b in b_refs]
        o_ref[...] = epilogue(accs, extra_refs).reshape(o_ref.shape).astype(o_ref.dtype)
        return
    k = pl.program_id(2)

    @pl.when(k == 0)
    def _():
        for acc in acc_refs:
            acc[...] = jnp.zeros_like(acc)

    for acc, b in zip(acc_refs, b_refs):
        acc[...] += jnp.dot(a, b[...].astype(BF16), preferred_element_type=F32)

    @pl.when(k == nk - 1)
    def _():
        accs = [acc[...] for acc in acc_refs]
        o_ref[...] = epilogue(accs, extra_refs).reshape(o_ref.shape).astype(o_ref.dtype)


def _matmul(name, a, bs, *, m, n, k, tm, tn, tk, out_shape, out_spec, a_spec=None, b_specs=None,
            extras=(), extra_specs=(), epilogue=None):
    nk = k // tk
    if epilogue is None:
        epilogue = lambda accs, ex: accs[0]
    if a_spec is None:
        a_spec = pl.BlockSpec((tm, tk), lambda i, j, kk: (i, kk))
    if b_specs is None:
        b_specs = [pl.BlockSpec((tk, tn), lambda i, j, kk: (kk, j)) for _ in bs]
    scratch = [pltpu.VMEM((tm, tn), F32) for _ in bs] if nk > 1 else []
    body = functools.partial(_mm_body, n_b=len(bs), n_extra=len(extras), nk=nk, epilogue=epilogue)
    return pl.pallas_call(
        body,
        grid=(m // tm, n // tn, nk),
        in_specs=[a_spec, *b_specs, *extra_specs],
        out_specs=out_spec,
        out_shape=out_shape,
        scratch_shapes=scratch,
        compiler_params=_params("parallel", "parallel", "arbitrary"),
        name=name,
    )(a, *bs, *extras)


def _residual_epilogue(accs, ex):
    return ex[0][...] + ex[1][...] * accs[0]


def _proj_residual(name, a, w, x, gate, *, tm=1024, tn=512):
    m, k = a.shape
    n = w.shape[1]
    tm, tn = _pick(m, tm), _pick(n, tn)
    return _matmul(
        name, a, [w], m=m, n=n, k=k, tm=tm, tn=tn, tk=k,
        out_shape=jax.ShapeDtypeStruct((m, n), F32),
        out_spec=pl.BlockSpec((tm, tn), lambda i, j, kk: (i, j)),
        extras=[x, gate],
        extra_specs=[pl.BlockSpec((tm, tn), lambda i, j, kk: (i, j)),
                     pl.BlockSpec((1, tn), lambda i, j, kk: (0, j))],
        epilogue=_residual_epilogue)


def _dot_split(a, b):
    a_hi, b_hi = a.astype(BF16), b.astype(BF16)
    a_lo = (a - a_hi.astype(F32)).astype(BF16)
    b_lo = (b - b_hi.astype(F32)).astype(BF16)
    dot = functools.partial(jnp.dot, preferred_element_type=F32)
    return dot(a_hi, b_hi) + (dot(a_hi, b_lo) + dot(a_lo, b_hi))


def _ada_body(a_ref, w_ref, b_ref, o_ref, *, silu):
    a = a_ref[...]
    if silu:
        a = a * jax.nn.sigmoid(a)
    o_ref[...] = _dot_split(a, w_ref[...]) + b_ref[...]


def _adaln_all(c, c_ctx, ada_down, ada_up, ada_b):
    depth, d, r = ada_down.shape
    width = ada_up.shape[2]
    rows = jnp.zeros((SUBLANES, d), F32).at[0].set(c[0]).at[1].set(c_ctx)
    tn1 = _pick(r, 256)
    t = pl.pallas_call(
        functools.partial(_ada_body, silu=True),
        grid=(depth, r // tn1),
        in_specs=[pl.BlockSpec((SUBLANES, d), lambda l, j: (0, 0)),
                  pl.BlockSpec((None, d, tn1), lambda l, j: (l, 0, j)),
                  pl.BlockSpec((1, tn1), lambda l, j: (0, j))],
        out_specs=pl.BlockSpec((None, SUBLANES, tn1), lambda l, j: (l, 0, j)),
        out_shape=jax.ShapeDtypeStruct((depth, SUBLANES, r), F32),
        compiler_params=_params("parallel", "parallel"),
        name="adaln_down",
    )(rows, ada_down, jnp.zeros((1, r), F32))
    tn2 = _pick(width, 2048)
    return pl.pallas_call(
        functools.partial(_ada_body, silu=False),
        grid=(depth, width // tn2),
        in_specs=[pl.BlockSpec((None, SUBLANES, r), lambda l, j: (l, 0, 0)),
                  pl.BlockSpec((None, r, tn2), lambda l, j: (l, 0, j)),
                  pl.BlockSpec((None, 1, tn2), lambda l, j: (l, 0, j))],
        out_specs=pl.BlockSpec((None, SUBLANES, tn2), lambda l, j: (l, 0, j)),
        out_shape=jax.ShapeDtypeStruct((depth, SUBLANES, width), F32),
        compiler_params=_params("parallel", "parallel"),
        name="adaln_up",
    )(t, ada_up, ada_b[:, None, :])


def _norm_rows(x, g):
    return x * lax.rsqrt(jnp.mean(x * x, axis=-1, keepdims=True) + EPS) * g


def _norm_mod_body(x_ref, g_ref, shift_ref, scale_ref, o_ref):
    y = _norm_rows(x_ref[...], g_ref[...])
    o_ref[...] = (y * (1.0 + scale_ref[...]) + shift_ref[...]).astype(o_ref.dtype)


def _norm_mod(x, g, shift, scale, out_dtype, *, tm=256):
    m, d = x.shape
    tm = _pick(m, tm, SUBLANES)
    row = pl.BlockSpec((1, d), lambda i: (0, 0))
    return pl.pallas_call(
        _norm_mod_body,
        grid=(m // tm,),
        in_specs=[pl.BlockSpec((tm, d), lambda i: (i, 0)), row, row, row],
        out_specs=pl.BlockSpec((tm, d), lambda i: (i, 0)),
        out_shape=jax.ShapeDtypeStruct((m, d), out_dtype),
        compiler_params=_params("parallel"),
        name="norm_mod",
    )(x, g, shift, scale)


def _rmsnorm_body(x_ref, g_ref, o_ref):
    o_ref[...] = _norm_rows(x_ref[...], g_ref[...])


def _rmsnorm(x, g, *, tm=256):
    m, d = x.shape
    tm = _pick(m, tm, SUBLANES)
    return pl.pallas_call(
        _rmsnorm_body,
        grid=(m // tm,),
        in_specs=[pl.BlockSpec((tm, d), lambda i: (i, 0)), pl.BlockSpec((1, d), lambda i: (0, 0))],
        out_specs=pl.BlockSpec((tm, d), lambda i: (i, 0)),
        out_shape=jax.ShapeDtypeStruct((m, d), F32),
        compiler_params=_params("parallel"),
        name="final_norm",
    )(x, g)


def _norm_router_body(x_ref, g_ref, shift_ref, scale_ref, rw_ref, rb_ref, h_ref, sel_ref):
    h = _norm_rows(x_ref[...], g_ref[...]) * (1.0 + scale_ref[...]) + shift_ref[...]
    h_ref[...] = h
    logits = jnp.dot(h, rw_ref[...], precision=HIGHEST, preferred_element_type=F32) + rb_ref[...]
    lane = lax.broadcasted_iota(jnp.int32, logits.shape, 1)
    m1 = jnp.max(logits, axis=-1, keepdims=True)
    i1 = jnp.min(jnp.where(logits == m1, lane, LANES), axis=-1, keepdims=True)
    rest = jnp.where(lane == i1, -jnp.inf, logits)
    m2 = jnp.max(rest, axis=-1, keepdims=True)
    i2 = jnp.min(jnp.where(rest == m2, lane, LANES), axis=-1, keepdims=True)
    e = jnp.exp(m2 - m1)
    w1 = 1.0 / (1.0 + e)
    w2 = e * w1
    sel_ref[...] = jnp.where(lane == 0, i1.astype(F32),
                             jnp.where(lane == 1, i2.astype(F32),
                                       jnp.where(lane == 2, w1, jnp.where(lane == 3, w2, 0.0))))


def _norm_router(x, g, shift, scale, router, router_b, *, tm=256):
    m, d = x.shape
    n_e = router.shape[1]
    tm = _pick(m, tm, SUBLANES)
    rw = jnp.zeros((d, LANES), F32).at[:, :n_e].set(router)
    rb = jnp.full((1, LANES), -1e30, F32).at[0, :n_e].set(router_b)
    row = pl.BlockSpec((1, d), lambda i: (0, 0))
    return pl.pallas_call(
        _norm_router_body,
        grid=(m // tm,),
        in_specs=[pl.BlockSpec((tm, d), lambda i: (i, 0)), row, row, row,
                  pl.BlockSpec((d, LANES), lambda i: (0, 0)),
                  pl.BlockSpec((1, LANES), lambda i: (0, 0))],
        out_specs=[pl.BlockSpec((tm, d), lambda i: (i, 0)),
                   pl.BlockSpec((tm, LANES), lambda i: (i, 0))],
        out_shape=[jax.ShapeDtypeStruct((m, d), F32),
                   jax.ShapeDtypeStruct((m, LANES), F32)],
        compiler_params=_params("parallel"),
        name="norm_router",
    )(x, g, shift, scale, rw, rb)


def _silu(a):
    return a * jax.nn.sigmoid(a)


def _swiglu(h, w_gate, w_up, w_down, layer, x, gate_f):
    m, d = h.shape
    f = w_gate.shape[2]
    tm = _pick(m, 2048)
    tn = _pick(f, 256)
    wspec = pl.BlockSpec((None, d, tn), lambda i, j, kk: (layer, 0, j))
    hid = _matmul(
        "swiglu_up", h, [w_gate, w_up], m=m, n=f, k=d, tm=tm, tn=tn, tk=d,
        out_shape=jax.ShapeDtypeStruct((m, f), BF16),
        out_spec=pl.BlockSpec((tm, tn), lambda i, j, kk: (i, j)),
        a_spec=pl.BlockSpec((tm, d), lambda i, j, kk: (i, 0), pipeline_mode=pl.Buffered(1)),
        b_specs=[wspec, wspec],
        epilogue=lambda accs, ex: _silu(accs[0]) * accs[1])
    tm2 = _pick(m, 1024)
    tn2 = _pick(d, 256)
    return _matmul(
        "swiglu_down", hid, [w_down], m=m, n=d, k=f, tm=tm2, tn=tn2, tk=f,
        out_shape=jax.ShapeDtypeStruct((m, d), F32),
        out_spec=pl.BlockSpec((tm2, tn2), lambda i, j, kk: (i, j)),
        a_spec=pl.BlockSpec((tm2, f), lambda i, j, kk: (i, 0), pipeline_mode=pl.Buffered(1)),
        b_specs=[pl.BlockSpec((None, f, tn2), lambda i, j, kk: (layer, 0, j))],
        extras=[x, gate_f],
        extra_specs=[pl.BlockSpec((tm2, tn2), lambda i, j, kk: (i, j)),
                     pl.BlockSpec((1, tn2), lambda i, j, kk: (0, j))],
        epilogue=_residual_epilogue)


MOE_TILE = 512


def _moe_plan(sel, n_e, tile):
    l = sel.shape[0]
    e = jnp.concatenate([sel[:, 0], sel[:, 1]]).astype(jnp.int32)
    onehot = (e[:, None] == jnp.arange(n_e, dtype=jnp.int32)[None, :]).astype(jnp.int32)
    incl = jnp.cumsum(onehot, axis=0)
    counts = incl[-1]
    padded = ((counts + tile - 1) // tile) * tile
    ends = jnp.cumsum(padded)
    starts = ends - padded
    dest = jnp.sum(onehot * (starts[None, :] + incl - 1), axis=1)
    n_rows = 2 * l + n_e * tile
    n_tiles = n_rows // tile
    token = jnp.tile(jnp.arange(l, dtype=jnp.int32), 2)
    tok = jnp.zeros((n_rows,), jnp.int32).at[dest].set(token)
    n_active = ends[-1] // tile
    t_clamped = jnp.minimum(jnp.arange(n_tiles, dtype=jnp.int32), n_active - 1)
    tile_expert = jnp.sum((t_clamped[:, None] * tile >= ends[None, :]).astype(jnp.int32), axis=1)
    return dest, tok, tile_expert, n_active.reshape(1).astype(jnp.int32)


def _prefetched_rows(copies, n_rows, n_steps):
    i = pl.program_id(0)
    slot = i % 2

    def issue(step, into):
        def body(r, carry):
            for cp in copies(step, r, into):
                cp.start()
            return carry
        lax.fori_loop(0, n_rows, body, 0, unroll=8)

    def drain(r, carry):
        for cp in copies(i, r, slot):
            cp.wait()
        return carry

    @pl.when(i == 0)
    def _():
        issue(0, 0)

    @pl.when(i + 1 < n_steps)
    def _():
        issue(i + 1, 1 - slot)

    lax.fori_loop(0, n_rows, drain, 0, unroll=8)
    return slot


def _row_gather_body(idx_ref, src_hbm, o_ref, buf, sem, *, rows, n_steps):
    def copies(step, r, slot):
        return [pltpu.make_async_copy(src_hbm.at[pl.ds(idx_ref[step * rows + r], 1), :],
                                      buf.at[slot, pl.ds(r, 1), :], sem.at[slot])]

    slot = _prefetched_rows(copies, rows, n_steps)
    o_ref[...] = buf[slot].astype(o_ref.dtype)


def _row_gather(src, idx, out_dtype, *, rows=256):
    n, d = idx.shape[0], src.shape[1]
    rows = _pick(n, rows, SUBLANES)
    n_steps = n // rows
    return pl.pallas_call(
        functools.partial(_row_gather_body, rows=rows, n_steps=n_steps),
        grid_spec=pltpu.PrefetchScalarGridSpec(
            num_scalar_prefetch=1,
            grid=(n_steps,),
            in_specs=[pl.BlockSpec(memory_space=pl.ANY)],
            out_specs=pl.BlockSpec((rows, d), lambda i, idx_ref: (i, 0)),
            scratch_shapes=[pltpu.VMEM((2, rows, d), src.dtype), pltpu.SemaphoreType.DMA((2,))]),
        out_shape=jax.ShapeDtypeStruct((n, d), out_dtype),
        compiler_params=_params("arbitrary"),
        name="moe_gather",
    )(idx, src)


def _grouped_body(te_ref, na_ref, a_ref, *refs, n_b, n_extra, epilogue):
    b_refs = refs[:n_b]
    extra_refs = refs[n_b:n_b + n_extra]
    o_ref = refs[n_b + n_extra]
    bf_refs = refs[n_b + n_extra + 1:]
    t = pl.program_id(1)
    active = t < na_ref[0]
    fresh = jnp.logical_or(t == 0, te_ref[t] != te_ref[jnp.maximum(t - 1, 0)])

    @pl.when(jnp.logical_and(active, fresh))
    def _():
        for b, bf in zip(b_refs, bf_refs):
            bf[...] = b[...].astype(BF16)

    @pl.when(active)
    def _():
        a = a_ref[...]
        accs = [jnp.dot(a, bf[...], preferred_element_type=F32) for bf in bf_refs]
        o_ref[...] = epilogue(accs, extra_refs).astype(o_ref.dtype)

    @pl.when(jnp.logical_not(active))
    def _():
        o_ref[...] = jnp.zeros_like(o_ref)


def _grouped_matmul(name, a, ws, layer, tile_expert, n_active, *, tile, tn, out_dtype, extras=(), epilogue=None):
    p, k = a.shape
    n = ws[0].shape[3]
    if epilogue is None:
        epilogue = lambda accs, ex: accs[0]
    row = lambda j, t, te, na: jnp.minimum(t, na[0] - 1)
    return pl.pallas_call(
        functools.partial(_grouped_body, n_b=len(ws), n_extra=len(extras), epilogue=epilogue),
        grid_spec=pltpu.PrefetchScalarGridSpec(
            num_scalar_prefetch=2,
            grid=(n // tn, p // tile),
            in_specs=[pl.BlockSpec((tile, k), lambda j, t, te, na: (row(j, t, te, na), 0)),
                      *[pl.BlockSpec((None, None, k, tn), lambda j, t, te, na: (layer, te[t], 0, j)) for _ in ws],
                      *[pl.BlockSpec((tile, e.shape[1]), lambda j, t, te, na: (row(j, t, te, na), 0))
                        for e in extras]],
            out_specs=pl.BlockSpec((tile, tn), lambda j, t, te, na: (t, j)),
            scratch_shapes=[pltpu.VMEM((k, tn), BF16) for _ in ws]),
        out_shape=jax.ShapeDtypeStruct((p, n), out_dtype),
        compiler_params=_params("arbitrary", "arbitrary"),
        name=name,
    )(tile_expert, n_active, a, *ws, *extras)


def _combine_body(pos_ref, y_hbm, x_ref, g_ref, sel_ref, o_ref, buf, sem, *, rows, n_tok, n_steps):
    def copies(step, r, slot):
        return [pltpu.make_async_copy(y_hbm.at[pl.ds(pos_ref[k * n_tok + step * rows + r], 1), :],
                                      buf.at[slot, k, pl.ds(r, 1), :], sem.at[slot]) for k in range(TOP_K)]

    slot = _prefetched_rows(copies, rows, n_steps)
    sel = sel_ref[...]
    y = sel[:, 2:3] * buf[slot, 0] + sel[:, 3:4] * buf[slot, 1]
    o_ref[...] = x_ref[...] + g_ref[...] * y


def _moe_combine(y, dest, sel, x, gate_f, *, rows=256):
    l, d = x.shape
    rows = _pick(l, rows, SUBLANES)
    n_steps = l // rows
    return pl.pallas_call(
        functools.partial(_combine_body, rows=rows, n_tok=l, n_steps=n_steps),
        grid_spec=pltpu.PrefetchScalarGridSpec(
            num_scalar_prefetch=1,
            grid=(n_steps,),
            in_specs=[pl.BlockSpec(memory_space=pl.ANY),
                      pl.BlockSpec((rows, d), lambda i, pos: (i, 0)),
                      pl.BlockSpec((1, d), lambda i, pos: (0, 0)),
                      pl.BlockSpec((rows, LANES), lambda i, pos: (i, 0))],
            out_specs=pl.BlockSpec((rows, d), lambda i, pos: (i, 0)),
            scratch_shapes=[pltpu.VMEM((2, TOP_K, rows, d), F32), pltpu.SemaphoreType.DMA((2,))]),
        out_shape=jax.ShapeDtypeStruct((l, d), F32),
        compiler_params=_params("arbitrary"),
        name="moe_combine",
    )(dest, y, x, gate_f, sel)


def _moe(h, sel, w_gate, w_up, w_down, layer, x, gate_f):
    _, n_e, d, fe = w_gate.shape
    tile = MOE_TILE
    dest, tok, tile_expert, n_active = _moe_plan(sel, n_e, tile)
    hs = _row_gather(h, tok, BF16)
    hid = _grouped_matmul(
        "moe_up", hs, [w_gate, w_up], layer, tile_expert, n_active, tile=tile, tn=_pick(fe, 256),
        out_dtype=BF16, epilogue=lambda accs, ex: _silu(accs[0]) * accs[1])
    y = _grouped_matmul("moe_down", hid, [w_down], layer, tile_expert, n_active, tile=tile, tn=_pick(d, 1024),
                        out_dtype=F32)
    return _moe_combine(y, dest, sel, x, gate_f)


def _rope_tables(n_tokens):
    pairs = DA_HEAD_DIM // 4
    rows = n_tokens // GRID_W
    row = jnp.repeat(jnp.arange(rows, dtype=F32), GRID_W)
    col = jnp.tile(jnp.arange(GRID_W, dtype=F32), rows)
    inv = ROPE_THETA ** (-jnp.arange(pairs, dtype=F32) / pairs)
    ang = jnp.concatenate([row[:, None] * inv, col[:, None] * inv], axis=-1)
    cos, sin = jnp.cos(ang), jnp.sin(ang)
    return jnp.concatenate([cos, cos], axis=-1), jnp.concatenate([-sin, sin], axis=-1)


def _qkv_rope(h, w_qkv, cos2, sin2, *, tm=1024, tn=512):
    m, d = h.shape
    n = w_qkv.shape[1]
    tm, tn = _pick(m, tm), _pick(d, tn)
    n_q = d // tn
    half = DA_HEAD_DIM // 2

    def epi(accs, ex):
        acc = accs[0]
        j = pl.program_id(1)
        cos, sin = ex[0][...], ex[1][...]
        heads = []
        for t in range(tn // DA_HEAD_DIM):
            xh = acc[:, t * DA_HEAD_DIM:(t + 1) * DA_HEAD_DIM]
            heads.append(xh * cos + pltpu.roll(xh, half, 1) * sin)
        roped = jnp.concatenate(heads, axis=1)
        qscale = jnp.where(j < n_q, DA_HEAD_DIM ** -0.5 * math.log2(math.e), 1.0).astype(F32)
        return jnp.where(j < 2 * n_q, roped * qscale, acc)

    tab = pl.BlockSpec((tm, DA_HEAD_DIM), lambda i, j, kk: (i, 0))
    return _matmul(
        "qkv_rope", h, [w_qkv], m=m, n=n, k=d, tm=tm, tn=tn, tk=d,
        out_shape=jax.ShapeDtypeStruct((m, n), BF16),
        out_spec=pl.BlockSpec((tm, tn), lambda i, j, kk: (i, j)),
        extras=[cos2, sin2], extra_specs=[tab, tab], epilogue=epi)


def _flash_body(lam_ref, g_ref, q_ref, kt_ref, vp_ref, vc_ref, o_ref,
                m_sc, l_sc, acc_sc, p0_sc, p1_sc, a0_sc, a1_sc, *, nkv, lambda_init):
    kv = pl.program_id(2)
    n_val = vp_ref.shape[1] // LANES

    @pl.when(kv == 0)
    def _():
        m_sc[...] = jnp.full_like(m_sc, -jnp.inf)
        l_sc[...] = jnp.zeros_like(l_sc)
        acc_sc[...] = jnp.zeros_like(acc_sc)
        p1_sc[...] = jnp.zeros_like(p1_sc)
        a1_sc[...] = jnp.ones_like(a1_sc)

    def accumulate(p_sc, a_sc, v):
        for c in range(2):
            alpha = jnp.concatenate([a_sc[c]] * n_val, axis=1)
            acc_sc[c] = acc_sc[c] * alpha + jnp.dot(p_sc[c], v, preferred_element_type=F32)

    def step(p_w, a_w, p_r, a_r):
        q = q_ref[...]
        kt = kt_ref[...]
        n_lane_tiles = kt.shape[1] // LANES
        for c in range(2):
            qc = q[:, c * DA_HEAD_DIM:(c + 1) * DA_HEAD_DIM]
            kc = kt[c * DA_HEAD_DIM:(c + 1) * DA_HEAD_DIM, :]
            s = jnp.dot(qc, kc, preferred_element_type=F32)
            tiles = [s[:, t * LANES:(t + 1) * LANES] for t in range(n_lane_tiles)]
            m_prev = m_sc[c]
            m_next = jnp.maximum(m_prev, jnp.max(functools.reduce(jnp.maximum, tiles), axis=1, keepdims=True))
            alpha = jnp.exp2(m_prev - m_next)
            ps = [jnp.exp2(t - m_next) for t in tiles]
            l_sc[c] = alpha * l_sc[c] + functools.reduce(jnp.add, ps)
            m_sc[c] = m_next
            a_w[c] = alpha
            p_w[c] = jnp.concatenate(ps, axis=1).astype(BF16)
        accumulate(p_r, a_r, vp_ref[...])

    @pl.when(kv % 2 == 0)
    def _():
        step(p0_sc, a0_sc, p1_sc, a1_sc)

    @pl.when(kv % 2 == 1)
    def _():
        step(p1_sc, a1_sc, p0_sc, a0_sc)

    @pl.when(kv == nkv - 1)
    def _():
        if (nkv - 1) % 2 == 0:
            accumulate(p0_sc, a0_sc, vc_ref[...])
        else:
            accumulate(p1_sc, a1_sc, vc_ref[...])
        lam = lam_ref[...]
        lam_full = (jnp.exp(jnp.sum(lam[0:1] * lam[1:2], axis=-1, keepdims=True))
                    - jnp.exp(jnp.sum(lam[2:3] * lam[3:4], axis=-1, keepdims=True)) + lambda_init)
        l0 = jnp.sum(l_sc[0], axis=1, keepdims=True)
        l1 = jnp.sum(l_sc[1], axis=1, keepdims=True)
        o = acc_sc[0] / l0 - lam_full * (acc_sc[1] / l1)
        o_ref[...] = (_norm_rows(o, g_ref[...]) * (1.0 - lambda_init)).astype(o_ref.dtype)


def _diff_flash(q, kt_all, v_all, lam, subln_g, lambda_init, *, tq=1024, tk=768):
    l = q.shape[0]
    t, d = v_all.shape
    hw = 2 * DA_HEAD_DIM
    n_heads = d // hw
    tq = _pick(l, tq)
    tk = _pick(t, tk)
    nkv = t // tk
    return pl.pallas_call(
        functools.partial(_flash_body, nkv=nkv, lambda_init=lambda_init),
        grid=(n_heads, l // tq, nkv),
        in_specs=[pl.BlockSpec(lam.shape, lambda h, i, j: (0, 0)),
                  pl.BlockSpec((1, hw), lambda h, i, j: (0, 0)),
                  pl.BlockSpec((tq, hw), lambda h, i, j: (i, h)),
                  pl.BlockSpec((hw, tk), lambda h, i, j: (h, j)),
                  pl.BlockSpec((tk, hw), lambda h, i, j: (jnp.maximum(j - 1, 0), h)),
                  pl.BlockSpec((tk, hw), lambda h, i, j: (j, h))],
        out_specs=pl.BlockSpec((tq, hw), lambda h, i, j: (i, h)),
        out_shape=jax.ShapeDtypeStruct((l, d), BF16),
        scratch_shapes=[pltpu.VMEM((2, tq, LANES), F32), pltpu.VMEM((2, tq, LANES), F32),
                        pltpu.VMEM((2, tq, hw), F32),
                        pltpu.VMEM((2, tq, tk), BF16), pltpu.VMEM((2, tq, tk), BF16),
                        pltpu.VMEM((2, tq, LANES), F32), pltpu.VMEM((2, tq, LANES), F32)],
        compiler_params=_params("parallel", "parallel", "arbitrary"),
        name="diff_flash",
    )(lam, subln_g, q, kt_all, v_all, v_all)


def _diff_attention(h, hc, w_qkv, w_o, lam, subln_g, lambda_init, x, gate_a):
    l, d = h.shape
    cos2, sin2 = _rope_tables(l)
    qkv = _qkv_rope(h, w_qkv, cos2, sin2)
    c_len = hc.shape[0]
    tmc = _pick(c_len, 256)
    kvc = _matmul(
        "ctx_kv", hc, [w_qkv], m=c_len, n=2 * d, k=d, tm=tmc, tn=512, tk=d,
        out_shape=jax.ShapeDtypeStruct((c_len, 2 * d), BF16),
        out_spec=pl.BlockSpec((tmc, 512), lambda i, j, kk: (i, j)),
        b_specs=[pl.BlockSpec((d, 512), lambda i, j, kk: (0, j + d // 512))])
    kt_all = jnp.concatenate([qkv[:, d:2 * d], kvc[:, :d]], axis=0).T
    v_all = jnp.concatenate([qkv[:, 2 * d:], kvc[:, d:]], axis=0)
    o = _diff_flash(qkv, kt_all, v_all, lam, subln_g, lambda_init)
    return _proj_residual("attn_out", o, w_o, x, gate_a)


def _angles(num, den):
    return (2.0 * math.pi / den) * (num % den).astype(F32)


def _page_matrix(n_ko, n_po, n_s, n_pi, entry):
    r = jnp.arange(n_ko * n_po * SUBLANES, dtype=jnp.int32)[:, None]
    c = jnp.arange(n_s * n_pi * SUBLANES, dtype=jnp.int32)[None, :]
    rr, cc = r // SUBLANES, c // SUBLANES
    val = entry(rr // n_po, rr % n_po, cc // n_pi, cc % n_pi)
    return jnp.where(r % SUBLANES == c % SUBLANES, val, 0.0).astype(BF16)


def _page_mm_body(m_ref, x_ref, o_ref):
    x = x_ref[...]
    x2 = x.reshape(-1, x.shape[-1]).astype(BF16)
    y = jnp.dot(m_ref[...], x2, preferred_element_type=F32)
    o_ref[...] = y.reshape(o_ref.shape).astype(o_ref.dtype)


def _page_mm(mat, x4, ko, po, *, dc=512):
    s, pi, f, d = x4.shape
    dc = _pick(d, dc)
    return pl.pallas_call(
        _page_mm_body,
        grid=(f // SUBLANES, d // dc),
        in_specs=[pl.BlockSpec(mat.shape, lambda a, b: (0, 0)),
                  pl.BlockSpec((s, pi, SUBLANES, dc), lambda a, b: (0, 0, a, b))],
        out_specs=pl.BlockSpec((ko, po, SUBLANES, dc), lambda a, b: (0, 0, a, b)),
        out_shape=jax.ShapeDtypeStruct((ko, po, f, d), F32),
        compiler_params=_params("parallel", "parallel"),
        name="dft_lead",
    )(mat, x4)


def _short_conv_body(u0_ref, u1_ref, u2_ref, w0_ref, w1_ref, w2_ref, b0_ref, b1_ref, b2_ref,
                     x0_ref, p_ref, *, rows):
    n_rows = u0_ref.shape[0]
    n_chunks = n_rows // rows

    def conv(u_ref, w_ref, b_ref, r0, c):
        cur = u_ref[pl.ds(r0, rows), :]
        prev = u_ref[pl.ds(jnp.maximum(r0 - 1, 0), 1), :] * jnp.where(c > 0, 1.0, 0.0)
        nxt = u_ref[pl.ds(jnp.minimum(r0 + rows, n_rows - 1), 1), :] * jnp.where(c < n_chunks - 1, 1.0, 0.0)
        ridx = lax.broadcasted_iota(jnp.int32, cur.shape, 0)
        up = jnp.where(ridx == 0, prev, pltpu.roll(cur, 1, 0))
        down = jnp.where(ridx == rows - 1, nxt, pltpu.roll(cur, rows - 1, 0))
        w = w_ref[...]
        return w[0:1] * up + w[1:2] * cur + w[2:3] * down + b_ref[...]

    def step(c, carry):
        r0 = pl.multiple_of(c * rows, rows)
        x0_ref[pl.ds(r0, rows), :] = conv(u0_ref, w0_ref, b0_ref, r0, c)
        p_ref[pl.ds(r0, rows), :] = conv(u2_ref, w2_ref, b2_ref, r0, c) * conv(u1_ref, w1_ref, b1_ref, r0, c)
        return carry

    lax.fori_loop(0, n_chunks, step, 0)


def _short_conv_gate(u, conv_w, conv_b, *, dc=128, rows=512):
    l, d3 = u.shape
    d = d3 // 3
    nb = d // dc
    rows = _pick(l, rows, SUBLANES)
    us = [pl.BlockSpec((l, dc), lambda j, s=s: (0, j + s * nb)) for s in range(3)]
    ws = [pl.BlockSpec((3, dc), lambda j, s=s: (0, j + s * nb)) for s in range(3)]
    bs = [pl.BlockSpec((1, dc), lambda j, s=s: (0, j + s * nb)) for s in range(3)]
    out = pl.BlockSpec((l, dc), lambda j: (0, j))
    return pl.pallas_call(
        functools.partial(_short_conv_body, rows=rows),
        grid=(nb,),
        in_specs=[*us, *ws, *bs],
        out_specs=[out, out],
        out_shape=[jax.ShapeDtypeStruct((l, d), F32), jax.ShapeDtypeStruct((l, d), F32)],
        compiler_params=_params("parallel"),
        name="hyena_short_conv",
    )(u, u, u, conv_w, conv_w, conv_w, conv_b[None], conv_b[None], conv_b[None])


def _filter_body(z_ref, w_in_ref, b_in_ref, w_mid_ref, b_mid_ref, fq_ref, w_out_ref, dl_ref,
                 k_ref, ss_ref, *, n_inner):
    first = jnp.logical_and(pl.program_id(0) == 0, pl.program_id(1) == 0)

    @pl.when(first)
    def _():
        ss_ref[...] = jnp.zeros_like(ss_ref)

    z = z_ref[...]
    fq = fq_ref[...]
    hdn = jnp.sin(fq * (jnp.dot(z, w_in_ref[...], precision=HIGHEST, preferred_element_type=F32)
                        + b_in_ref[...]))
    for j in range(n_inner):
        hdn = jnp.sin(fq * (jnp.dot(hdn, w_mid_ref[j], precision=HIGHEST, preferred_element_type=F32)
                            + b_mid_ref[j:j + 1]))
    filt = _dot_split(hdn, w_out_ref[...])
    t = z[:, 0:1]
    valid = z[:, LANES - 1:LANES]
    kern = filt * jnp.exp(-t * dl_ref[...]) * valid
    k_ref[...] = kern
    ss_ref[...] += jnp.sum(kern * kern, axis=0, keepdims=True)


def _hyena_kernel(l, d, pe_w_in, pe_b_in, pe_w_mid, pe_b_mid, pe_w_out, sin_freq, *, tl=256):
    bands = (HY_EMB - 1) // 2
    hidden = pe_w_in.shape[1]
    n_inner = pe_w_mid.shape[0]
    pos = jnp.concatenate([jnp.arange(l), l - jnp.arange(l)]).astype(F32)
    valid = jnp.ones((2 * l,), F32).at[l].set(0.0)
    tt = pos / (l - 1)
    fr = jnp.linspace(1e-4, bands - 1, bands, dtype=F32)
    wpos = 2.0 * math.pi * pos[:, None] / l
    z = jnp.concatenate([tt[:, None], jnp.cos(fr * wpos), -jnp.sin(fr * wpos)], axis=-1)
    z = jnp.concatenate([z, jnp.zeros((2 * l, LANES - HY_EMB - 1), F32), valid[:, None]], axis=-1)
    w_in = jnp.zeros((LANES, hidden), F32).at[:HY_EMB].set(pe_w_in)
    max_decay = math.log(HY_DECAY_TARGET) / HY_FAST_DECAY
    min_decay = math.log(HY_DECAY_TARGET) / HY_SLOW_DECAY
    deltas = jnp.abs(jnp.linspace(min_decay, max_decay, d, dtype=F32))[None]
    tl = _pick(l, tl, SUBLANES)
    nt = l // tl
    const = lambda shape: pl.BlockSpec(shape, lambda a, i: tuple(0 for _ in shape))
    return pl.pallas_call(
        functools.partial(_filter_body, n_inner=n_inner),
        grid=(2, nt),
        in_specs=[pl.BlockSpec((tl, LANES), lambda a, i: (a * nt + i, 0)),
                  const((LANES, hidden)), const((1, hidden)),
                  const((n_inner, hidden, hidden)), const((n_inner, hidden)), const((1, hidden)),
                  pl.BlockSpec((hidden, d), lambda a, i: (0, a)),
                  const((1, d))],
        out_specs=[pl.BlockSpec((tl, d), lambda a, i: (a * nt + i, 0)),
                   pl.BlockSpec((1, d), lambda a, i: (0, 0))],
        out_shape=[jax.ShapeDtypeStruct((2 * l, d), F32), jax.ShapeDtypeStruct((1, d), F32)],
        compiler_params=_params("arbitrary", "arbitrary"),
        name="hyena_filter",
    )(z, w_in, pe_b_in[None], pe_w_mid, pe_b_mid, sin_freq[None], pe_w_out, deltas)


def _conv_tables(l):
    n = 2 * l
    f = DFT_F
    s = n // f

    def fwd(k1, po, ss, pi):
        th = _angles(k1 * ss, s)
        return jnp.where(po == 0, jnp.cos(th), -jnp.sin(th))

    def inv(so, po, k1, pi):
        th = _angles(so * k1, s)
        return jnp.where(pi == 0, jnp.cos(th), -jnp.sin(th)) / n

    p1_full = _page_matrix(s, 2, s, 1, fwd)
    p1_half = _page_matrix(s, 2, s // 2, 1, fwd)
    p2 = _page_matrix(s // 2, 1, s, 2, inv)
    k1 = jnp.arange(s, dtype=jnp.int32)[:, None, None]
    row = jnp.arange(2 * f, dtype=jnp.int32)[None, :, None]
    col = jnp.arange(2 * f, dtype=jnp.int32)[None, None, :]
    po, pi = row // f, col // f
    ph1 = _angles((col % f) * (k1 + s * (row % f)), n)
    g1 = jnp.where(po == pi, jnp.cos(ph1), jnp.where(po == 0, jnp.sin(ph1), -jnp.sin(ph1))).astype(BF16)
    ph2 = _angles((row % f) * (k1 + s * (col % f)), n)
    g2 = jnp.where(po == pi, jnp.cos(ph2), jnp.where(po == 0, -jnp.sin(ph2), jnp.sin(ph2))).astype(BF16)
    return p1_half, p1_full, p2, g1, g2


def _spectral_conv_body(g1_ref, g2_ref, a_ref, ka_ref, o_ref):
    g1 = g1_ref[...]
    x = jnp.dot(g1, a_ref[...].astype(BF16), preferred_element_type=F32)
    kf = jnp.dot(g1, ka_ref[...].astype(BF16), preferred_element_type=F32)
    f = x.shape[0] // 2
    xr, xi, kr, ki = x[:f], x[f:], kf[:f], kf[f:]
    y = jnp.concatenate([xr * kr - xi * ki, xr * ki + xi * kr], axis=0).astype(BF16)
    o_ref[...] = jnp.dot(g2_ref[...], y, preferred_element_type=F32)


def _batched_stage(name, body, mats, arrays, *, dc=2048):
    nb, rows, d = arrays[0].shape
    dc = _pick(d, dc)
    mspecs = [pl.BlockSpec((None,) + m.shape[1:], lambda b, j: (b, 0, 0)) for m in mats]
    aspecs = [pl.BlockSpec((None, rows, dc), lambda b, j: (b, 0, j)) for _ in arrays]
    return pl.pallas_call(
        body,
        grid=(nb, d // dc),
        in_specs=[*mspecs, *aspecs],
        out_specs=pl.BlockSpec((None, mats[-1].shape[1], dc), lambda b, j: (b, 0, j)),
        out_shape=jax.ShapeDtypeStruct((nb, mats[-1].shape[1], d), F32),
        compiler_params=_params("parallel", "parallel"),
        name=name,
    )(*mats, *arrays)


def _conv_out_body(m_ref, b_ref, p_ref, x0_ref, ss_ref, bias_ref, o_ref):
    b = b_ref[...]
    b2 = b.reshape(-1, b.shape[-1]).astype(BF16)
    y = jnp.dot(m_ref[...], b2, preferred_element_type=F32).reshape(p_ref.shape)
    z = y * lax.rsqrt(ss_ref[...] + EPS) + p_ref[...] * bias_ref[...]
    o_ref[...] = (x0_ref[...] * z).astype(o_ref.dtype)


def _hyena(h, w_in, conv_w, conv_b, pe_w_in, pe_b_in, pe_w_mid, pe_b_mid, pe_w_out, sin_freq, bias,
           w_out, x, gate_a):
    l, d = h.shape
    f = DFT_F
    s = 2 * l // f
    tm, tn = _pick(l, 1024), _pick(3 * d, 512)
    u = _matmul("hyena_in", h, [w_in], m=l, n=3 * d, k=d, tm=tm, tn=tn, tk=d,
                out_shape=jax.ShapeDtypeStruct((l, 3 * d), F32),
                out_spec=pl.BlockSpec((tm, tn), lambda i, j, kk: (i, j)))
    x0c, p = _short_conv_gate(u, conv_w, conv_b)
    kern, ss = _hyena_kernel(l, d, pe_w_in, pe_b_in, pe_w_mid, pe_b_mid, pe_w_out, sin_freq)
    p1_half, p1_full, p2, g1, g2 = _conv_tables(l)
    ka = _page_mm(p1_full, kern.reshape(s, 1, f, d), s, 2)
    pa = _page_mm(p1_half, p.reshape(s // 2, 1, f, d), s, 2)
    pb = _batched_stage("hyena_spectral_conv", _spectral_conv_body, [g1, g2],
                        [pa.reshape(s, 2 * f, d), ka.reshape(s, 2 * f, d)])
    dc = _pick(d, 512)
    page = pl.BlockSpec((s // 2, SUBLANES, dc), lambda a, b: (0, a, b))
    chan = pl.BlockSpec((1, dc), lambda a, b: (0, b))
    y = pl.pallas_call(
        _conv_out_body,
        grid=(f // SUBLANES, d // dc),
        in_specs=[pl.BlockSpec(p2.shape, lambda a, b: (0, 0)),
                  pl.BlockSpec((s, 2, SUBLANES, dc), lambda a, b: (0, 0, a, b)),
                  page, page, chan, chan],
        out_specs=page,
        out_shape=jax.ShapeDtypeStruct((s // 2, f, d), BF16),
        compiler_params=_params("parallel", "parallel"),
        name="hyena_conv_out",
    )(p2, pb.reshape(s, 2, f, d), p.reshape(s // 2, f, d), x0c.reshape(s // 2, f, d), ss, bias[None])
    return _proj_residual("hyena_out", y.reshape(l, d), w_out, x, gate_a)


def _pool_body(h_ref, o_ref, pad_ref, *, rows, per_group, halo):
    n_rows = h_ref.shape[0]
    dc = h_ref.shape[1]
    pad_ref[pl.ds(0, halo), :] = jnp.zeros((halo, dc), F32)
    pad_ref[pl.ds(halo + n_rows, halo), :] = jnp.zeros((halo, dc), F32)
    pad_ref[pl.ds(halo, n_rows), :] = h_ref[...]
    group = pl.program_id(0) // per_group

    for g, w in enumerate(POOL_SIZES):
        @pl.when(group == g)
        def _(w=w):
            before, after = w // 2, w - w // 2

            def step(c, carry):
                r0 = pl.multiple_of(c * rows, rows)
                ext = pad_ref[pl.ds(r0, rows + 2 * halo), :]
                tot = ext[halo - before:halo - before + rows]
                for o in range(1 - before, after):
                    tot = tot + ext[halo + o:halo + o + rows]
                t = r0 + lax.broadcasted_iota(jnp.int32, (rows, dc), 0)
                cnt = jnp.minimum(t + after, n_rows) - jnp.maximum(t - before, 0)
                cur = ext[halo:halo + rows]
                o_ref[pl.ds(r0, rows), :] = (tot / cnt.astype(F32) - cur).astype(o_ref.dtype)
                return carry

            lax.fori_loop(0, n_rows // rows, step, 0)


def _pool_features(h, *, dc=256, rows=256):
    l, d = h.shape
    group = d // len(POOL_SIZES)
    dc = _pick(group, dc)
    rows = _pick(l, rows, SUBLANES)
    halo = max(POOL_SIZES) // 2
    return pl.pallas_call(
        functools.partial(_pool_body, rows=rows, per_group=group // dc, halo=halo),
        grid=(d // dc,),
        in_specs=[pl.BlockSpec((l, dc), lambda j: (0, j))],
        out_specs=pl.BlockSpec((l, dc), lambda j: (0, j)),
        out_shape=jax.ShapeDtypeStruct((l, d), BF16),
        scratch_shapes=[pltpu.VMEM((l + 2 * halo, dc), F32)],
        compiler_params=_params("parallel"),
        name="pool_features",
    )(h)


def _pool_mixer(h, w_groups, scale, x, gate_a, *, tm=1024, tn=512):
    l, d = h.shape
    n_g, group, _ = w_groups.shape
    pooled = _pool_features(h)
    tm, tn = _pick(l, tm), _pick(group, tn)
    per = group // tn
    return _matmul(
        "pool_proj", pooled, [w_groups], m=l, n=d, k=group, tm=tm, tn=tn, tk=group,
        out_shape=jax.ShapeDtypeStruct((l, d), F32),
        out_spec=pl.BlockSpec((tm, tn), lambda i, j, kk: (i, j)),
        a_spec=pl.BlockSpec((tm, group), lambda i, j, kk: (i, j // per)),
        b_specs=[pl.BlockSpec((None, group, tn), lambda i, j, kk: (j // per, 0, j % per))],
        extras=[x, gate_a, scale],
        extra_specs=[pl.BlockSpec((tm, tn), lambda i, j, kk: (i, j)),
                     pl.BlockSpec((1, tn), lambda i, j, kk: (0, j)),
                     pl.BlockSpec((1, tn), lambda i, j, kk: (0, j))],
        epilogue=lambda accs, ex: ex[0][...] + ex[1][...] * (accs[0] * ex[2][...]))


def _fourier_tables(l, group):
    f = DFT_F
    s = l // f
    ci = jnp.arange(group, dtype=jnp.int32)[:, None]
    co = jnp.arange(2 * group, dtype=jnp.int32)[None, :]
    thc = _angles(ci * (co % group), group)
    wc = jnp.where(co < group, jnp.cos(thc), -jnp.sin(thc)).astype(BF16)

    def stage1(k1, po, ss, pi):
        th = _angles(k1 * ss, s)
        return jnp.where(po == pi, jnp.cos(th), jnp.where(po == 0, jnp.sin(th), -jnp.sin(th)))

    p1 = _page_matrix(s, 2, s, 2, stage1)
    q = jnp.arange(s // SUBLANES, dtype=jnp.int32)[:, None, None]
    row = jnp.arange(f * SUBLANES, dtype=jnp.int32)[None, :, None]
    col = jnp.arange(SUBLANES * 2 * f, dtype=jnp.int32)[None, None, :]
    k = SUBLANES * q + row % SUBLANES + s * (row // SUBLANES)
    ph = _angles((col % f) * k, l)
    val = jnp.where((col // f) % 2 == 0, jnp.cos(ph), jnp.sin(ph))
    scat = jnp.where(row % SUBLANES == col // (2 * f), val, 0.0).astype(BF16)
    return wc, p1, scat


def _fourier_out_body(m_ref, a_ref, o_ref, *, scale):
    y = jnp.dot(m_ref[...], a_ref[...].astype(BF16), preferred_element_type=F32) * scale
    o_ref[...] = y.reshape(o_ref.shape)


def _fourier_mixer(h, w_out, x, gate_a, *, tm=1024, tn=512, dc=1024):
    l, d = h.shape
    group = d // FN_GROUPS
    f = DFT_F
    s = l // f
    wc, p1, scat = _fourier_tables(l, group)
    tm, tn = _pick(l, tm, f), _pick(group, tn)
    per = group // tn
    z = _matmul(
        "fourier_chan", h, [wc], m=l, n=2 * d, k=group, tm=tm, tn=tn, tk=group,
        out_shape=jax.ShapeDtypeStruct((s, 2, f, d), F32),
        out_spec=pl.BlockSpec((tm // f, None, f, tn),
                              lambda i, j, kk: (i, (j // per) % 2, 0, (j // (2 * per)) * per + j % per)),
        a_spec=pl.BlockSpec((tm, group), lambda i, j, kk: (i, j // (2 * per))),
        b_specs=[pl.BlockSpec((group, tn), lambda i, j, kk: (0, j % (2 * per)))])
    a2 = _page_mm(p1, z, s, 2)
    dc = _pick(d, dc)
    q = s // SUBLANES
    mixed = pl.pallas_call(
        functools.partial(_fourier_out_body, scale=1.0 / math.sqrt(l * group)),
        grid=(q, d // dc),
        in_specs=[pl.BlockSpec((None,) + scat.shape[1:], lambda a, b: (a, 0, 0)),
                  pl.BlockSpec((None, SUBLANES * 2 * f, dc), lambda a, b: (a, 0, b))],
        out_specs=pl.BlockSpec((f, None, SUBLANES, dc), lambda a, b: (0, a, 0, b)),
        out_shape=jax.ShapeDtypeStruct((f, q, SUBLANES, d), F32),
        compiler_params=_params("parallel", "parallel"),
        name="fourier_seq_out",
    )(scat, a2.reshape(q, SUBLANES * 2 * f, d))
    return _proj_residual("fourier_out", mixed.reshape(l, d), w_out, x, gate_a, tm=512)


def kernel(x, c, ctx, c_ctx, ada_down, ada_up, ada_b, norm_g, final_g, da_w_qkv, da_w_o, da_lambda, da_subln_g, hy_w_in, hy_conv_w, hy_conv_b, hy_pe_w_in, hy_pe_b_in, hy_pe_w_mid, hy_pe_b_mid, hy_pe_w_out, hy_sin_freq, hy_bias, hy_w_out, pool_w, pool_scale, fn_w_out, ffn_w_gate, ffn_w_up, ffn_w_down, moe_router, moe_router_b, moe_w_gate, moe_w_up, moe_w_down):
    batch, l, d = x.shape
    depth = ada_down.shape[0]
    assert batch == 1
    xs = x[0]
    mod = _adaln_all(c, c_ctx, ada_down, ada_up, ada_b)
    ffn_w_down_bf = ffn_w_down.astype(BF16)
    for i in range(depth):
        shift_a, scale_a, gate_a, shift_f, scale_f, gate_f = [
            mod[i, 0:1, n * d:(n + 1) * d] for n in range(N_MOD)]
        g_a, g_f = norm_g[i, 0][None], norm_g[i, 1][None]
        mixer = i % 4
        if mixer == 0:
            h = _norm_mod(xs, g_a, shift_a, scale_a, BF16)
            hc = _norm_mod(ctx[0], g_a, mod[i, 1:2, 0:d], mod[i, 1:2, d:2 * d], BF16)
            xs = _diff_attention(h, hc, da_w_qkv, da_w_o, da_lambda, da_subln_g[None],
                                 0.8 - 0.6 * math.exp(-0.3 * i), xs, gate_a)
        elif mixer == 1:
            h = _norm_mod(xs, g_a, shift_a, scale_a, BF16)
            xs = _hyena(h, hy_w_in, hy_conv_w, hy_conv_b, hy_pe_w_in, hy_pe_b_in, hy_pe_w_mid,
                        hy_pe_b_mid, hy_pe_w_out, hy_sin_freq, hy_bias, hy_w_out, xs, gate_a)
        elif mixer == 2:
            h = _norm_mod(xs, g_a, shift_a, scale_a, F32)
            xs = _pool_mixer(h, pool_w, pool_scale[None], xs, gate_a)
        else:
            h = _norm_mod(xs, g_a, shift_a, scale_a, BF16)
            xs = _fourier_mixer(h, fn_w_out, xs, gate_a)
        if i % 2 == 0:
            h = _norm_mod(xs, g_f, shift_f, scale_f, BF16)
            xs = _swiglu(h, ffn_w_gate, ffn_w_up, ffn_w_down_bf, i // 2, xs, gate_f)
        else:
            j = i // 2
            h, sel = _norm_router(xs, g_f, shift_f, scale_f, moe_router[j], moe_router_b[j])
            xs = _moe(h, sel, moe_w_gate, moe_w_up, moe_w_down, j, xs, gate_f)
    return _rmsnorm(xs, final_g[None])[None]
```

```python
import functools
import math

import jax
import jax.numpy as jnp
from jax import lax
from jax.experimental import pallas as pl
from jax.experimental.pallas import tpu as pltpu

F32 = jnp.float32
BF16 = jnp.bfloat16
HIGHEST = lax.Precision.HIGHEST

EPS = 1e-6
GRID_W = 64
N_MOD = 6
DA_HEADS = 16
DA_HEAD_DIM = 128
ROPE_THETA = 10000.0
HY_EMB = 33
HY_DECAY_TARGET = 1e-2
HY_FAST_DECAY = 0.3
HY_SLOW_DECAY = 1.5
POOL_SIZES = (2, 4, 8, 16)
FN_GROUPS = 4
TOP_K = 2

LANES = 128
SUBLANES = 8
DFT_F = 128
VMEM_LIMIT = 56 * 1024 * 1024


def _params(*sem):
    return pltpu.CompilerParams(dimension_semantics=sem, vmem_limit_bytes=VMEM_LIMIT)


def _pick(dim, pref, align=LANES):
    if dim <= pref:
        return dim
    t = (pref // align) * align
    while t >= align:
        if dim % t == 0:
            return t
        t -= align
    raise ValueError(f"no {align}-aligned tile of {dim} below {pref}")


def _mm_body(*refs, n_b, n_extra, nk, epilogue):
    a_ref = refs[0]
    b_refs = refs[1:1 + n_b]
    extra_refs = refs[1 + n_b:1 + n_b + n_extra]
    o_ref = refs[1 + n_b + n_extra]
    acc_refs = refs[2 + n_b + n_extra:]
    a = a_ref[...].astype(BF16)
    if nk == 1:
        accs = [jnp.dot(a, b[...].astype(BF16), preferred_element_type=F32) for b in b_refs]
        o_ref[...] = epilogue(accs, extra_refs).reshape(o_ref.shape).astype(o_ref.dtype)
        return
    k = pl.program_id(2)

    @pl.when(k == 0)
    def _():
        for acc in acc_refs:
            acc[...] = jnp.zeros_like(acc)

    for acc, b in zip(acc_refs, b_refs):
        acc[...] += jnp.dot(a, b[...].astype(BF16), preferred_element_type=F32)

    @pl.when(k == nk - 1)
    def _():
        accs = [acc[...] for acc in acc_refs]
        o_ref[...] = epilogue(accs, extra_refs).reshape(o_ref.shape).astype(o_ref.dtype)


def _matmul(name, a, bs, *, m, n, k, tm, tn, tk, out_shape, out_spec, a_spec=None, b_specs=None,
            extras=(), extra_specs=(), epilogue=None):
    nk = k // tk
    if epilogue is None:
        epilogue = lambda accs, ex: accs[0]
    if a_spec is None:
        a_spec = pl.BlockSpec((tm, tk), lambda i, j, kk: (i, kk))
    if b_specs is None:
        b_specs = [pl.BlockSpec((tk, tn), lambda i, j, kk: (kk, j)) for _ in bs]
    scratch = [pltpu.VMEM((tm, tn), F32) for _ in bs] if nk > 1 else []
    body = functools.partial(_mm_body, n_b=len(bs), n_extra=len(extras), nk=nk, epilogue=epilogue)
    return pl.pallas_call(
        body,
        grid=(m // tm, n // tn, nk),
        in_specs=[a_spec, *b_specs, *extra_specs],
        out_specs=out_spec,
        out_shape=out_shape,
        scratch_shapes=scratch,
        compiler_params=_params("parallel", "parallel", "arbitrary"),
        name=name,
    )(a, *bs, *extras)


def _residual_epilogue(accs, ex):
    return ex[0][...] + ex[1][...] * accs[0]


def _proj_residual(name, a, w, x, gate, *, tm=1024, tn=512):
    m, k = a.shape
    n = w.shape[1]
    tm, tn = _pick(m, tm), _pick(n, tn)
    return _matmul(
        name, a, [w], m=m, n=n, k=k, tm=tm, tn=tn, tk=k,
        out_shape=jax.ShapeDtypeStruct((m, n), F32),
        out_spec=pl.BlockSpec((tm, tn), lambda i, j, kk: (i, j)),
        extras=[x, gate],
        extra_specs=[pl.BlockSpec((tm, tn), lambda i, j, kk: (i, j)),
                     pl.BlockSpec((1, tn), lambda i, j, kk: (0, j))],
        epilogue=_residual_epilogue)


def _dot_split(a, b):
    a_hi, b_hi = a.astype(BF16), b.astype(BF16)
    a_lo = (a - a_hi.astype(F32)).astype(BF16)
    b_lo = (b - b_hi.astype(F32)).astype(BF16)
    dot = functools.partial(jnp.dot, preferred_element_type=F32)
    return dot(a_hi, b_hi) + (dot(a_hi, b_lo) + dot(a_lo, b_hi))


def _ada_body(a_ref, w_ref, b_ref, o_ref, *, silu):
    a = a_ref[...]
    if silu:
        a = a * jax.nn.sigmoid(a)
    o_ref[...] = _dot_split(a, w_ref[...]) + b_ref[...]


def _adaln_all(c, c_ctx, ada_down, ada_up, ada_b):
    depth, d, r = ada_down.shape
    width = ada_up.shape[2]
    rows = jnp.zeros((SUBLANES, d), F32).at[0].set(c[0]).at[1].set(c_ctx)
    tn1 = _pick(r, 256)
    t = pl.pallas_call(
        functools.partial(_ada_body, silu=True),
        grid=(depth, r // tn1),
        in_specs=[pl.BlockSpec((SUBLANES, d), lambda l, j: (0, 0)),
                  pl.BlockSpec((None, d, tn1), lambda l, j: (l, 0, j)),
                  pl.BlockSpec((1, tn1), lambda l, j: (0, j))],
        out_specs=pl.BlockSpec((None, SUBLANES, tn1), lambda l, j: (l, 0, j)),
        out_shape=jax.ShapeDtypeStruct((depth, SUBLANES, r), F32),
        compiler_params=_params("parallel", "parallel"),
        name="adaln_down",
    )(rows, ada_down, jnp.zeros((1, r), F32))
    tn2 = _pick(width, 2048)
    return pl.pallas_call(
        functools.partial(_ada_body, silu=False),
        grid=(depth, width // tn2),
        in_specs=[pl.BlockSpec((None, SUBLANES, r), lambda l, j: (l, 0, 0)),
                  pl.BlockSpec((None, r, tn2), lambda l, j: (l, 0, j)),
                  pl.BlockSpec((None, 1, tn2), lambda l, j: (l, 0, j))],
        out_specs=pl.BlockSpec((None, SUBLANES, tn2), lambda l, j: (l, 0, j)),
        out_shape=jax.ShapeDtypeStruct((depth, SUBLANES, width), F32),
        compiler_params=_params("parallel", "parallel"),
        name="adaln_up",
    )(t, ada_up, ada_b[:, None, :])


def _norm_rows(x, g):
    return x * lax.rsqrt(jnp.mean(x * x, axis=-1, keepdims=True) + EPS) * g


def _norm_mod_body(x_ref, g_ref, shift_ref, scale_ref, o_ref):
    y = _norm_rows(x_ref[...], g_ref[...])
    o_ref[...] = (y * (1.0 + scale_ref[...]) + shift_ref[...]).astype(o_ref.dtype)


def _norm_mod(x, g, shift, scale, out_dtype, *, tm=256):
    m, d = x.shape
    tm = _pick(m, tm, SUBLANES)
    row = pl.BlockSpec((1, d), lambda i: (0, 0))
    return pl.pallas_call(
        _norm_mod_body,
        grid=(m // tm,),
        in_specs=[pl.BlockSpec((tm, d), lambda i: (i, 0)), row, row, row],
        out_specs=pl.BlockSpec((tm, d), lambda i: (i, 0)),
        out_shape=jax.ShapeDtypeStruct((m, d), out_dtype),
        compiler_params=_params("parallel"),
        name="norm_mod",
    )(x, g, shift, scale)


def _rmsnorm_body(x_ref, g_ref, o_ref):
    o_ref[...] = _norm_rows(x_ref[...], g_ref[...])


def _rmsnorm(x, g, *, tm=256):
    m, d = x.shape
    tm = _pick(m, tm, SUBLANES)
    return pl.pallas_call(
        _rmsnorm_body,
        grid=(m // tm,),
        in_specs=[pl.BlockSpec((tm, d), lambda i: (i, 0)), pl.BlockSpec((1, d), lambda i: (0, 0))],
        out_specs=pl.BlockSpec((tm, d), lambda i: (i, 0)),
        out_shape=jax.ShapeDtypeStruct((m, d), F32),
        compiler_params=_params("parallel"),
        name="final_norm",
    )(x, g)


def _norm_router_body(x_ref, g_ref, shift_ref, scale_ref, rw_ref, rb_ref, h_ref, sel_ref):
    h = _norm_rows(x_ref[...], g_ref[...]) * (1.0 + scale_ref[...]) + shift_ref[...]
    h_ref[...] = h
    logits = jnp.dot(h, rw_ref[...], precision=HIGHEST, preferred_element_type=F32) + rb_ref[...]
    lane = lax.broadcasted_iota(jnp.int32, logits.shape, 1)
    m1 = jnp.max(logits, axis=-1, keepdims=True)
    i1 = jnp.min(jnp.where(logits == m1, lane, LANES), axis=-1, keepdims=True)
    rest = jnp.where(lane == i1, -jnp.inf, logits)
    m2 = jnp.max(rest, axis=-1, keepdims=True)
    i2 = jnp.min(jnp.where(rest == m2, lane, LANES), axis=-1, keepdims=True)
    e = jnp.exp(m2 - m1)
    w1 = 1.0 / (1.0 + e)
    w2 = e * w1
    sel_ref[...] = jnp.where(lane == 0, i1.astype(F32),
                             jnp.where(lane == 1, i2.astype(F32),
                                       jnp.where(lane == 2, w1, jnp.where(lane == 3, w2, 0.0))))


def _norm_router(x, g, shift, scale, router, router_b, *, tm=256):
    m, d = x.shape
    n_e = router.shape[1]
    tm = _pick(m, tm, SUBLANES)
    rw = jnp.zeros((d, LANES), F32).at[:, :n_e].set(router)
    rb = jnp.full((1, LANES), -1e30, F32).at[0, :n_e].set(router_b)
    row = pl.BlockSpec((1, d), lambda i: (0, 0))
    return pl.pallas_call(
        _norm_router_body,
        grid=(m // tm,),
        in_specs=[pl.BlockSpec((tm, d), lambda i: (i, 0)), row, row, row,
                  pl.BlockSpec((d, LANES), lambda i: (0, 0)),
                  pl.BlockSpec((1, LANES), lambda i: (0, 0))],
        out_specs=[pl.BlockSpec((tm, d), lambda i: (i, 0)),
                   pl.BlockSpec((tm, LANES), lambda i: (i, 0))],
        out_shape=[jax.ShapeDtypeStruct((m, d), F32),
                   jax.ShapeDtypeStruct((m, LANES), F32)],
        compiler_params=_params("parallel"),
        name="norm_router",
    )(x, g, shift, scale, rw, rb)


def _silu(a):
    return a * jax.nn.sigmoid(a)


def _swiglu(h, w_gate, w_up, w_down, layer, x, gate_f):
    m, d = h.shape
    f = w_gate.shape[2]
    tm = _pick(m, 2048)
    tn = _pick(f, 256)
    wspec = pl.BlockSpec((None, d, tn), lambda i, j, kk: (layer, 0, j))
    hid = _matmul(
        "swiglu_up", h, [w_gate, w_up], m=m, n=f, k=d, tm=tm, tn=tn, tk=d,
        out_shape=jax.ShapeDtypeStruct((m, f), BF16),
        out_spec=pl.BlockSpec((tm, tn), lambda i, j, kk: (i, j)),
        a_spec=pl.BlockSpec((tm, d), lambda i, j, kk: (i, 0), pipeline_mode=pl.Buffered(1)),
        b_specs=[wspec, wspec],
        epilogue=lambda accs, ex: _silu(accs[0]) * accs[1])
    tm2 = _pick(m, 1024)
    tn2 = _pick(d, 256)
    return _matmul(
        "swiglu_down", hid, [w_down], m=m, n=d, k=f, tm=tm2, tn=tn2, tk=f,
        out_shape=jax.ShapeDtypeStruct((m, d), F32),
        out_spec=pl.BlockSpec((tm2, tn2), lambda i, j, kk: (i, j)),
        a_spec=pl.BlockSpec((tm2, f), lambda i, j, kk: (i, 0), pipeline_mode=pl.Buffered(1)),
        b_specs=[pl.BlockSpec((None, f, tn2), lambda i, j, kk: (layer, 0, j))],
        extras=[x, gate_f],
        extra_specs=[pl.BlockSpec((tm2, tn2), lambda i, j, kk: (i, j)),
                     pl.BlockSpec((1, tn2), lambda i, j, kk: (0, j))],
        epilogue=_residual_epilogue)


MOE_TILE = 512


def _moe_plan(sel, n_e, tile):
    l = sel.shape[0]
    e = jnp.concatenate([sel[:, 0], sel[:, 1]]).astype(jnp.int32)
    onehot = (e[:, None] == jnp.arange(n_e, dtype=jnp.int32)[None, :]).astype(jnp.int32)
    incl = jnp.cumsum(onehot, axis=0)
    counts = incl[-1]
    padded = ((counts + tile - 1) // tile) * tile
    ends = jnp.cumsum(padded)
    starts = ends - padded
    dest = jnp.sum(onehot * (starts[None, :] + incl - 1), axis=1)
    n_rows = 2 * l + n_e * tile
    n_tiles = n_rows // tile
    token = jnp.tile(jnp.arange(l, dtype=jnp.int32), 2)
    tok = jnp.zeros((n_rows,), jnp.int32).at[dest].set(token)
    n_active = ends[-1] // tile
    t_clamped = jnp.minimum(jnp.arange(n_tiles, dtype=jnp.int32), n_active - 1)
    tile_expert = jnp.sum((t_clamped[:, None] * tile >= ends[None, :]).astype(jnp.int32), axis=1)
    return dest, tok, tile_expert, n_active.reshape(1).astype(jnp.int32)


def _prefetched_rows(copies, n_rows, n_steps):
    i = pl.program_id(0)
    slot = i % 2

    def issue(step, into):
        def body(r, carry):
            for cp in copies(step, r, into):
                cp.start()
            return carry
        lax.fori_loop(0, n_rows, body, 0, unroll=8)

    def drain(r, carry):
        for cp in copies(i, r, slot):
            cp.wait()
        return carry

    @pl.when(i == 0)
    def _():
        issue(0, 0)

    @pl.when(i + 1 < n_steps)
    def _():
        issue(i + 1, 1 - slot)

    lax.fori_loop(0, n_rows, drain, 0, unroll=8)
    return slot


def _row_gather_body(idx_ref, src_hbm, o_ref, buf, sem, *, rows, n_steps):
    def copies(step, r, slot):
        return [pltpu.make_async_copy(src_hbm.at[pl.ds(idx_ref[step * rows + r], 1), :],
                                      buf.at[slot, pl.ds(r, 1), :], sem.at[slot])]

    slot = _prefetched_rows(copies, rows, n_steps)
    o_ref[...] = buf[slot].astype(o_ref.dtype)


def _row_gather(src, idx, out_dtype, *, rows=256):
    n, d = idx.shape[0], src.shape[1]
    rows = _pick(n, rows, SUBLANES)
    n_steps = n // rows
    return pl.pallas_call(
        functools.partial(_row_gather_body, rows=rows, n_steps=n_steps),
        grid_spec=pltpu.PrefetchScalarGridSpec(
            num_scalar_prefetch=1,
            grid=(n_steps,),
            in_specs=[pl.BlockSpec(memory_space=pl.ANY)],
            out_specs=pl.BlockSpec((rows, d), lambda i, idx_ref: (i, 0)),
            scratch_shapes=[pltpu.VMEM((2, rows, d), src.dtype), pltpu.SemaphoreType.DMA((2,))]),
        out_shape=jax.ShapeDtypeStruct((n, d), out_dtype),
        compiler_params=_params("arbitrary"),
        name="moe_gather",
    )(idx, src)


def _grouped_body(te_ref, na_ref, a_ref, *refs, n_b, n_extra, epilogue):
    b_refs = refs[:n_b]
    extra_refs = refs[n_b:n_b + n_extra]
    o_ref = refs[n_b + n_extra]
    bf_refs = refs[n_b + n_extra + 1:]
    t = pl.program_id(1)
    active = t < na_ref[0]
    fresh = jnp.logical_or(t == 0, te_ref[t] != te_ref[jnp.maximum(t - 1, 0)])

    @pl.when(jnp.logical_and(active, fresh))
    def _():
        for b, bf in zip(b_refs, bf_refs):
            bf[...] = b[...].astype(BF16)

    @pl.when(active)
    def _():
        a = a_ref[...]
        accs = [jnp.dot(a, bf[...], preferred_element_type=F32) for bf in bf_refs]
        o_ref[...] = epilogue(accs, extra_refs).astype(o_ref.dtype)

    @pl.when(jnp.logical_not(active))
    def _():
        o_ref[...] = jnp.zeros_like(o_ref)


def _grouped_matmul(name, a, ws, layer, tile_expert, n_active, *, tile, tn, out_dtype, extras=(), epilogue=None):
    p, k = a.shape
    n = ws[0].shape[3]
    if epilogue is None:
        epilogue = lambda accs, ex: accs[0]
    row = lambda j, t, te, na: jnp.minimum(t, na[0] - 1)
    return pl.pallas_call(
        functools.partial(_grouped_body, n_b=len(ws), n_extra=len(extras), epilogue=epilogue),
        grid_spec=pltpu.PrefetchScalarGridSpec(
            num_scalar_prefetch=2,
            grid=(n // tn, p // tile),
            in_specs=[pl.BlockSpec((tile, k), lambda j, t, te, na: (row(j, t, te, na), 0)),
                      *[pl.BlockSpec((None, None, k, tn), lambda j, t, te, na: (layer, te[t], 0, j)) for _ in ws],
                      *[pl.BlockSpec((tile, e.shape[1]), lambda j, t, te, na: (row(j, t, te, na), 0))
                        for e in extras]],
            out_specs=pl.BlockSpec((tile, tn), lambda j, t, te, na: (t, j)),
            scratch_shapes=[pltpu.VMEM((k, tn), BF16) for _ in ws]),
        out_shape=jax.ShapeDtypeStruct((p, n), out_dtype),
        compiler_params=_params("arbitrary", "arbitrary"),
        name=name,
    )(tile_expert, n_active, a, *ws, *extras)


def _combine_body(pos_ref, y_hbm, x_ref, g_ref, sel_ref, o_ref, buf, sem, *, rows, n_tok, n_steps):
    def copies(step, r, slot):
        return [pltpu.make_async_copy(y_hbm.at[pl.ds(pos_ref[k * n_tok + step * rows + r], 1), :],
                                      buf.at[slot, k, pl.ds(r, 1), :], sem.at[slot]) for k in range(TOP_K)]

    slot = _prefetched_rows(copies, rows, n_steps)
    sel = sel_ref[...]
    y = sel[:, 2:3] * buf[slot, 0] + sel[:, 3:4] * buf[slot, 1]
    o_ref[...] = x_ref[...] + g_ref[...] * y


def _moe_combine(y, dest, sel, x, gate_f, *, rows=256):
    l, d = x.shape
    rows = _pick(l, rows, SUBLANES)
    n_steps = l // rows
    return pl.pallas_call(
        functools.partial(_combine_body, rows=rows, n_tok=l, n_steps=n_steps),
        grid_spec=pltpu.PrefetchScalarGridSpec(
            num_scalar_prefetch=1,
            grid=(n_steps,),
            in_specs=[pl.BlockSpec(memory_space=pl.ANY),
                      pl.BlockSpec((rows, d), lambda i, pos: (i, 0)),
                      pl.BlockSpec((1, d), lambda i, pos: (0, 0)),
                      pl.BlockSpec((rows, LANES), lambda i, pos: (i, 0))],
            out_specs=pl.BlockSpec((rows, d), lambda i, pos: (i, 0)),
            scratch_shapes=[pltpu.VMEM((2, TOP_K, rows, d), F32), pltpu.SemaphoreType.DMA((2,))]),
        out_shape=jax.ShapeDtypeStruct((l, d), F32),
        compiler_params=_params("arbitrary"),
        name="moe_combine",
    )(dest, y, x, gate_f, sel)


def _moe(h, sel, w_gate, w_up, w_down, layer, x, gate_f):
    _, n_e, d, fe = w_gate.shape
    tile = MOE_TILE
    dest, tok, tile_expert, n_active = _moe_plan(sel, n_e, tile)
    hs = _row_gather(h, tok, BF16)
    hid = _grouped_matmul(
        "moe_up", hs, [w_gate, w_up], layer, tile_expert, n_active, tile=tile, tn=_pick(fe, 256),
        out_dtype=BF16, epilogue=lambda accs, ex: _silu(accs[0]) * accs[1])
    y = _grouped_matmul("moe_down", hid, [w_down], layer, tile_expert, n_active, tile=tile, tn=_pick(d, 1024),
                        out_dtype=F32)
    return _moe_combine(y, dest, sel, x, gate_f)


def _rope_tables(n_tokens):
    pairs = DA_HEAD_DIM // 4
    rows = n_tokens // GRID_W
    row = jnp.repeat(jnp.arange(rows, dtype=F32), GRID_W)
    col = jnp.tile(jnp.arange(GRID_W, dtype=F32), rows)
    inv = ROPE_THETA ** (-jnp.arange(pairs, dtype=F32) / pairs)
    ang = jnp.concatenate([row[:, None] * inv, col[:, None] * inv], axis=-1)
    cos, sin = jnp.cos(ang), jnp.sin(ang)
    return jnp.concatenate([cos, cos], axis=-1), jnp.concatenate([-sin, sin], axis=-1)


def _qkv_rope(h, w_qkv, cos2, sin2, *, tm=1024, tn=512):
    m, d = h.shape
    n = w_qkv.shape[1]
    tm, tn = _pick(m, tm), _pick(d, tn)
    n_q = d // tn
    half = DA_HEAD_DIM // 2

    def epi(accs, ex):
        acc = accs[0]
        j = pl.program_id(1)
        cos, sin = ex[0][...], ex[1][...]
        heads = []
        for t in range(tn // DA_HEAD_DIM):
            xh = acc[:, t * DA_HEAD_DIM:(t + 1) * DA_HEAD_DIM]
            heads.append(xh * cos + pltpu.roll(xh, half, 1) * sin)
        roped = jnp.concatenate(heads, axis=1)
        qscale = jnp.where(j < n_q, DA_HEAD_DIM ** -0.5 * math.log2(math.e), 1.0).astype(F32)
        return jnp.where(j < 2 * n_q, roped * qscale, acc)

    tab = pl.BlockSpec((tm, DA_HEAD_DIM), lambda i, j, kk: (i, 0))
    return _matmul(
        "qkv_rope", h, [w_qkv], m=m, n=n, k=d, tm=tm, tn=tn, tk=d,
        out_shape=jax.ShapeDtypeStruct((m, n), BF16),
        out_spec=pl.BlockSpec((tm, tn), lambda i, j, kk: (i, j)),
        extras=[cos2, sin2], extra_specs=[tab, tab], epilogue=epi)


def _flash_body(lam_ref, g_ref, q_ref, kt_ref, v_ref, o_ref,
                m_sc, l_sc, acc_sc, p0_sc, p1_sc, a0_sc, a1_sc, *, tk, nkv, lambda_init):
    n_val = v_ref.shape[1] // LANES
    n_lane_tiles = tk // LANES
    m_sc[...] = jnp.full_like(m_sc, -jnp.inf)
    l_sc[...] = jnp.zeros_like(l_sc)
    acc_sc[...] = jnp.zeros_like(acc_sc)

    def accumulate(j, p_r, a_r):
        v = v_ref[pl.ds(pl.multiple_of(j * tk, tk), tk), :]
        for c in range(2):
            alpha = jnp.concatenate([a_r[c]] * n_val, axis=1)
            acc_sc[c] = acc_sc[c] * alpha + jnp.dot(p_r[c], v, preferred_element_type=F32)

    def score(j, p_w, a_w):
        q = q_ref[...]
        kt = kt_ref[j]
        for c in range(2):
            qc = q[:, c * DA_HEAD_DIM:(c + 1) * DA_HEAD_DIM]
            kc = kt[c * DA_HEAD_DIM:(c + 1) * DA_HEAD_DIM, :]
            s = jnp.dot(qc, kc, preferred_element_type=F32)
            tiles = [s[:, t * LANES:(t + 1) * LANES] for t in range(n_lane_tiles)]
            m_prev = m_sc[c]
            m_next = jnp.maximum(m_prev, jnp.max(functools.reduce(jnp.maximum, tiles), axis=1, keepdims=True))
            alpha = jnp.exp2(m_prev - m_next)
            ps = [jnp.exp2(t - m_next) for t in tiles]
            l_sc[c] = alpha * l_sc[c] + functools.reduce(jnp.add, ps)
            m_sc[c] = m_next
            a_w[c] = alpha
            p_w[c] = jnp.concatenate(ps, axis=1).astype(BF16)

    even, odd = (p0_sc, a0_sc), (p1_sc, a1_sc)
    score(0, *even)

    def pair(i, carry):
        j = 2 * i + 1
        score(j, *odd)
        accumulate(j - 1, *even)
        score(j + 1, *even)
        accumulate(j, *odd)
        return carry

    lax.fori_loop(0, (nkv - 1) // 2, pair, 0)
    if nkv % 2 == 0:
        score(nkv - 1, *odd)
        accumulate(nkv - 2, *even)
        accumulate(nkv - 1, *odd)
    else:
        accumulate(nkv - 1, *even)

    lam = lam_ref[...]
    lam_full = (jnp.exp(jnp.sum(lam[0:1] * lam[1:2], axis=-1, keepdims=True))
                - jnp.exp(jnp.sum(lam[2:3] * lam[3:4], axis=-1, keepdims=True)) + lambda_init)
    l0 = jnp.sum(l_sc[0], axis=1, keepdims=True)
    l1 = jnp.sum(l_sc[1], axis=1, keepdims=True)
    o = acc_sc[0] / l0 - lam_full * (acc_sc[1] / l1)
    o_ref[...] = (_norm_rows(o, g_ref[...]) * (1.0 - lambda_init)).astype(o_ref.dtype)


def _diff_flash(q, k_all, v_all, lam, subln_g, lambda_init, *, tq=1024, tk=768):
    l = q.shape[0]
    t, d = v_all.shape
    hw = 2 * DA_HEAD_DIM
    n_heads = d // hw
    tq = _pick(l, tq)
    tk = _pick(t, tk)
    nkv = t // tk
    kt = k_all.reshape(nkv, tk, n_heads, hw).transpose(2, 0, 3, 1)
    return pl.pallas_call(
        functools.partial(_flash_body, tk=tk, nkv=nkv, lambda_init=lambda_init),
        grid=(n_heads, l // tq),
        in_specs=[pl.BlockSpec(lam.shape, lambda h, i: (0, 0)),
                  pl.BlockSpec((1, hw), lambda h, i: (0, 0)),
                  pl.BlockSpec((tq, hw), lambda h, i: (i, h)),
                  pl.BlockSpec((None, nkv, hw, tk), lambda h, i: (h, 0, 0, 0)),
                  pl.BlockSpec((t, hw), lambda h, i: (0, h))],
        out_specs=pl.BlockSpec((tq, hw), lambda h, i: (i, h)),
        out_shape=jax.ShapeDtypeStruct((l, d), BF16),
        scratch_shapes=[pltpu.VMEM((2, tq, LANES), F32), pltpu.VMEM((2, tq, LANES), F32),
                        pltpu.VMEM((2, tq, hw), F32),
                        pltpu.VMEM((2, tq, tk), BF16), pltpu.VMEM((2, tq, tk), BF16),
                        pltpu.VMEM((2, tq, LANES), F32), pltpu.VMEM((2, tq, LANES), F32)],
        compiler_params=_params("parallel", "parallel"),
        name="diff_flash",
    )(lam, subln_g, q, kt, v_all)


def _diff_attention(h, hc, w_qkv, w_o, lam, subln_g, lambda_init, x, gate_a):
    l, d = h.shape
    cos2, sin2 = _rope_tables(l)
    qkv = _qkv_rope(h, w_qkv, cos2, sin2)
    c_len = hc.shape[0]
    tmc = _pick(c_len, 256)
    kvc = _matmul(
        "ctx_kv", hc, [w_qkv], m=c_len, n=2 * d, k=d, tm=tmc, tn=512, tk=d,
        out_shape=jax.ShapeDtypeStruct((c_len, 2 * d), BF16),
        out_spec=pl.BlockSpec((tmc, 512), lambda i, j, kk: (i, j)),
        b_specs=[pl.BlockSpec((d, 512), lambda i, j, kk: (0, j + d // 512))])
    k_all = jnp.concatenate([qkv[:, d:2 * d], kvc[:, :d]], axis=0)
    v_all = jnp.concatenate([qkv[:, 2 * d:], kvc[:, d:]], axis=0)
    o = _diff_flash(qkv, k_all, v_all, lam, subln_g, lambda_init)
    return _proj_residual("attn_out", o, w_o, x, gate_a)


def _angles(num, den):
    return (2.0 * math.pi / den) * (num % den).astype(F32)


def _cis(num, den):
    th = _angles(num, den)
    return jnp.cos(th), jnp.sin(th)


def _twiddle_grid(s, f, n):
    ar = lambda m: jnp.arange(m, dtype=jnp.int32)
    ac, asn = _cis(ar(s)[:, None] * ar(f)[None, :], n)
    bc, bsn = _cis(ar(f)[:, None] * ar(f)[None, :], f)
    cos = ac[:, None, :] * bc[None, :, :] - asn[:, None, :] * bsn[None, :, :]
    sin = asn[:, None, :] * bc[None, :, :] + ac[:, None, :] * bsn[None, :, :]
    return cos, sin


def _page_matrix(base):
    nr, nc = base.shape
    r = jnp.arange(nr * SUBLANES, dtype=jnp.int32)[:, None]
    c = jnp.arange(nc * SUBLANES, dtype=jnp.int32)[None, :]
    expand_r = (r // SUBLANES == jnp.arange(nr, dtype=jnp.int32)[None, :]).astype(F32)
    expand_c = (jnp.arange(nc, dtype=jnp.int32)[:, None] == c // SUBLANES).astype(F32)
    full = jnp.dot(jnp.dot(expand_r, base), expand_c)
    return jnp.where(r % SUBLANES == c % SUBLANES, full, 0.0).astype(BF16)


def _page_mm_body(m_ref, x_ref, o_ref):
    x = x_ref[...]
    x2 = x.reshape(-1, x.shape[-1]).astype(BF16)
    y = jnp.dot(m_ref[...], x2, preferred_element_type=F32)
    o_ref[...] = y.reshape(o_ref.shape).astype(o_ref.dtype)


def _page_mm(mat, x4, ko, po, *, dc=512):
    s, pi, f, d = x4.shape
    dc = _pick(d, dc)
    return pl.pallas_call(
        _page_mm_body,
        grid=(f // SUBLANES, d // dc),
        in_specs=[pl.BlockSpec(mat.shape, lambda a, b: (0, 0)),
                  pl.BlockSpec((s, pi, SUBLANES, dc), lambda a, b: (0, 0, a, b))],
        out_specs=pl.BlockSpec((ko, po, SUBLANES, dc), lambda a, b: (0, 0, a, b)),
        out_shape=jax.ShapeDtypeStruct((ko, po, f, d), F32),
        compiler_params=_params("parallel", "parallel"),
        name="dft_lead",
    )(mat, x4)


def _short_conv_body(u0_ref, u1_ref, u2_ref, w0_ref, w1_ref, w2_ref, b0_ref, b1_ref, b2_ref,
                     x0_ref, p_ref, *, rows):
    n_rows = u0_ref.shape[0]
    n_chunks = n_rows // rows

    def conv(u_ref, w_ref, b_ref, r0, c):
        cur = u_ref[pl.ds(r0, rows), :]
        prev = u_ref[pl.ds(jnp.maximum(r0 - 1, 0), 1), :] * jnp.where(c > 0, 1.0, 0.0)
        nxt = u_ref[pl.ds(jnp.minimum(r0 + rows, n_rows - 1), 1), :] * jnp.where(c < n_chunks - 1, 1.0, 0.0)
        ridx = lax.broadcasted_iota(jnp.int32, cur.shape, 0)
        up = jnp.where(ridx == 0, prev, pltpu.roll(cur, 1, 0))
        down = jnp.where(ridx == rows - 1, nxt, pltpu.roll(cur, rows - 1, 0))
        w = w_ref[...]
        return w[0:1] * up + w[1:2] * cur + w[2:3] * down + b_ref[...]

    def step(c, carry):
        r0 = pl.multiple_of(c * rows, rows)
        x0_ref[pl.ds(r0, rows), :] = conv(u0_ref, w0_ref, b0_ref, r0, c)
        p_ref[pl.ds(r0, rows), :] = conv(u2_ref, w2_ref, b2_ref, r0, c) * conv(u1_ref, w1_ref, b1_ref, r0, c)
        return carry

    lax.fori_loop(0, n_chunks, step, 0)


def _short_conv_gate(u, conv_w, conv_b, *, dc=128, rows=512):
    l, d3 = u.shape
    d = d3 // 3
    nb = d // dc
    rows = _pick(l, rows, SUBLANES)
    us = [pl.BlockSpec((l, dc), lambda j, s=s: (0, j + s * nb)) for s in range(3)]
    ws = [pl.BlockSpec((3, dc), lambda j, s=s: (0, j + s * nb)) for s in range(3)]
    bs = [pl.BlockSpec((1, dc), lambda j, s=s: (0, j + s * nb)) for s in range(3)]
    out = pl.BlockSpec((l, dc), lambda j: (0, j))
    return pl.pallas_call(
        functools.partial(_short_conv_body, rows=rows),
        grid=(nb,),
        in_specs=[*us, *ws, *bs],
        out_specs=[out, out],
        out_shape=[jax.ShapeDtypeStruct((l, d), F32), jax.ShapeDtypeStruct((l, d), F32)],
        compiler_params=_params("parallel"),
        name="hyena_short_conv",
    )(u, u, u, conv_w, conv_w, conv_w, conv_b[None], conv_b[None], conv_b[None])


def _filter_body(z_ref, w_in_ref, b_in_ref, w_mid_ref, b_mid_ref, fq_ref, w_out_ref, dl_ref,
                 k_ref, ss_ref, *, n_inner):
    first = jnp.logical_and(pl.program_id(0) == 0, pl.program_id(1) == 0)

    @pl.when(first)
    def _():
        ss_ref[...] = jnp.zeros_like(ss_ref)

    z = z_ref[...]
    fq = fq_ref[...]
    hdn = jnp.sin(fq * (jnp.dot(z, w_in_ref[...], precision=HIGHEST, preferred_element_type=F32)
                        + b_in_ref[...]))
    for j in range(n_inner):
        hdn = jnp.sin(fq * (jnp.dot(hdn, w_mid_ref[j], precision=HIGHEST, preferred_element_type=F32)
                            + b_mid_ref[j:j + 1]))
    filt = _dot_split(hdn, w_out_ref[...])
    t = z[:, 0:1]
    valid = z[:, LANES - 1:LANES]
    kern = filt * jnp.exp(-t * dl_ref[...]) * valid
    k_ref[...] = kern
    ss_ref[...] += jnp.sum(kern * kern, axis=0, keepdims=True)


def _hyena_kernel(l, d, pe_w_in, pe_b_in, pe_w_mid, pe_b_mid, pe_w_out, sin_freq, *, tl=256):
    bands = (HY_EMB - 1) // 2
    hidden = pe_w_in.shape[1]
    n_inner = pe_w_mid.shape[0]
    pos = jnp.concatenate([jnp.arange(l), l - jnp.arange(l)]).astype(F32)
    valid = jnp.ones((2 * l,), F32).at[l].set(0.0)
    tt = pos / (l - 1)
    fr = jnp.linspace(1e-4, bands - 1, bands, dtype=F32)
    wpos = 2.0 * math.pi * pos[:, None] / l
    z = jnp.concatenate([tt[:, None], jnp.cos(fr * wpos), -jnp.sin(fr * wpos)], axis=-1)
    z = jnp.concatenate([z, jnp.zeros((2 * l, LANES - HY_EMB - 1), F32), valid[:, None]], axis=-1)
    w_in = jnp.zeros((LANES, hidden), F32).at[:HY_EMB].set(pe_w_in)
    max_decay = math.log(HY_DECAY_TARGET) / HY_FAST_DECAY
    min_decay = math.log(HY_DECAY_TARGET) / HY_SLOW_DECAY
    deltas = jnp.abs(jnp.linspace(min_decay, max_decay, d, dtype=F32))[None]
    tl = _pick(l, tl, SUBLANES)
    nt = l // tl
    const = lambda shape: pl.BlockSpec(shape, lambda a, i: tuple(0 for _ in shape))
    return pl.pallas_call(
        functools.partial(_filter_body, n_inner=n_inner),
        grid=(2, nt),
        in_specs=[pl.BlockSpec((tl, LANES), lambda a, i: (a * nt + i, 0)),
                  const((LANES, hidden)), const((1, hidden)),
                  const((n_inner, hidden, hidden)), const((n_inner, hidden)), const((1, hidden)),
                  pl.BlockSpec((hidden, d), lambda a, i: (0, a)),
                  const((1, d))],
        out_specs=[pl.BlockSpec((tl, d), lambda a, i: (a * nt + i, 0)),
                   pl.BlockSpec((1, d), lambda a, i: (0, 0))],
        out_shape=[jax.ShapeDtypeStruct((2 * l, d), F32), jax.ShapeDtypeStruct((1, d), F32)],
        compiler_params=_params("arbitrary", "arbitrary"),
        name="hyena_filter",
    )(z, w_in, pe_b_in[None], pe_w_mid, pe_b_mid, sin_freq[None], pe_w_out, deltas)


def _conv_tables(l):
    n = 2 * l
    f = DFT_F
    s = n // f
    ar = jnp.arange(s, dtype=jnp.int32)
    c, sn = _cis(ar[:, None] * ar[None, :], s)
    fwd = jnp.stack([c, -sn], axis=1)
    p1_full = _page_matrix(fwd.reshape(2 * s, s))
    p1_half = _page_matrix(fwd[:, :, :s // 2].reshape(2 * s, s // 2))
    inv = jnp.stack([c, -sn], axis=-1)[:s // 2] / n
    p2 = _page_matrix(inv.reshape(s // 2, 2 * s))
    cos, sin = _twiddle_grid(s, f, n)
    g1 = jnp.concatenate([jnp.concatenate([cos, sin], axis=2),
                          jnp.concatenate([-sin, cos], axis=2)], axis=1).astype(BF16)
    cos_t, sin_t = jnp.swapaxes(cos, 1, 2), jnp.swapaxes(sin, 1, 2)
    g2 = jnp.concatenate([jnp.concatenate([cos_t, -sin_t], axis=2),
                          jnp.concatenate([sin_t, cos_t], axis=2)], axis=1).astype(BF16)
    return p1_half, p1_full, p2, g1, g2


def _spectral_conv_body(g1_ref, g2_ref, a_ref, ka_ref, o_ref):
    g1 = g1_ref[...]
    x = jnp.dot(g1, a_ref[...].astype(BF16), preferred_element_type=F32)
    kf = jnp.dot(g1, ka_ref[...].astype(BF16), preferred_element_type=F32)
    f = x.shape[0] // 2
    xr, xi, kr, ki = x[:f], x[f:], kf[:f], kf[f:]
    y = jnp.concatenate([xr * kr - xi * ki, xr * ki + xi * kr], axis=0).astype(BF16)
    o_ref[...] = jnp.dot(g2_ref[...], y, preferred_element_type=F32)


def _batched_stage(name, body, mats, arrays, *, dc=2048):
    nb, rows, d = arrays[0].shape
    dc = _pick(d, dc)
    mspecs = [pl.BlockSpec((None,) + m.shape[1:], lambda b, j: (b, 0, 0)) for m in mats]
    aspecs = [pl.BlockSpec((None, rows, dc), lambda b, j: (b, 0, j)) for _ in arrays]
    return pl.pallas_call(
        body,
        grid=(nb, d // dc),
        in_specs=[*mspecs, *aspecs],
        out_specs=pl.BlockSpec((None, mats[-1].shape[1], dc), lambda b, j: (b, 0, j)),
        out_shape=jax.ShapeDtypeStruct((nb, mats[-1].shape[1], d), F32),
        compiler_params=_params("parallel", "parallel"),
        name=name,
    )(*mats, *arrays)


def _conv_out_body(m_ref, b_ref, p_ref, x0_ref, ss_ref, bias_ref, o_ref):
    b = b_ref[...]
    b2 = b.reshape(-1, b.shape[-1]).astype(BF16)
    y = jnp.dot(m_ref[...], b2, preferred_element_type=F32).reshape(p_ref.shape)
    z = y * lax.rsqrt(ss_ref[...] + EPS) + p_ref[...] * bias_ref[...]
    o_ref[...] = (x0_ref[...] * z).astype(o_ref.dtype)


def _hyena(h, w_in, conv_w, conv_b, pe_w_in, pe_b_in, pe_w_mid, pe_b_mid, pe_w_out, sin_freq, bias,
           w_out, x, gate_a):
    l, d = h.shape
    f = DFT_F
    s = 2 * l // f
    tm, tn = _pick(l, 1024), _pick(3 * d, 512)
    u = _matmul("hyena_in", h, [w_in], m=l, n=3 * d, k=d, tm=tm, tn=tn, tk=d,
                out_shape=jax.ShapeDtypeStruct((l, 3 * d), F32),
                out_spec=pl.BlockSpec((tm, tn), lambda i, j, kk: (i, j)))
    x0c, p = _short_conv_gate(u, conv_w, conv_b)
    kern, ss = _hyena_kernel(l, d, pe_w_in, pe_b_in, pe_w_mid, pe_b_mid, pe_w_out, sin_freq)
    p1_half, p1_full, p2, g1, g2 = _conv_tables(l)
    ka = _page_mm(p1_full, kern.reshape(s, 1, f, d), s, 2)
    pa = _page_mm(p1_half, p.reshape(s // 2, 1, f, d), s, 2)
    pb = _batched_stage("hyena_spectral_conv", _spectral_conv_body, [g1, g2],
                        [pa.reshape(s, 2 * f, d), ka.reshape(s, 2 * f, d)])
    dc = _pick(d, 512)
    page = pl.BlockSpec((s // 2, SUBLANES, dc), lambda a, b: (0, a, b))
    chan = pl.BlockSpec((1, dc), lambda a, b: (0, b))
    y = pl.pallas_call(
        _conv_out_body,
        grid=(f // SUBLANES, d // dc),
        in_specs=[pl.BlockSpec(p2.shape, lambda a, b: (0, 0)),
                  pl.BlockSpec((s, 2, SUBLANES, dc), lambda a, b: (0, 0, a, b)),
                  page, page, chan, chan],
        out_specs=page,
        out_shape=jax.ShapeDtypeStruct((s // 2, f, d), BF16),
        compiler_params=_params("parallel", "parallel"),
        name="hyena_conv_out",
    )(p2, pb.reshape(s, 2, f, d), p.reshape(s // 2, f, d), x0c.reshape(s // 2, f, d), ss, bias[None])
    return _proj_residual("hyena_out", y.reshape(l, d), w_out, x, gate_a)


def _pool_body(h_ref, o_ref, pad_ref, *, rows, per_group, halo):
    n_rows = h_ref.shape[0]
    dc = h_ref.shape[1]
    pad_ref[pl.ds(0, halo), :] = jnp.zeros((halo, dc), F32)
    pad_ref[pl.ds(halo + n_rows, halo), :] = jnp.zeros((halo, dc), F32)
    pad_ref[pl.ds(halo, n_rows), :] = h_ref[...]
    group = pl.program_id(0) // per_group

    for g, w in enumerate(POOL_SIZES):
        @pl.when(group == g)
        def _(w=w):
            before, after = w // 2, w - w // 2

            def step(c, carry):
                r0 = pl.multiple_of(c * rows, rows)
                ext = pad_ref[pl.ds(r0, rows + 2 * halo), :]
                tot = ext[halo - before:halo - before + rows]
                for o in range(1 - before, after):
                    tot = tot + ext[halo + o:halo + o + rows]
                t = r0 + lax.broadcasted_iota(jnp.int32, (rows, dc), 0)
                cnt = jnp.minimum(t + after, n_rows) - jnp.maximum(t - before, 0)
                cur = ext[halo:halo + rows]
                o_ref[pl.ds(r0, rows), :] = (tot / cnt.astype(F32) - cur).astype(o_ref.dtype)
                return carry

            lax.fori_loop(0, n_rows // rows, step, 0)


def _pool_features(h, *, dc=256, rows=256):
    l, d = h.shape
    group = d // len(POOL_SIZES)
    dc = _pick(group, dc)
    rows = _pick(l, rows, SUBLANES)
    halo = max(POOL_SIZES) // 2
    return pl.pallas_call(
        functools.partial(_pool_body, rows=rows, per_group=group // dc, halo=halo),
        grid=(d // dc,),
        in_specs=[pl.BlockSpec((l, dc), lambda j: (0, j))],
        out_specs=pl.BlockSpec((l, dc), lambda j: (0, j)),
        out_shape=jax.ShapeDtypeStruct((l, d), BF16),
        scratch_shapes=[pltpu.VMEM((l + 2 * halo, dc), F32)],
        compiler_params=_params("parallel"),
        name="pool_features",
    )(h)


def _pool_mixer(h, w_groups, scale, x, gate_a, *, tm=1024, tn=512):
    l, d = h.shape
    n_g, group, _ = w_groups.shape
    pooled = _pool_features(h)
    tm, tn = _pick(l, tm), _pick(group, tn)
    per = group // tn
    return _matmul(
        "pool_proj", pooled, [w_groups], m=l, n=d, k=group, tm=tm, tn=tn, tk=group,
        out_shape=jax.ShapeDtypeStruct((l, d), F32),
        out_spec=pl.BlockSpec((tm, tn), lambda i, j, kk: (i, j)),
        a_spec=pl.BlockSpec((tm, group), lambda i, j, kk: (i, j // per)),
        b_specs=[pl.BlockSpec((None, group, tn), lambda i, j, kk: (j // per, 0, j % per))],
        extras=[x, gate_a, scale],
        extra_specs=[pl.BlockSpec((tm, tn), lambda i, j, kk: (i, j)),
                     pl.BlockSpec((1, tn), lambda i, j, kk: (0, j)),
                     pl.BlockSpec((1, tn), lambda i, j, kk: (0, j))],
        epilogue=lambda accs, ex: ex[0][...] + ex[1][...] * (accs[0] * ex[2][...]))


def _fourier_tables(l, group):
    f = DFT_F
    s = l // f
    ci = jnp.arange(group, dtype=jnp.int32)[:, None]
    co = jnp.arange(2 * group, dtype=jnp.int32)[None, :]
    thc = _angles(ci * (co % group), group)
    wc = jnp.where(co < group, jnp.cos(thc), -jnp.sin(thc)).astype(BF16)
    ar = jnp.arange(s, dtype=jnp.int32)
    c, sn = _cis(ar[:, None] * ar[None, :], s)
    stage1 = jnp.stack([jnp.stack([c, sn], axis=-1), jnp.stack([-sn, c], axis=-1)], axis=1)
    p1 = _page_matrix(stage1.reshape(2 * s, 2 * s))
    cos, sin = _twiddle_grid(s, f, l)
    g = jnp.concatenate([cos, sin], axis=2).reshape(s // SUBLANES, SUBLANES, f, 2 * f)
    rows = g.transpose(0, 2, 1, 3).reshape(s // SUBLANES, f * SUBLANES, 2 * f)
    row = jnp.arange(f * SUBLANES, dtype=jnp.int32)[None, :, None]
    col = jnp.arange(SUBLANES * 2 * f, dtype=jnp.int32)[None, None, :]
    scat = jnp.where(row % SUBLANES == col // (2 * f), jnp.tile(rows, (1, 1, SUBLANES)), 0.0).astype(BF16)
    return wc, p1, scat


def _fourier_out_body(m_ref, a_ref, o_ref, *, scale):
    y = jnp.dot(m_ref[...], a_ref[...].astype(BF16), preferred_element_type=F32) * scale
    o_ref[...] = y.reshape(o_ref.shape)


def _fourier_mixer(h, w_out, x, gate_a, *, tm=1024, tn=512, dc=1024):
    l, d = h.shape
    group = d // FN_GROUPS
    f = DFT_F
    s = l // f
    wc, p1, scat = _fourier_tables(l, group)
    tm, tn = _pick(l, tm, f), _pick(group, tn)
    per = group // tn
    z = _matmul(
        "fourier_chan", h, [wc], m=l, n=2 * d, k=group, tm=tm, tn=tn, tk=group,
        out_shape=jax.ShapeDtypeStruct((s, 2, f, d), F32),
        out_spec=pl.BlockSpec((tm // f, None, f, tn),
                              lambda i, j, kk: (i, (j // per) % 2, 0, (j // (2 * per)) * per + j % per)),
        a_spec=pl.BlockSpec((tm, group), lambda i, j, kk: (i, j // (2 * per))),
        b_specs=[pl.BlockSpec((group, tn), lambda i, j, kk: (0, j % (2 * per)))])
    a2 = _page_mm(p1, z, s, 2)
    dc = _pick(d, dc)
    q = s // SUBLANES
    mixed = pl.pallas_call(
        functools.partial(_fourier_out_body, scale=1.0 / math.sqrt(l * group)),
        grid=(q, d // dc),
        in_specs=[pl.BlockSpec((None,) + scat.shape[1:], lambda a, b: (a, 0, 0)),
                  pl.BlockSpec((None, SUBLANES * 2 * f, dc), lambda a, b: (a, 0, b))],
        out_specs=pl.BlockSpec((f, None, SUBLANES, dc), lambda a, b: (0, a, 0, b)),
        out_shape=jax.ShapeDtypeStruct((f, q, SUBLANES, d), F32),
        compiler_params=_params("parallel", "parallel"),
        name="fourier_seq_out",
    )(scat, a2.reshape(q, SUBLANES * 2 * f, d))
    return _proj_residual("fourier_out", mixed.reshape(l, d), w_out, x, gate_a, tm=512)


def kernel(x, c, ctx, c_ctx, ada_down, ada_up, ada_b, norm_g, final_g, da_w_qkv, da_w_o, da_lambda, da_subln_g, hy_w_in, hy_conv_w, hy_conv_b, hy_pe_w_in, hy_pe_b_in, hy_pe_w_mid, hy_pe_b_mid, hy_pe_w_out, hy_sin_freq, hy_bias, hy_w_out, pool_w, pool_scale, fn_w_out, ffn_w_gate, ffn_w_up, ffn_w_down, moe_router, moe_router_b, moe_w_gate, moe_w_up, moe_w_down):
    batch, l, d = x.shape
    depth = ada_down.shape[0]
    assert batch == 1
    xs = x[0]
    mod = _adaln_all(c, c_ctx, ada_down, ada_up, ada_b)
    ffn_w_down_bf = ffn_w_down.astype(BF16)
    for i in range(depth):
        shift_a, scale_a, gate_a, shift_f, scale_f, gate_f = [
            mod[i, 0:1, n * d:(n + 1) * d] for n in range(N_MOD)]
        g_a, g_f = norm_g[i, 0][None], norm_g[i, 1][None]
        mixer = i % 4
        if mixer == 0:
            h = _norm_mod(xs, g_a, shift_a, scale_a, BF16)
            hc = _norm_mod(ctx[0], g_a, mod[i, 1:2, 0:d], mod[i, 1:2, d:2 * d], BF16)
            xs = _diff_attention(h, hc, da_w_qkv, da_w_o, da_lambda, da_subln_g[None],
                                 0.8 - 0.6 * math.exp(-0.3 * i), xs, gate_a)
        elif mixer == 1:
            h = _norm_mod(xs, g_a, shift_a, scale_a, BF16)
            xs = _hyena(h, hy_w_in, hy_conv_w, hy_conv_b, hy_pe_w_in, hy_pe_b_in, hy_pe_w_mid,
                        hy_pe_b_mid, hy_pe_w_out, hy_sin_freq, hy_bias, hy_w_out, xs, gate_a)
        elif mixer == 2:
            h = _norm_mod(xs, g_a, shift_a, scale_a, F32)
            xs = _pool_mixer(h, pool_w, pool_scale[None], xs, gate_a)
        else:
            h = _norm_mod(xs, g_a, shift_a, scale_a, BF16)
            xs = _fourier_mixer(h, fn_w_out, xs, gate_a)
        if i % 2 == 0:
            h = _norm_mod(xs, g_f, shift_f, scale_f, BF16)
            xs = _swiglu(h, ffn_w_gate, ffn_w_up, ffn_w_down_bf, i // 2, xs, gate_f)
        else:
            j = i // 2
            h, sel = _norm_router(xs, g_f, shift_f, scale_f, moe_router[j], moe_router_b[j])
            xs = _moe(h, sel, moe_w_gate, moe_w_up, moe_w_down, j, xs, gate_f)
    return _rmsnorm(xs, final_g[None])[None]
```

```python
import functools
import math

import jax
import jax.numpy as jnp
from jax import lax
from jax.experimental import pallas as pl
from jax.experimental.pallas import tpu as pltpu

F32 = jnp.float32
BF16 = jnp.bfloat16
HIGHEST = lax.Precision.HIGHEST

EPS = 1e-6
GRID_W = 64
N_MOD = 6
DA_HEADS = 16
DA_HEAD_DIM = 128
ROPE_THETA = 10000.0
HY_EMB = 33
HY_DECAY_TARGET = 1e-2
HY_FAST_DECAY = 0.3
HY_SLOW_DECAY = 1.5
POOL_SIZES = (2, 4, 8, 16)
FN_GROUPS = 4
TOP_K = 2

LANES = 128
SUBLANES = 8
DFT_F = 128
VMEM_LIMIT = 56 * 1024 * 1024


def _params(*sem):
    return pltpu.CompilerParams(dimension_semantics=sem, vmem_limit_bytes=VMEM_LIMIT)


def _pick(dim, pref, align=LANES):
    if dim <= pref:
        return dim
    t = (pref // align) * align
    while t >= align:
        if dim % t == 0:
            return t
        t -= align
    raise ValueError(f"no {align}-aligned tile of {dim} below {pref}")


def _mm_body(*refs, n_b, n_extra, nk, epilogue):
    a_ref = refs[0]
    b_refs = refs[1:1 + n_b]
    extra_refs = refs[1 + n_b:1 + n_b + n_extra]
    o_ref = refs[1 + n_b + n_extra]
    acc_refs = refs[2 + n_b + n_extra:]
    a = a_ref[...].astype(BF16)
    if nk == 1:
        accs = [jnp.dot(a, b[...].astype(BF16), preferred_element_type=F32) for b in b_refs]
        o_ref[...] = epilogue(accs, extra_refs).reshape(o_ref.shape).astype(o_ref.dtype)
        return
    k = pl.program_id(2)

    @pl.when(k == 0)
    def _():
        for acc in acc_refs:
            acc[...] = jnp.zeros_like(acc)

    for acc, b in zip(acc_refs, b_refs):
        acc[...] += jnp.dot(a, b[...].astype(BF16), preferred_element_type=F32)

    @pl.when(k == nk - 1)
    def _():
        accs = [acc[...] for acc in acc_refs]
        o_ref[...] = epilogue(accs, extra_refs).reshape(o_ref.shape).astype(o_ref.dtype)


def _matmul(name, a, bs, *, m, n, k, tm, tn, tk, out_shape, out_spec, a_spec=None, b_specs=None,
            extras=(), extra_specs=(), epilogue=None):
    nk = k // tk
    if epilogue is None:
        epilogue = lambda accs, ex: accs[0]
    if a_spec is None:
        a_spec = pl.BlockSpec((tm, tk), lambda i, j, kk: (i, kk))
    if b_specs is None:
        b_specs = [pl.BlockSpec((tk, tn), lambda i, j, kk: (kk, j)) for _ in bs]
    scratch = [pltpu.VMEM((tm, tn), F32) for _ in bs] if nk > 1 else []
    body = functools.partial(_mm_body, n_b=len(bs), n_extra=len(extras), nk=nk, epilogue=epilogue)
    return pl.pallas_call(
        body,
        grid=(m // tm, n // tn, nk),
        in_specs=[a_spec, *b_specs, *extra_specs],
        out_specs=out_spec,
        out_shape=out_shape,
        scratch_shapes=scratch,
        compiler_params=_params("parallel", "parallel", "arbitrary"),
        name=name,
    )(a, *bs, *extras)


def _residual_epilogue(accs, ex):
    return ex[0][...] + ex[1][...] * accs[0]


def _proj_residual(name, a, w, x, gate, *, tm=1024, tn=512):
    m, k = a.shape
    n = w.shape[1]
    tm, tn = _pick(m, tm), _pick(n, tn)
    return _matmul(
        name, a, [w], m=m, n=n, k=k, tm=tm, tn=tn, tk=k,
        out_shape=jax.ShapeDtypeStruct((m, n), F32),
        out_spec=pl.BlockSpec((tm, tn), lambda i, j, kk: (i, j)),
        extras=[x, gate],
        extra_specs=[pl.BlockSpec((tm, tn), lambda i, j, kk: (i, j)),
                     pl.BlockSpec((1, tn), lambda i, j, kk: (0, j))],
        epilogue=_residual_epilogue)


def _dot_split(a, b):
    a_hi, b_hi = a.astype(BF16), b.astype(BF16)
    a_lo = (a - a_hi.astype(F32)).astype(BF16)
    b_lo = (b - b_hi.astype(F32)).astype(BF16)
    dot = functools.partial(jnp.dot, preferred_element_type=F32)
    return dot(a_hi, b_hi) + (dot(a_hi, b_lo) + dot(a_lo, b_hi))


def _ada_body(a_ref, w_ref, b_ref, o_ref, *, silu):
    a = a_ref[...]
    if silu:
        a = a * jax.nn.sigmoid(a)
    o_ref[...] = _dot_split(a, w_ref[...]) + b_ref[...]


def _adaln_all(c, c_ctx, ada_down, ada_up, ada_b):
    depth, d, r = ada_down.shape
    width = ada_up.shape[2]
    rows = jnp.zeros((SUBLANES, d), F32).at[0].set(c[0]).at[1].set(c_ctx)
    tn1 = _pick(r, 256)
    t = pl.pallas_call(
        functools.partial(_ada_body, silu=True),
        grid=(depth, r // tn1),
        in_specs=[pl.BlockSpec((SUBLANES, d), lambda l, j: (0, 0)),
                  pl.BlockSpec((None, d, tn1), lambda l, j: (l, 0, j)),
                  pl.BlockSpec((1, tn1), lambda l, j: (0, j))],
        out_specs=pl.BlockSpec((None, SUBLANES, tn1), lambda l, j: (l, 0, j)),
        out_shape=jax.ShapeDtypeStruct((depth, SUBLANES, r), F32),
        compiler_params=_params("parallel", "parallel"),
        name="adaln_down",
    )(rows, ada_down, jnp.zeros((1, r), F32))
    tn2 = _pick(width, 2048)
    return pl.pallas_call(
        functools.partial(_ada_body, silu=False),
        grid=(depth, width // tn2),
        in_specs=[pl.BlockSpec((None, SUBLANES, r), lambda l, j: (l, 0, 0)),
                  pl.BlockSpec((None, r, tn2), lambda l, j: (l, 0, j)),
                  pl.BlockSpec((None, 1, tn2), lambda l, j: (l, 0, j))],
        out_specs=pl.BlockSpec((None, SUBLANES, tn2), lambda l, j: (l, 0, j)),
        out_shape=jax.ShapeDtypeStruct((depth, SUBLANES, width), F32),
        compiler_params=_params("parallel", "parallel"),
        name="adaln_up",
    )(t, ada_up, ada_b[:, None, :])


def _norm_rows(x, g):
    return x * lax.rsqrt(jnp.mean(x * x, axis=-1, keepdims=True) + EPS) * g


def _norm_mod_body(x_ref, g_ref, shift_ref, scale_ref, o_ref):
    y = _norm_rows(x_ref[...], g_ref[...])
    o_ref[...] = (y * (1.0 + scale_ref[...]) + shift_ref[...]).astype(o_ref.dtype)


def _norm_mod(x, g, shift, scale, out_dtype, *, tm=256):
    m, d = x.shape
    tm = _pick(m, tm, SUBLANES)
    row = pl.BlockSpec((1, d), lambda i: (0, 0))
    return pl.pallas_call(
        _norm_mod_body,
        grid=(m // tm,),
        in_specs=[pl.BlockSpec((tm, d), lambda i: (i, 0)), row, row, row],
        out_specs=pl.BlockSpec((tm, d), lambda i: (i, 0)),
        out_shape=jax.ShapeDtypeStruct((m, d), out_dtype),
        compiler_params=_params("parallel"),
        name="norm_mod",
    )(x, g, shift, scale)


def _rmsnorm_body(x_ref, g_ref, o_ref):
    o_ref[...] = _norm_rows(x_ref[...], g_ref[...])


def _rmsnorm(x, g, *, tm=256):
    m, d = x.shape
    tm = _pick(m, tm, SUBLANES)
    return pl.pallas_call(
        _rmsnorm_body,
        grid=(m // tm,),
        in_specs=[pl.BlockSpec((tm, d), lambda i: (i, 0)), pl.BlockSpec((1, d), lambda i: (0, 0))],
        out_specs=pl.BlockSpec((tm, d), lambda i: (i, 0)),
        out_shape=jax.ShapeDtypeStruct((m, d), F32),
        compiler_params=_params("parallel"),
        name="final_norm",
    )(x, g)


def _pack_bf16_pairs(x):
    half = x.shape[1] // 2
    bits = pltpu.bitcast(x.astype(BF16).astype(F32), jnp.uint32)
    return (bits[:, half:] & jnp.uint32(0xFFFF0000)) | (bits[:, :half] >> 16)


def _unpack_bf16_pairs(w):
    lo = pltpu.bitcast(w << 16, F32)
    hi = pltpu.bitcast(w & jnp.uint32(0xFFFF0000), F32)
    return jnp.concatenate([lo, hi], axis=1)


def _norm_router_body(x_ref, g_ref, shift_ref, scale_ref, rw_ref, rb_ref, h_ref, sel_ref):
    h = _norm_rows(x_ref[...], g_ref[...]) * (1.0 + scale_ref[...]) + shift_ref[...]
    h_ref[...] = _pack_bf16_pairs(h)
    logits = jnp.dot(h, rw_ref[...], precision=HIGHEST, preferred_element_type=F32) + rb_ref[...]
    lane = lax.broadcasted_iota(jnp.int32, logits.shape, 1)
    m1 = jnp.max(logits, axis=-1, keepdims=True)
    i1 = jnp.min(jnp.where(logits == m1, lane, LANES), axis=-1, keepdims=True)
    rest = jnp.where(lane == i1, -jnp.inf, logits)
    m2 = jnp.max(rest, axis=-1, keepdims=True)
    i2 = jnp.min(jnp.where(rest == m2, lane, LANES), axis=-1, keepdims=True)
    e = jnp.exp(m2 - m1)
    w1 = 1.0 / (1.0 + e)
    w2 = e * w1
    sel_ref[...] = jnp.where(lane == 0, i1.astype(F32),
                             jnp.where(lane == 1, i2.astype(F32),
                                       jnp.where(lane == 2, w1, jnp.where(lane == 3, w2, 0.0))))


def _norm_router(x, g, shift, scale, router, router_b, *, tm=256):
    m, d = x.shape
    n_e = router.shape[1]
    tm = _pick(m, tm, SUBLANES)
    rw = jnp.zeros((d, LANES), F32).at[:, :n_e].set(router)
    rb = jnp.full((1, LANES), -1e30, F32).at[0, :n_e].set(router_b)
    row = pl.BlockSpec((1, d), lambda i: (0, 0))
    return pl.pallas_call(
        _norm_router_body,
        grid=(m // tm,),
        in_specs=[pl.BlockSpec((tm, d), lambda i: (i, 0)), row, row, row,
                  pl.BlockSpec((d, LANES), lambda i: (0, 0)),
                  pl.BlockSpec((1, LANES), lambda i: (0, 0))],
        out_specs=[pl.BlockSpec((tm, d // 2), lambda i: (i, 0)),
                   pl.BlockSpec((tm, LANES), lambda i: (i, 0))],
        out_shape=[jax.ShapeDtypeStruct((m, d // 2), jnp.uint32),
                   jax.ShapeDtypeStruct((m, LANES), F32)],
        compiler_params=_params("parallel"),
        name="norm_router",
    )(x, g, shift, scale, rw, rb)


def _silu(a):
    return a * jax.nn.sigmoid(a)


def _swiglu(h, w_gate, w_up, w_down, layer, x, gate_f):
    m, d = h.shape
    f = w_gate.shape[2]
    tm = _pick(m, 2048)
    tn = _pick(f, 256)
    wspec = pl.BlockSpec((None, d, tn), lambda i, j, kk: (layer, 0, j))
    hid = _matmul(
        "swiglu_up", h, [w_gate, w_up], m=m, n=f, k=d, tm=tm, tn=tn, tk=d,
        out_shape=jax.ShapeDtypeStruct((m, f), BF16),
        out_spec=pl.BlockSpec((tm, tn), lambda i, j, kk: (i, j)),
        a_spec=pl.BlockSpec((tm, d), lambda i, j, kk: (i, 0), pipeline_mode=pl.Buffered(1)),
        b_specs=[wspec, wspec],
        epilogue=lambda accs, ex: _silu(accs[0]) * accs[1])
    tm2 = _pick(m, 1024)
    tn2 = _pick(d, 256)
    return _matmul(
        "swiglu_down", hid, [w_down], m=m, n=d, k=f, tm=tm2, tn=tn2, tk=f,
        out_shape=jax.ShapeDtypeStruct((m, d), F32),
        out_spec=pl.BlockSpec((tm2, tn2), lambda i, j, kk: (i, j)),
        a_spec=pl.BlockSpec((tm2, f), lambda i, j, kk: (i, 0), pipeline_mode=pl.Buffered(1)),
        b_specs=[pl.BlockSpec((None, f, tn2), lambda i, j, kk: (layer, 0, j))],
        extras=[x, gate_f],
        extra_specs=[pl.BlockSpec((tm2, tn2), lambda i, j, kk: (i, j)),
                     pl.BlockSpec((1, tn2), lambda i, j, kk: (0, j))],
        epilogue=_residual_epilogue)


MOE_TILE = 512


def _moe_plan(sel, n_e, tile):
    l = sel.shape[0]
    e = jnp.concatenate([sel[:, 0], sel[:, 1]]).astype(jnp.int32)
    onehot = (e[:, None] == jnp.arange(n_e, dtype=jnp.int32)[None, :]).astype(jnp.int32)
    incl = jnp.cumsum(onehot, axis=0)
    counts = incl[-1]
    padded = ((counts + tile - 1) // tile) * tile
    ends = jnp.cumsum(padded)
    starts = ends - padded
    dest = jnp.sum(onehot * (starts[None, :] + incl - 1), axis=1)
    n_rows = 2 * l + n_e * tile
    n_tiles = n_rows // tile
    token = jnp.tile(jnp.arange(l, dtype=jnp.int32), 2)
    tok = jnp.zeros((n_rows,), jnp.int32).at[dest].set(token)
    n_active = ends[-1] // tile
    t_clamped = jnp.minimum(jnp.arange(n_tiles, dtype=jnp.int32), n_active - 1)
    tile_expert = jnp.sum((t_clamped[:, None] * tile >= ends[None, :]).astype(jnp.int32), axis=1)
    return dest, tok, tile_expert, n_active.reshape(1).astype(jnp.int32)


def _prefetched_rows(copies, n_rows, n_steps):
    i = pl.program_id(0)
    slot = i % 2

    def issue(step, into):
        def body(r, carry):
            for cp in copies(step, r, into):
                cp.start()
            return carry
        lax.fori_loop(0, n_rows, body, 0, unroll=8)

    def drain(r, carry):
        for cp in copies(i, r, slot):
            cp.wait()
        return carry

    @pl.when(i == 0)
    def _():
        issue(0, 0)

    @pl.when(i + 1 < n_steps)
    def _():
        issue(i + 1, 1 - slot)

    lax.fori_loop(0, n_rows, drain, 0, unroll=8)
    return slot


def _row_gather_body(idx_ref, src_hbm, o_ref, buf, sem, *, rows, n_steps):
    def copies(step, r, slot):
        return [pltpu.make_async_copy(src_hbm.at[pl.ds(idx_ref[step * rows + r], 1), :],
                                      buf.at[slot, pl.ds(r, 1), :], sem.at[slot])]

    slot = _prefetched_rows(copies, rows, n_steps)
    o_ref[...] = _unpack_bf16_pairs(buf[slot]).astype(o_ref.dtype)


def _row_gather(src, idx, *, rows=256):
    n, d = idx.shape[0], src.shape[1]
    rows = _pick(n, rows, SUBLANES)
    n_steps = n // rows
    return pl.pallas_call(
        functools.partial(_row_gather_body, rows=rows, n_steps=n_steps),
        grid_spec=pltpu.PrefetchScalarGridSpec(
            num_scalar_prefetch=1,
            grid=(n_steps,),
            in_specs=[pl.BlockSpec(memory_space=pl.ANY)],
            out_specs=pl.BlockSpec((rows, 2 * d), lambda i, idx_ref: (i, 0)),
            scratch_shapes=[pltpu.VMEM((2, rows, d), src.dtype), pltpu.SemaphoreType.DMA((2,))]),
        out_shape=jax.ShapeDtypeStruct((n, 2 * d), BF16),
        compiler_params=_params("arbitrary"),
        name="moe_gather",
    )(idx, src)


def _grouped_body(te_ref, na_ref, a_ref, *refs, n_b, n_extra, epilogue):
    b_refs = refs[:n_b]
    extra_refs = refs[n_b:n_b + n_extra]
    o_ref = refs[n_b + n_extra]
    bf_refs = refs[n_b + n_extra + 1:]
    t = pl.program_id(1)
    active = t < na_ref[0]
    fresh = jnp.logical_or(t == 0, te_ref[t] != te_ref[jnp.maximum(t - 1, 0)])

    @pl.when(jnp.logical_and(active, fresh))
    def _():
        for b, bf in zip(b_refs, bf_refs):
            bf[...] = b[...].astype(BF16)

    @pl.when(active)
    def _():
        a = a_ref[...]
        accs = [jnp.dot(a, bf[...], preferred_element_type=F32) for bf in bf_refs]
        o_ref[...] = epilogue(accs, extra_refs).astype(o_ref.dtype)

    @pl.when(jnp.logical_not(active))
    def _():
        o_ref[...] = jnp.zeros_like(o_ref)


def _grouped_matmul(name, a, ws, layer, tile_expert, n_active, *, tile, tn, out_dtype, extras=(), epilogue=None):
    p, k = a.shape
    n = ws[0].shape[3]
    if epilogue is None:
        epilogue = lambda accs, ex: accs[0]
    out_tn = tn // 2 if out_dtype == jnp.uint32 else tn
    row = lambda j, t, te, na: jnp.minimum(t, na[0] - 1)
    return pl.pallas_call(
        functools.partial(_grouped_body, n_b=len(ws), n_extra=len(extras), epilogue=epilogue),
        grid_spec=pltpu.PrefetchScalarGridSpec(
            num_scalar_prefetch=2,
            grid=(n // tn, p // tile),
            in_specs=[pl.BlockSpec((tile, k), lambda j, t, te, na: (row(j, t, te, na), 0)),
                      *[pl.BlockSpec((None, None, k, tn), lambda j, t, te, na: (layer, te[t], 0, j)) for _ in ws],
                      *[pl.BlockSpec((tile, e.shape[1]), lambda j, t, te, na: (row(j, t, te, na), 0))
                        for e in extras]],
            out_specs=pl.BlockSpec((tile, out_tn), lambda j, t, te, na: (t, j)),
            scratch_shapes=[pltpu.VMEM((k, tn), BF16) for _ in ws]),
        out_shape=jax.ShapeDtypeStruct((p, n // tn * out_tn), out_dtype),
        compiler_params=_params("arbitrary", "arbitrary"),
        name=name,
    )(tile_expert, n_active, a, *ws, *extras)


def _combine_body(pos_ref, y_hbm, x_ref, g_ref, sel_ref, ng_ref, o_ref, buf, sem, *,
                  rows, n_tok, n_steps, pack_block, final_norm):
    def copies(step, r, slot):
        return [pltpu.make_async_copy(y_hbm.at[pl.ds(pos_ref[k * n_tok + step * rows + r], 1), :],
                                      buf.at[slot, k, pl.ds(r, 1), :], sem.at[slot]) for k in range(TOP_K)]

    slot = _prefetched_rows(copies, rows, n_steps)

    def expert_rows(k):
        w = buf[slot, k]
        half = pack_block // 2
        return jnp.concatenate([_unpack_bf16_pairs(w[:, b * half:(b + 1) * half])
                                for b in range(w.shape[1] // half)], axis=1)

    sel = sel_ref[...]
    y = sel[:, 2:3] * expert_rows(0) + sel[:, 3:4] * expert_rows(1)
    out = x_ref[...] + g_ref[...] * y
    o_ref[...] = _norm_rows(out, ng_ref[...]) if final_norm else out


def _moe_combine(y, dest, sel, x, gate_f, norm_g, *, pack_block, final_norm, rows=256):
    l, d = x.shape
    rows = _pick(l, rows, SUBLANES)
    n_steps = l // rows
    vec = pl.BlockSpec((1, d), lambda i, pos: (0, 0))
    return pl.pallas_call(
        functools.partial(_combine_body, rows=rows, n_tok=l, n_steps=n_steps, pack_block=pack_block,
                          final_norm=final_norm),
        grid_spec=pltpu.PrefetchScalarGridSpec(
            num_scalar_prefetch=1,
            grid=(n_steps,),
            in_specs=[pl.BlockSpec(memory_space=pl.ANY),
                      pl.BlockSpec((rows, d), lambda i, pos: (i, 0)),
                      vec,
                      pl.BlockSpec((rows, LANES), lambda i, pos: (i, 0)),
                      vec],
            out_specs=pl.BlockSpec((rows, d), lambda i, pos: (i, 0)),
            scratch_shapes=[pltpu.VMEM((2, TOP_K, rows, d // 2), jnp.uint32), pltpu.SemaphoreType.DMA((2,))]),
        out_shape=jax.ShapeDtypeStruct((l, d), F32),
        compiler_params=_params("arbitrary"),
        name="moe_combine",
    )(dest, y, x, gate_f, sel, norm_g)


def _moe(h, sel, w_gate, w_up, w_down, layer, x, gate_f, norm_g, final_norm):
    _, n_e, d, fe = w_gate.shape
    tile = MOE_TILE
    dest, tok, tile_expert, n_active = _moe_plan(sel, n_e, tile)
    hs = _row_gather(h, tok)
    hid = _grouped_matmul(
        "moe_up", hs, [w_gate, w_up], layer, tile_expert, n_active, tile=tile, tn=_pick(fe, 256),
        out_dtype=BF16, epilogue=lambda accs, ex: _silu(accs[0]) * accs[1])
    tn = _pick(d, 1024)
    y = _grouped_matmul("moe_down", hid, [w_down], layer, tile_expert, n_active, tile=tile, tn=tn,
                        out_dtype=jnp.uint32, epilogue=lambda accs, ex: _pack_bf16_pairs(accs[0]))
    return _moe_combine(y, dest, sel, x, gate_f, norm_g, pack_block=tn, final_norm=final_norm)


def _rope_tables(n_tokens):
    pairs = DA_HEAD_DIM // 4
    rows = n_tokens // GRID_W
    row = jnp.repeat(jnp.arange(rows, dtype=F32), GRID_W)
    col = jnp.tile(jnp.arange(GRID_W, dtype=F32), rows)
    inv = ROPE_THETA ** (-jnp.arange(pairs, dtype=F32) / pairs)
    ang = jnp.concatenate([row[:, None] * inv, col[:, None] * inv], axis=-1)
    cos, sin = jnp.cos(ang), jnp.sin(ang)
    return jnp.concatenate([cos, cos], axis=-1), jnp.concatenate([-sin, sin], axis=-1)


def _qkv_rope(h, w_qkv, cos2, sin2, *, tm=1024, tn=512):
    m, d = h.shape
    n = w_qkv.shape[1]
    tm, tn = _pick(m, tm), _pick(d, tn)
    n_q = d // tn
    half = DA_HEAD_DIM // 2

    def epi(accs, ex):
        acc = accs[0]
        j = pl.program_id(1)
        cos, sin = ex[0][...], ex[1][...]
        heads = []
        for t in range(tn // DA_HEAD_DIM):
            xh = acc[:, t * DA_HEAD_DIM:(t + 1) * DA_HEAD_DIM]
            heads.append(xh * cos + pltpu.roll(xh, half, 1) * sin)
        roped = jnp.concatenate(heads, axis=1)
        qscale = jnp.where(j < n_q, DA_HEAD_DIM ** -0.5 * math.log2(math.e), 1.0).astype(F32)
        return jnp.where(j < 2 * n_q, roped * qscale, acc)

    tab = pl.BlockSpec((tm, DA_HEAD_DIM), lambda i, j, kk: (i, 0))
    return _matmul(
        "qkv_rope", h, [w_qkv], m=m, n=n, k=d, tm=tm, tn=tn, tk=d,
        out_shape=jax.ShapeDtypeStruct((m, n), BF16),
        out_spec=pl.BlockSpec((tm, tn), lambda i, j, kk: (i, j)),
        extras=[cos2, sin2], extra_specs=[tab, tab], epilogue=epi)


def _flash_body(lam_ref, g_ref, q_ref, kt_ref, v_ref, o_ref,
                m_sc, l_sc, acc_sc, p0_sc, p1_sc, a0_sc, a1_sc, *, tk, nkv, lambda_init):
    n_val = v_ref.shape[1] // LANES
    n_lane_tiles = tk // LANES
    m_sc[...] = jnp.full_like(m_sc, -jnp.inf)
    l_sc[...] = jnp.zeros_like(l_sc)
    acc_sc[...] = jnp.zeros_like(acc_sc)

    def accumulate(j, p_r, a_r):
        v = v_ref[pl.ds(pl.multiple_of(j * tk, tk), tk), :]
        for c in range(2):
            alpha = jnp.concatenate([a_r[c]] * n_val, axis=1)
            acc_sc[c] = acc_sc[c] * alpha + jnp.dot(p_r[c], v, preferred_element_type=F32)

    def score(j, p_w, a_w):
        q = q_ref[...]
        kt = kt_ref[j]
        for c in range(2):
            qc = q[:, c * DA_HEAD_DIM:(c + 1) * DA_HEAD_DIM]
            kc = kt[c * DA_HEAD_DIM:(c + 1) * DA_HEAD_DIM, :]
            s = jnp.dot(qc, kc, preferred_element_type=F32)
            tiles = [s[:, t * LANES:(t + 1) * LANES] for t in range(n_lane_tiles)]
            m_prev = m_sc[c]
            m_next = jnp.maximum(m_prev, jnp.max(functools.reduce(jnp.maximum, tiles), axis=1, keepdims=True))
            alpha = jnp.exp2(m_prev - m_next)
            ps = [jnp.exp2(t - m_next) for t in tiles]
            l_sc[c] = alpha * l_sc[c] + functools.reduce(jnp.add, ps)
            m_sc[c] = m_next
            a_w[c] = alpha
            p_w[c] = jnp.concatenate(ps, axis=1).astype(BF16)

    even, odd = (p0_sc, a0_sc), (p1_sc, a1_sc)
    score(0, *even)

    def pair(i, carry):
        j = 2 * i + 1
        score(j, *odd)
        accumulate(j - 1, *even)
        score(j + 1, *even)
        accumulate(j, *odd)
        return carry

    lax.fori_loop(0, (nkv - 1) // 2, pair, 0)
    if nkv % 2 == 0:
        score(nkv - 1, *odd)
        accumulate(nkv - 2, *even)
        accumulate(nkv - 1, *odd)
    else:
        accumulate(nkv - 1, *even)

    lam = lam_ref[...]
    lam_full = (jnp.exp(jnp.sum(lam[0:1] * lam[1:2], axis=-1, keepdims=True))
                - jnp.exp(jnp.sum(lam[2:3] * lam[3:4], axis=-1, keepdims=True)) + lambda_init)
    l0 = jnp.sum(l_sc[0], axis=1, keepdims=True)
    l1 = jnp.sum(l_sc[1], axis=1, keepdims=True)
    o = acc_sc[0] / l0 - lam_full * (acc_sc[1] / l1)
    o_ref[...] = (_norm_rows(o, g_ref[...]) * (1.0 - lambda_init)).astype(o_ref.dtype)


def _diff_flash(q, k_all, v_all, lam, subln_g, lambda_init, *, tq=1024, tk=768):
    l = q.shape[0]
    t, d = v_all.shape
    hw = 2 * DA_HEAD_DIM
    n_heads = d // hw
    tq = _pick(l, tq)
    tk = _pick(t, tk)
    nkv = t // tk
    kt = k_all.reshape(nkv, tk, n_heads, hw).transpose(2, 0, 3, 1)
    return pl.pallas_call(
        functools.partial(_flash_body, tk=tk, nkv=nkv, lambda_init=lambda_init),
        grid=(n_heads, l // tq),
        in_specs=[pl.BlockSpec(lam.shape, lambda h, i: (0, 0)),
                  pl.BlockSpec((1, hw), lambda h, i: (0, 0)),
                  pl.BlockSpec((tq, hw), lambda h, i: (i, h)),
                  pl.BlockSpec((None, nkv, hw, tk), lambda h, i: (h, 0, 0, 0)),
                  pl.BlockSpec((t, hw), lambda h, i: (0, h))],
        out_specs=pl.BlockSpec((tq, hw), lambda h, i: (i, h)),
        out_shape=jax.ShapeDtypeStruct((l, d), BF16),
        scratch_shapes=[pltpu.VMEM((2, tq, LANES), F32), pltpu.VMEM((2, tq, LANES), F32),
                        pltpu.VMEM((2, tq, hw), F32),
                        pltpu.VMEM((2, tq, tk), BF16), pltpu.VMEM((2, tq, tk), BF16),
                        pltpu.VMEM((2, tq, LANES), F32), pltpu.VMEM((2, tq, LANES), F32)],
        compiler_params=_params("parallel", "parallel"),
        name="diff_flash",
    )(lam, subln_g, q, kt, v_all)


def _diff_attention(h, hc, w_qkv, w_o, lam, subln_g, lambda_init, x, gate_a):
    l, d = h.shape
    cos2, sin2 = _rope_tables(l)
    qkv = _qkv_rope(h, w_qkv, cos2, sin2)
    c_len = hc.shape[0]
    tmc = _pick(c_len, 256)
    kvc = _matmul(
        "ctx_kv", hc, [w_qkv], m=c_len, n=2 * d, k=d, tm=tmc, tn=512, tk=d,
        out_shape=jax.ShapeDtypeStruct((c_len, 2 * d), BF16),
        out_spec=pl.BlockSpec((tmc, 512), lambda i, j, kk: (i, j)),
        b_specs=[pl.BlockSpec((d, 512), lambda i, j, kk: (0, j + d // 512))])
    k_all = jnp.concatenate([qkv[:, d:2 * d], kvc[:, :d]], axis=0)
    v_all = jnp.concatenate([qkv[:, 2 * d:], kvc[:, d:]], axis=0)
    o = _diff_flash(qkv, k_all, v_all, lam, subln_g, lambda_init)
    return _proj_residual("attn_out", o, w_o, x, gate_a)


def _angles(num, den):
    return (2.0 * math.pi / den) * (num % den).astype(F32)


def _cis(num, den):
    th = _angles(num, den)
    return jnp.cos(th), jnp.sin(th)


def _twiddle_grid(s, f, n):
    ar = lambda m: jnp.arange(m, dtype=jnp.int32)
    ac, asn = _cis(ar(s)[:, None] * ar(f)[None, :], n)
    bc, bsn = _cis(ar(f)[:, None] * ar(f)[None, :], f)
    cos = ac[:, None, :] * bc[None, :, :] - asn[:, None, :] * bsn[None, :, :]
    sin = asn[:, None, :] * bc[None, :, :] + ac[:, None, :] * bsn[None, :, :]
    return cos, sin


def _page_matrix(base):
    nr, nc = base.shape
    r = jnp.arange(nr * SUBLANES, dtype=jnp.int32)[:, None]
    c = jnp.arange(nc * SUBLANES, dtype=jnp.int32)[None, :]
    expand_r = (r // SUBLANES == jnp.arange(nr, dtype=jnp.int32)[None, :]).astype(F32)
    expand_c = (jnp.arange(nc, dtype=jnp.int32)[:, None] == c // SUBLANES).astype(F32)
    full = jnp.dot(jnp.dot(expand_r, base), expand_c)
    return jnp.where(r % SUBLANES == c % SUBLANES, full, 0.0).astype(BF16)


def _page_mm_body(m_ref, x_ref, o_ref):
    x = x_ref[...]
    x2 = x.reshape(-1, x.shape[-1]).astype(BF16)
    y = jnp.dot(m_ref[...], x2, preferred_element_type=F32)
    o_ref[...] = y.reshape(o_ref.shape).astype(o_ref.dtype)


def _page_mm(mat, x4, ko, po, *, dc=512):
    s, pi, f, d = x4.shape
    dc = _pick(d, dc)
    return pl.pallas_call(
        _page_mm_body,
        grid=(f // SUBLANES, d // dc),
        in_specs=[pl.BlockSpec(mat.shape, lambda a, b: (0, 0)),
                  pl.BlockSpec((s, pi, SUBLANES, dc), lambda a, b: (0, 0, a, b))],
        out_specs=pl.BlockSpec((ko, po, SUBLANES, dc), lambda a, b: (0, 0, a, b)),
        out_shape=jax.ShapeDtypeStruct((ko, po, f, d), F32),
        compiler_params=_params("parallel", "parallel"),
        name="dft_lead",
    )(mat, x4)


def _short_conv_body(u0_ref, u1_ref, u2_ref, w0_ref, w1_ref, w2_ref, b0_ref, b1_ref, b2_ref,
                     x0_ref, p_ref, *, rows):
    n_rows = u0_ref.shape[0]
    n_chunks = n_rows // rows

    def conv(u_ref, w_ref, b_ref, r0, c):
        cur = u_ref[pl.ds(r0, rows), :]
        prev = u_ref[pl.ds(jnp.maximum(r0 - 1, 0), 1), :] * jnp.where(c > 0, 1.0, 0.0)
        nxt = u_ref[pl.ds(jnp.minimum(r0 + rows, n_rows - 1), 1), :] * jnp.where(c < n_chunks - 1, 1.0, 0.0)
        ridx = lax.broadcasted_iota(jnp.int32, cur.shape, 0)
        up = jnp.where(ridx == 0, prev, pltpu.roll(cur, 1, 0))
        down = jnp.where(ridx == rows - 1, nxt, pltpu.roll(cur, rows - 1, 0))
        w = w_ref[...]
        return w[0:1] * up + w[1:2] * cur + w[2:3] * down + b_ref[...]

    def step(c, carry):
        r0 = pl.multiple_of(c * rows, rows)
        x0_ref[pl.ds(r0, rows), :] = conv(u0_ref, w0_ref, b0_ref, r0, c)
        p_ref[pl.ds(r0, rows), :] = conv(u2_ref, w2_ref, b2_ref, r0, c) * conv(u1_ref, w1_ref, b1_ref, r0, c)
        return carry

    lax.fori_loop(0, n_chunks, step, 0)


def _short_conv_gate(u, conv_w, conv_b, *, dc=128, rows=512):
    l, d3 = u.shape
    d = d3 // 3
    nb = d // dc
    rows = _pick(l, rows, SUBLANES)
    us = [pl.BlockSpec((l, dc), lambda j, s=s: (0, j + s * nb)) for s in range(3)]
    ws = [pl.BlockSpec((3, dc), lambda j, s=s: (0, j + s * nb)) for s in range(3)]
    bs = [pl.BlockSpec((1, dc), lambda j, s=s: (0, j + s * nb)) for s in range(3)]
    out = pl.BlockSpec((l, dc), lambda j: (0, j))
    return pl.pallas_call(
        functools.partial(_short_conv_body, rows=rows),
        grid=(nb,),
        in_specs=[*us, *ws, *bs],
        out_specs=[out, out],
        out_shape=[jax.ShapeDtypeStruct((l, d), F32), jax.ShapeDtypeStruct((l, d), F32)],
        compiler_params=_params("parallel"),
        name="hyena_short_conv",
    )(u, u, u, conv_w, conv_w, conv_w, conv_b[None], conv_b[None], conv_b[None])


def _filter_body(z_ref, w_in_ref, b_in_ref, w_mid_ref, b_mid_ref, fq_ref, w_out_ref, dl_ref,
                 k_ref, ss_ref, *, n_inner):
    first = jnp.logical_and(pl.program_id(0) == 0, pl.program_id(1) == 0)

    @pl.when(first)
    def _():
        ss_ref[...] = jnp.zeros_like(ss_ref)

    z = z_ref[...]
    fq = fq_ref[...]
    hdn = jnp.sin(fq * (jnp.dot(z, w_in_ref[...], precision=HIGHEST, preferred_element_type=F32)
                        + b_in_ref[...]))
    for j in range(n_inner):
        hdn = jnp.sin(fq * (jnp.dot(hdn, w_mid_ref[j], precision=HIGHEST, preferred_element_type=F32)
                            + b_mid_ref[j:j + 1]))
    filt = _dot_split(hdn, w_out_ref[...])
    t = z[:, 0:1]
    valid = z[:, LANES - 1:LANES]
    kern = filt * jnp.exp(-t * dl_ref[...]) * valid
    k_ref[...] = kern
    ss_ref[...] += jnp.sum(kern * kern, axis=0, keepdims=True)


def _hyena_kernel(l, d, pe_w_in, pe_b_in, pe_w_mid, pe_b_mid, pe_w_out, sin_freq, *, tl=256):
    bands = (HY_EMB - 1) // 2
    hidden = pe_w_in.shape[1]
    n_inner = pe_w_mid.shape[0]
    pos = jnp.concatenate([jnp.arange(l), l - jnp.arange(l)]).astype(F32)
    valid = jnp.ones((2 * l,), F32).at[l].set(0.0)
    tt = pos / (l - 1)
    fr = jnp.linspace(1e-4, bands - 1, bands, dtype=F32)
    wpos = 2.0 * math.pi * pos[:, None] / l
    z = jnp.concatenate([tt[:, None], jnp.cos(fr * wpos), -jnp.sin(fr * wpos)], axis=-1)
    z = jnp.concatenate([z, jnp.zeros((2 * l, LANES - HY_EMB - 1), F32), valid[:, None]], axis=-1)
    w_in = jnp.zeros((LANES, hidden), F32).at[:HY_EMB].set(pe_w_in)
    max_decay = math.log(HY_DECAY_TARGET) / HY_FAST_DECAY
    min_decay = math.log(HY_DECAY_TARGET) / HY_SLOW_DECAY
    deltas = jnp.abs(jnp.linspace(min_decay, max_decay, d, dtype=F32))[None]
    tl = _pick(l, tl, SUBLANES)
    nt = l // tl
    const = lambda shape: pl.BlockSpec(shape, lambda a, i: tuple(0 for _ in shape))
    return pl.pallas_call(
        functools.partial(_filter_body, n_inner=n_inner),
        grid=(2, nt),
        in_specs=[pl.BlockSpec((tl, LANES), lambda a, i: (a * nt + i, 0)),
                  const((LANES, hidden)), const((1, hidden)),
                  const((n_inner, hidden, hidden)), const((n_inner, hidden)), const((1, hidden)),
                  pl.BlockSpec((hidden, d), lambda a, i: (0, a)),
                  const((1, d))],
        out_specs=[pl.BlockSpec((tl, d), lambda a, i: (a * nt + i, 0)),
                   pl.BlockSpec((1, d), lambda a, i: (0, 0))],
        out_shape=[jax.ShapeDtypeStruct((2 * l, d), F32), jax.ShapeDtypeStruct((1, d), F32)],
        compiler_params=_params("arbitrary", "arbitrary"),
        name="hyena_filter",
    )(z, w_in, pe_b_in[None], pe_w_mid, pe_b_mid, sin_freq[None], pe_w_out, deltas)


def _conv_tables(l):
    n = 2 * l
    f = DFT_F
    s = n // f
    ar = jnp.arange(s, dtype=jnp.int32)
    c, sn = _cis(ar[:, None] * ar[None, :], s)
    fwd = jnp.stack([c, -sn], axis=1)
    p1_full = _page_matrix(fwd.reshape(2 * s, s))
    p1_half = _page_matrix(fwd[:, :, :s // 2].reshape(2 * s, s // 2))
    inv = jnp.stack([c, -sn], axis=-1)[:s // 2] / n
    p2 = _page_matrix(inv.reshape(s // 2, 2 * s))
    cos, sin = _twiddle_grid(s, f, n)
    g1 = jnp.concatenate([jnp.concatenate([cos, sin], axis=2),
                          jnp.concatenate([-sin, cos], axis=2)], axis=1).astype(BF16)
    cos_t, sin_t = jnp.swapaxes(cos, 1, 2), jnp.swapaxes(sin, 1, 2)
    g2 = jnp.concatenate([jnp.concatenate([cos_t, -sin_t], axis=2),
                          jnp.concatenate([sin_t, cos_t], axis=2)], axis=1).astype(BF16)
    return p1_half, p1_full, p2, g1, g2


def _spectral_conv_body(g1_ref, g2_ref, a_ref, ka_ref, o_ref):
    g1 = g1_ref[...]
    x = jnp.dot(g1, a_ref[...].astype(BF16), preferred_element_type=F32)
    kf = jnp.dot(g1, ka_ref[...].astype(BF16), preferred_element_type=F32)
    f = x.shape[0] // 2
    xr, xi, kr, ki = x[:f], x[f:], kf[:f], kf[f:]
    y = jnp.concatenate([xr * kr - xi * ki, xr * ki + xi * kr], axis=0).astype(BF16)
    o_ref[...] = jnp.dot(g2_ref[...], y, preferred_element_type=F32)


def _batched_stage(name, body, mats, arrays, *, dc=2048):
    nb, rows, d = arrays[0].shape
    dc = _pick(d, dc)
    mspecs = [pl.BlockSpec((None,) + m.shape[1:], lambda b, j: (b, 0, 0)) for m in mats]
    aspecs = [pl.BlockSpec((None, rows, dc), lambda b, j: (b, 0, j)) for _ in arrays]
    return pl.pallas_call(
        body,
        grid=(nb, d // dc),
        in_specs=[*mspecs, *aspecs],
        out_specs=pl.BlockSpec((None, mats[-1].shape[1], dc), lambda b, j: (b, 0, j)),
        out_shape=jax.ShapeDtypeStruct((nb, mats[-1].shape[1], d), F32),
        compiler_params=_params("parallel", "parallel"),
        name=name,
    )(*mats, *arrays)


def _conv_out_body(m_ref, b_ref, p_ref, x0_ref, ss_ref, bias_ref, o_ref):
    b = b_ref[...]
    b2 = b.reshape(-1, b.shape[-1]).astype(BF16)
    y = jnp.dot(m_ref[...], b2, preferred_element_type=F32).reshape(p_ref.shape)
    z = y * lax.rsqrt(ss_ref[...] + EPS) + p_ref[...] * bias_ref[...]
    o_ref[...] = (x0_ref[...] * z).astype(o_ref.dtype)


def _hyena(h, w_in, conv_w, conv_b, pe_w_in, pe_b_in, pe_w_mid, pe_b_mid, pe_w_out, sin_freq, bias,
           w_out, x, gate_a):
    l, d = h.shape
    f = DFT_F
    s = 2 * l // f
    tm, tn = _pick(l, 1024), _pick(3 * d, 512)
    u = _matmul("hyena_in", h, [w_in], m=l, n=3 * d, k=d, tm=tm, tn=tn, tk=d,
                out_shape=jax.ShapeDtypeStruct((l, 3 * d), F32),
                out_spec=pl.BlockSpec((tm, tn), lambda i, j, kk: (i, j)))
    x0c, p = _short_conv_gate(u, conv_w, conv_b)
    kern, ss = _hyena_kernel(l, d, pe_w_in, pe_b_in, pe_w_mid, pe_b_mid, pe_w_out, sin_freq)
    p1_half, p1_full, p2, g1, g2 = _conv_tables(l)
    ka = _page_mm(p1_full, kern.reshape(s, 1, f, d), s, 2)
    pa = _page_mm(p1_half, p.reshape(s // 2, 1, f, d), s, 2)
    pb = _batched_stage("hyena_spectral_conv", _spectral_conv_body, [g1, g2],
                        [pa.reshape(s, 2 * f, d), ka.reshape(s, 2 * f, d)])
    dc = _pick(d, 512)
    page = pl.BlockSpec((s // 2, SUBLANES, dc), lambda a, b: (0, a, b))
    chan = pl.BlockSpec((1, dc), lambda a, b: (0, b))
    y = pl.pallas_call(
        _conv_out_body,
        grid=(f // SUBLANES, d // dc),
        in_specs=[pl.BlockSpec(p2.shape, lambda a, b: (0, 0)),
                  pl.BlockSpec((s, 2, SUBLANES, dc), lambda a, b: (0, 0, a, b)),
                  page, page, chan, chan],
        out_specs=page,
        out_shape=jax.ShapeDtypeStruct((s // 2, f, d), BF16),
        compiler_params=_params("parallel", "parallel"),
        name="hyena_conv_out",
    )(p2, pb.reshape(s, 2, f, d), p.reshape(s // 2, f, d), x0c.reshape(s // 2, f, d), ss, bias[None])
    return _proj_residual("hyena_out", y.reshape(l, d), w_out, x, gate_a)


def _pool_body(h_ref, o_ref, pad_ref, *, rows, per_group, halo):
    n_rows = h_ref.shape[0]
    dc = h_ref.shape[1]
    pad_ref[pl.ds(0, halo), :] = jnp.zeros((halo, dc), F32)
    pad_ref[pl.ds(halo + n_rows, halo), :] = jnp.zeros((halo, dc), F32)
    pad_ref[pl.ds(halo, n_rows), :] = h_ref[...]
    group = pl.program_id(0) // per_group

    for g, w in enumerate(POOL_SIZES):
        @pl.when(group == g)
        def _(w=w):
            before, after = w // 2, w - w // 2

            def step(c, carry):
                r0 = pl.multiple_of(c * rows, rows)
                ext = pad_ref[pl.ds(r0, rows + 2 * halo), :]
                tot = ext[halo - before:halo - before + rows]
                for o in range(1 - before, after):
                    tot = tot + ext[halo + o:halo + o + rows]
                t = r0 + lax.broadcasted_iota(jnp.int32, (rows, dc), 0)
                cnt = jnp.minimum(t + after, n_rows) - jnp.maximum(t - before, 0)
                cur = ext[halo:halo + rows]
                o_ref[pl.ds(r0, rows), :] = (tot / cnt.astype(F32) - cur).astype(o_ref.dtype)
                return carry

            lax.fori_loop(0, n_rows // rows, step, 0)


def _pool_features(h, *, dc=256, rows=256):
    l, d = h.shape
    group = d // len(POOL_SIZES)
    dc = _pick(group, dc)
    rows = _pick(l, rows, SUBLANES)
    halo = max(POOL_SIZES) // 2
    return pl.pallas_call(
        functools.partial(_pool_body, rows=rows, per_group=group // dc, halo=halo),
        grid=(d // dc,),
        in_specs=[pl.BlockSpec((l, dc), lambda j: (0, j))],
        out_specs=pl.BlockSpec((l, dc), lambda j: (0, j)),
        out_shape=jax.ShapeDtypeStruct((l, d), BF16),
        scratch_shapes=[pltpu.VMEM((l + 2 * halo, dc), F32)],
        compiler_params=_params("parallel"),
        name="pool_features",
    )(h)


def _pool_mixer(h, w_groups, scale, x, gate_a, *, tm=1024, tn=512):
    l, d = h.shape
    n_g, group, _ = w_groups.shape
    pooled = _pool_features(h)
    tm, tn = _pick(l, tm), _pick(group, tn)
    per = group // tn
    return _matmul(
        "pool_proj", pooled, [w_groups], m=l, n=d, k=group, tm=tm, tn=tn, tk=group,
        out_shape=jax.ShapeDtypeStruct((l, d), F32),
        out_spec=pl.BlockSpec((tm, tn), lambda i, j, kk: (i, j)),
        a_spec=pl.BlockSpec((tm, group), lambda i, j, kk: (i, j // per)),
        b_specs=[pl.BlockSpec((None, group, tn), lambda i, j, kk: (j // per, 0, j % per))],
        extras=[x, gate_a, scale],
        extra_specs=[pl.BlockSpec((tm, tn), lambda i, j, kk: (i, j)),
                     pl.BlockSpec((1, tn), lambda i, j, kk: (0, j)),
                     pl.BlockSpec((1, tn), lambda i, j, kk: (0, j))],
        epilogue=lambda accs, ex: ex[0][...] + ex[1][...] * (accs[0] * ex[2][...]))


def _fourier_tables(l, group):
    f = DFT_F
    s = l // f
    ci = jnp.arange(group, dtype=jnp.int32)[:, None]
    co = jnp.arange(2 * group, dtype=jnp.int32)[None, :]
    thc = _angles(ci * (co % group), group)
    wc = jnp.where(co < group, jnp.cos(thc), -jnp.sin(thc)).astype(BF16)
    ar = jnp.arange(s, dtype=jnp.int32)
    c, sn = _cis(ar[:, None] * ar[None, :], s)
    stage1 = jnp.stack([jnp.stack([c, sn], axis=-1), jnp.stack([-sn, c], axis=-1)], axis=1)
    p1 = _page_matrix(stage1.reshape(2 * s, 2 * s))
    cos, sin = _twiddle_grid(s, f, l)
    g = jnp.concatenate([cos, sin], axis=2).reshape(s // SUBLANES, SUBLANES, f, 2 * f)
    rows = g.transpose(0, 2, 1, 3).reshape(s // SUBLANES, f * SUBLANES, 2 * f)
    row = jnp.arange(f * SUBLANES, dtype=jnp.int32)[None, :, None]
    col = jnp.arange(SUBLANES * 2 * f, dtype=jnp.int32)[None, None, :]
    scat = jnp.where(row % SUBLANES == col // (2 * f), jnp.tile(rows, (1, 1, SUBLANES)), 0.0).astype(BF16)
    return wc, p1, scat


def _fourier_out_body(m_ref, a_ref, o_ref, *, scale):
    y = jnp.dot(m_ref[...], a_ref[...].astype(BF16), preferred_element_type=F32) * scale
    o_ref[...] = y.reshape(o_ref.shape)


def _fourier_mixer(h, w_out, x, gate_a, *, tm=1024, tn=512, dc=1024):
    l, d = h.shape
    group = d // FN_GROUPS
    f = DFT_F
    s = l // f
    wc, p1, scat = _fourier_tables(l, group)
    tm, tn = _pick(l, tm, f), _pick(group, tn)
    per = group // tn
    z = _matmul(
        "fourier_chan", h, [wc], m=l, n=2 * d, k=group, tm=tm, tn=tn, tk=group,
        out_shape=jax.ShapeDtypeStruct((s, 2, f, d), F32),
        out_spec=pl.BlockSpec((tm // f, None, f, tn),
                              lambda i, j, kk: (i, (j // per) % 2, 0, (j // (2 * per)) * per + j % per)),
        a_spec=pl.BlockSpec((tm, group), lambda i, j, kk: (i, j // (2 * per))),
        b_specs=[pl.BlockSpec((group, tn), lambda i, j, kk: (0, j % (2 * per)))])
    a2 = _page_mm(p1, z, s, 2)
    dc = _pick(d, dc)
    q = s // SUBLANES
    mixed = pl.pallas_call(
        functools.partial(_fourier_out_body, scale=1.0 / math.sqrt(l * group)),
        grid=(q, d // dc),
        in_specs=[pl.BlockSpec((None,) + scat.shape[1:], lambda a, b: (a, 0, 0)),
                  pl.BlockSpec((None, SUBLANES * 2 * f, dc), lambda a, b: (a, 0, b))],
        out_specs=pl.BlockSpec((f, None, SUBLANES, dc), lambda a, b: (0, a, 0, b)),
        out_shape=jax.ShapeDtypeStruct((f, q, SUBLANES, d), F32),
        compiler_params=_params("parallel", "parallel"),
        name="fourier_seq_out",
    )(scat, a2.reshape(q, SUBLANES * 2 * f, d))
    return _proj_residual("fourier_out", mixed.reshape(l, d).astype(BF16), w_out, x, gate_a)


def kernel(x, c, ctx, c_ctx, ada_down, ada_up, ada_b, norm_g, final_g, da_w_qkv, da_w_o, da_lambda, da_subln_g, hy_w_in, hy_conv_w, hy_conv_b, hy_pe_w_in, hy_pe_b_in, hy_pe_w_mid, hy_pe_b_mid, hy_pe_w_out, hy_sin_freq, hy_bias, hy_w_out, pool_w, pool_scale, fn_w_out, ffn_w_gate, ffn_w_up, ffn_w_down, moe_router, moe_router_b, moe_w_gate, moe_w_up, moe_w_down):
    batch, l, d = x.shape
    depth = ada_down.shape[0]
    assert batch == 1
    xs = x[0]
    mod = _adaln_all(c, c_ctx, ada_down, ada_up, ada_b)
    ffn_w_down_bf = ffn_w_down.astype(BF16)
    for i in range(depth):
        shift_a, scale_a, gate_a, shift_f, scale_f, gate_f = [
            mod[i, 0:1, n * d:(n + 1) * d] for n in range(N_MOD)]
        g_a, g_f = norm_g[i, 0][None], norm_g[i, 1][None]
        mixer = i % 4
        if mixer == 0:
            h = _norm_mod(xs, g_a, shift_a, scale_a, BF16)
            hc = _norm_mod(ctx[0], g_a, mod[i, 1:2, 0:d], mod[i, 1:2, d:2 * d], BF16)
            xs = _diff_attention(h, hc, da_w_qkv, da_w_o, da_lambda, da_subln_g[None],
                                 0.8 - 0.6 * math.exp(-0.3 * i), xs, gate_a)
        elif mixer == 1:
            h = _norm_mod(xs, g_a, shift_a, scale_a, BF16)
            xs = _hyena(h, hy_w_in, hy_conv_w, hy_conv_b, hy_pe_w_in, hy_pe_b_in, hy_pe_w_mid,
                        hy_pe_b_mid, hy_pe_w_out, hy_sin_freq, hy_bias, hy_w_out, xs, gate_a)
        elif mixer == 2:
            h = _norm_mod(xs, g_a, shift_a, scale_a, F32)
            xs = _pool_mixer(h, pool_w, pool_scale[None], xs, gate_a)
        else:
            h = _norm_mod(xs, g_a, shift_a, scale_a, BF16)
            xs = _fourier_mixer(h, fn_w_out, xs, gate_a)
        if i % 2 == 0:
            h = _norm_mod(xs, g_f, shift_f, scale_f, BF16)
            xs = _swiglu(h, ffn_w_gate, ffn_w_up, ffn_w_down_bf, i // 2, xs, gate_f)
        else:
            j = i // 2
            h, sel = _norm_router(xs, g_f, shift_f, scale_f, moe_router[j], moe_router_b[j])
            last = i == depth - 1
            xs = _moe(h, sel, moe_w_gate, moe_w_up, moe_w_down, j, xs, gate_f, final_g[None], last)
    if depth % 2 == 1:
        xs = _rmsnorm(xs, final_g[None])
    return xs[None]
```

```python
import functools
import math

import jax
import jax.numpy as jnp
from jax import lax
from jax.experimental import pallas as pl
from jax.experimental.pallas import tpu as pltpu

F32 = jnp.float32
BF16 = jnp.bfloat16
HIGHEST = lax.Precision.HIGHEST

EPS = 1e-6
GRID_W = 64
N_MOD = 6
DA_HEADS = 16
DA_HEAD_DIM = 128
ROPE_THETA = 10000.0
HY_EMB = 33
HY_DECAY_TARGET = 1e-2
HY_FAST_DECAY = 0.3
HY_SLOW_DECAY = 1.5
POOL_SIZES = (2, 4, 8, 16)
FN_GROUPS = 4
TOP_K = 2

LANES = 128
SUBLANES = 8
DFT_F = 128
VMEM_LIMIT = 56 * 1024 * 1024


def _params(*sem):
    return pltpu.CompilerParams(dimension_semantics=sem, vmem_limit_bytes=VMEM_LIMIT)


def _pick(dim, pref, align=LANES):
    if dim <= pref:
        return dim
    t = (pref // align) * align
    while t >= align:
        if dim % t == 0:
            return t
        t -= align
    raise ValueError(f"no {align}-aligned tile of {dim} below {pref}")


def _mm_body(*refs, n_b, n_extra, nk, epilogue):
    a_ref = refs[0]
    b_refs = refs[1:1 + n_b]
    extra_refs = refs[1 + n_b:1 + n_b + n_extra]
    o_ref = refs[1 + n_b + n_extra]
    acc_refs = refs[2 + n_b + n_extra:]
    a = a_ref[...].astype(BF16)
    if nk == 1:
        accs = [jnp.dot(a, b[...].astype(BF16), preferred_element_type=F32) for b in b_refs]
        o_ref[...] = epilogue(accs, extra_refs).reshape(o_ref.shape).astype(o_ref.dtype)
        return
    k = pl.program_id(2)

    @pl.when(k == 0)
    def _():
        for acc in acc_refs:
            acc[...] = jnp.zeros_like(acc)

    for acc, b in zip(acc_refs, b_refs):
        acc[...] += jnp.dot(a, b[...].astype(BF16), preferred_element_type=F32)

    @pl.when(k == nk - 1)
    def _():
        accs = [acc[...] for acc in acc_refs]
        o_ref[...] = epilogue(accs, extra_refs).reshape(o_ref.shape).astype(o_ref.dtype)


def _matmul(name, a, bs, *, m, n, k, tm, tn, tk, out_shape, out_spec, a_spec=None, b_specs=None,
            extras=(), extra_specs=(), epilogue=None):
    nk = k // tk
    if epilogue is None:
        epilogue = lambda accs, ex: accs[0]
    if a_spec is None:
        a_spec = pl.BlockSpec((tm, tk), lambda i, j, kk: (i, kk))
    if b_specs is None:
        b_specs = [pl.BlockSpec((tk, tn), lambda i, j, kk: (kk, j)) for _ in bs]
    scratch = [pltpu.VMEM((tm, tn), F32) for _ in bs] if nk > 1 else []
    body = functools.partial(_mm_body, n_b=len(bs), n_extra=len(extras), nk=nk, epilogue=epilogue)
    return pl.pallas_call(
        body,
        grid=(m // tm, n // tn, nk),
        in_specs=[a_spec, *b_specs, *extra_specs],
        out_specs=out_spec,
        out_shape=out_shape,
        scratch_shapes=scratch,
        compiler_params=_params("parallel", "parallel", "arbitrary"),
        name=name,
    )(a, *bs, *extras)


def _residual_epilogue(accs, ex):
    return ex[0][...] + ex[1][...] * accs[0]


def _proj_residual(name, a, w, x, gate, *, tm=1024, tn=512):
    m, k = a.shape
    n = w.shape[1]
    tm, tn = _pick(m, tm), _pick(n, tn)
    return _matmul(
        name, a, [w], m=m, n=n, k=k, tm=tm, tn=tn, tk=k,
        out_shape=jax.ShapeDtypeStruct((m, n), F32),
        out_spec=pl.BlockSpec((tm, tn), lambda i, j, kk: (i, j)),
        extras=[x, gate],
        extra_specs=[pl.BlockSpec((tm, tn), lambda i, j, kk: (i, j)),
                     pl.BlockSpec((1, tn), lambda i, j, kk: (0, j))],
        epilogue=_residual_epilogue)


def _dot_split(a, b):
    a_hi, b_hi = a.astype(BF16), b.astype(BF16)
    a_lo = (a - a_hi.astype(F32)).astype(BF16)
    b_lo = (b - b_hi.astype(F32)).astype(BF16)
    dot = functools.partial(jnp.dot, preferred_element_type=F32)
    return dot(a_hi, b_hi) + (dot(a_hi, b_lo) + dot(a_lo, b_hi))


def _ada_body(a_ref, w_ref, b_ref, o_ref, *, silu):
    a = a_ref[...]
    if silu:
        a = a * jax.nn.sigmoid(a)
    o_ref[...] = _dot_split(a, w_ref[...]) + b_ref[...]


def _adaln_all(c, c_ctx, ada_down, ada_up, ada_b):
    depth, d, r = ada_down.shape
    width = ada_up.shape[2]
    rows = jnp.zeros((SUBLANES, d), F32).at[0].set(c[0]).at[1].set(c_ctx)
    tn1 = _pick(r, 256)
    t = pl.pallas_call(
        functools.partial(_ada_body, silu=True),
        grid=(depth, r // tn1),
        in_specs=[pl.BlockSpec((SUBLANES, d), lambda l, j: (0, 0)),
                  pl.BlockSpec((None, d, tn1), lambda l, j: (l, 0, j)),
                  pl.BlockSpec((1, tn1), lambda l, j: (0, j))],
        out_specs=pl.BlockSpec((None, SUBLANES, tn1), lambda l, j: (l, 0, j)),
        out_shape=jax.ShapeDtypeStruct((depth, SUBLANES, r), F32),
        compiler_params=_params("parallel", "parallel"),
        name="adaln_down",
    )(rows, ada_down, jnp.zeros((1, r), F32))
    tn2 = _pick(width, 2048)
    return pl.pallas_call(
        functools.partial(_ada_body, silu=False),
        grid=(depth, width // tn2),
        in_specs=[pl.BlockSpec((None, SUBLANES, r), lambda l, j: (l, 0, 0)),
                  pl.BlockSpec((None, r, tn2), lambda l, j: (l, 0, j)),
                  pl.BlockSpec((None, 1, tn2), lambda l, j: (l, 0, j))],
        out_specs=pl.BlockSpec((None, SUBLANES, tn2), lambda l, j: (l, 0, j)),
        out_shape=jax.ShapeDtypeStruct((depth, SUBLANES, width), F32),
        compiler_params=_params("parallel", "parallel"),
        name="adaln_up",
    )(t, ada_up, ada_b[:, None, :])


def _norm_rows(x, g):
    return x * lax.rsqrt(jnp.mean(x * x, axis=-1, keepdims=True) + EPS) * g


def _norm_mod_body(x_ref, g_ref, shift_ref, scale_ref, o_ref):
    y = _norm_rows(x_ref[...], g_ref[...])
    o_ref[...] = (y * (1.0 + scale_ref[...]) + shift_ref[...]).astype(o_ref.dtype)


def _norm_mod(x, g, shift, scale, out_dtype, *, tm=256):
    m, d = x.shape
    tm = _pick(m, tm, SUBLANES)
    row = pl.BlockSpec((1, d), lambda i: (0, 0))
    return pl.pallas_call(
        _norm_mod_body,
        grid=(m // tm,),
        in_specs=[pl.BlockSpec((tm, d), lambda i: (i, 0)), row, row, row],
        out_specs=pl.BlockSpec((tm, d), lambda i: (i, 0)),
        out_shape=jax.ShapeDtypeStruct((m, d), out_dtype),
        compiler_params=_params("parallel"),
        name="norm_mod",
    )(x, g, shift, scale)


def _rmsnorm_body(x_ref, g_ref, o_ref):
    o_ref[...] = _norm_rows(x_ref[...], g_ref[...])


def _rmsnorm(x, g, *, tm=256):
    m, d = x.shape
    tm = _pick(m, tm, SUBLANES)
    return pl.pallas_call(
        _rmsnorm_body,
        grid=(m // tm,),
        in_specs=[pl.BlockSpec((tm, d), lambda i: (i, 0)), pl.BlockSpec((1, d), lambda i: (0, 0))],
        out_specs=pl.BlockSpec((tm, d), lambda i: (i, 0)),
        out_shape=jax.ShapeDtypeStruct((m, d), F32),
        compiler_params=_params("parallel"),
        name="final_norm",
    )(x, g)


def _pack_bf16_pairs(x):
    half = x.shape[1] // 2
    bits = pltpu.bitcast(x.astype(BF16).astype(F32), jnp.uint32)
    return (bits[:, half:] & jnp.uint32(0xFFFF0000)) | (bits[:, :half] >> 16)


def _unpack_bf16_pairs(w):
    lo = pltpu.bitcast(w << 16, F32)
    hi = pltpu.bitcast(w & jnp.uint32(0xFFFF0000), F32)
    return jnp.concatenate([lo, hi], axis=1)


def _norm_router_body(x_ref, g_ref, shift_ref, scale_ref, rw_ref, rb_ref, h_ref, sel_ref):
    h = _norm_rows(x_ref[...], g_ref[...]) * (1.0 + scale_ref[...]) + shift_ref[...]
    h_ref[...] = _pack_bf16_pairs(h)
    logits = jnp.dot(h, rw_ref[...], precision=HIGHEST, preferred_element_type=F32) + rb_ref[...]
    lane = lax.broadcasted_iota(jnp.int32, logits.shape, 1)
    m1 = jnp.max(logits, axis=-1, keepdims=True)
    i1 = jnp.min(jnp.where(logits == m1, lane, LANES), axis=-1, keepdims=True)
    rest = jnp.where(lane == i1, -jnp.inf, logits)
    m2 = jnp.max(rest, axis=-1, keepdims=True)
    i2 = jnp.min(jnp.where(rest == m2, lane, LANES), axis=-1, keepdims=True)
    e = jnp.exp(m2 - m1)
    w1 = 1.0 / (1.0 + e)
    w2 = e * w1
    sel_ref[...] = jnp.where(lane == 0, i1.astype(F32),
                             jnp.where(lane == 1, i2.astype(F32),
                                       jnp.where(lane == 2, w1, jnp.where(lane == 3, w2, 0.0))))


def _norm_router(x, g, shift, scale, router, router_b, *, tm=256):
    m, d = x.shape
    n_e = router.shape[1]
    tm = _pick(m, tm, SUBLANES)
    rw = jnp.zeros((d, LANES), F32).at[:, :n_e].set(router)
    rb = jnp.full((1, LANES), -1e30, F32).at[0, :n_e].set(router_b)
    row = pl.BlockSpec((1, d), lambda i: (0, 0))
    return pl.pallas_call(
        _norm_router_body,
        grid=(m // tm,),
        in_specs=[pl.BlockSpec((tm, d), lambda i: (i, 0)), row, row, row,
                  pl.BlockSpec((d, LANES), lambda i: (0, 0)),
                  pl.BlockSpec((1, LANES), lambda i: (0, 0))],
        out_specs=[pl.BlockSpec((tm, d // 2), lambda i: (i, 0)),
                   pl.BlockSpec((tm, LANES), lambda i: (i, 0))],
        out_shape=[jax.ShapeDtypeStruct((m, d // 2), jnp.uint32),
                   jax.ShapeDtypeStruct((m, LANES), F32)],
        compiler_params=_params("parallel"),
        name="norm_router",
    )(x, g, shift, scale, rw, rb)


def _silu(a):
    return a * jax.nn.sigmoid(a)


def _swiglu(h, w_gate, w_up, w_down, layer, x, gate_f):
    m, d = h.shape
    f = w_gate.shape[2]
    tm = _pick(m, 2048)
    tn = _pick(f, 256)
    wspec = pl.BlockSpec((None, d, tn), lambda i, j, kk: (layer, 0, j))
    hid = _matmul(
        "swiglu_up", h, [w_gate, w_up], m=m, n=f, k=d, tm=tm, tn=tn, tk=d,
        out_shape=jax.ShapeDtypeStruct((m, f), BF16),
        out_spec=pl.BlockSpec((tm, tn), lambda i, j, kk: (i, j)),
        a_spec=pl.BlockSpec((tm, d), lambda i, j, kk: (i, 0), pipeline_mode=pl.Buffered(1)),
        b_specs=[wspec, wspec],
        epilogue=lambda accs, ex: _silu(accs[0]) * accs[1])
    tm2 = _pick(m, 512)
    tn2 = _pick(d, 256)
    return _matmul(
        "swiglu_down", hid, [w_down], m=m, n=d, k=f, tm=tm2, tn=tn2, tk=f,
        out_shape=jax.ShapeDtypeStruct((m, d), F32),
        out_spec=pl.BlockSpec((tm2, tn2), lambda i, j, kk: (i, j)),
        a_spec=pl.BlockSpec((tm2, f), lambda i, j, kk: (i, 0)),
        b_specs=[pl.BlockSpec((None, f, tn2), lambda i, j, kk: (layer, 0, j))],
        extras=[x, gate_f],
        extra_specs=[pl.BlockSpec((tm2, tn2), lambda i, j, kk: (i, j)),
                     pl.BlockSpec((1, tn2), lambda i, j, kk: (0, j))],
        epilogue=_residual_epilogue)


MOE_TILE = 512


def _moe_plan(sel, n_e, tile):
    l = sel.shape[0]
    e = jnp.concatenate([sel[:, 0], sel[:, 1]]).astype(jnp.int32)
    onehot = (e[:, None] == jnp.arange(n_e, dtype=jnp.int32)[None, :]).astype(jnp.int32)
    incl = jnp.cumsum(onehot, axis=0)
    counts = incl[-1]
    padded = ((counts + tile - 1) // tile) * tile
    ends = jnp.cumsum(padded)
    starts = ends - padded
    dest = jnp.sum(onehot * (starts[None, :] + incl - 1), axis=1)
    n_rows = 2 * l + n_e * tile
    n_tiles = n_rows // tile
    token = jnp.tile(jnp.arange(l, dtype=jnp.int32), 2)
    tok = jnp.zeros((n_rows,), jnp.int32).at[dest].set(token)
    n_active = ends[-1] // tile
    t_clamped = jnp.minimum(jnp.arange(n_tiles, dtype=jnp.int32), n_active - 1)
    tile_expert = jnp.sum((t_clamped[:, None] * tile >= ends[None, :]).astype(jnp.int32), axis=1)
    return dest, tok, tile_expert, n_active.reshape(1).astype(jnp.int32)


def _prefetched_rows(copies, n_rows, n_steps):
    i = pl.program_id(0)
    slot = i % 2

    def issue(step, into):
        def body(r, carry):
            for cp in copies(step, r, into):
                cp.start()
            return carry
        lax.fori_loop(0, n_rows, body, 0, unroll=8)

    def drain(r, carry):
        for cp in copies(i, r, slot):
            cp.wait()
        return carry

    @pl.when(i == 0)
    def _():
        issue(0, 0)

    @pl.when(i + 1 < n_steps)
    def _():
        issue(i + 1, 1 - slot)

    lax.fori_loop(0, n_rows, drain, 0, unroll=8)
    return slot


def _row_gather_body(idx_ref, src_hbm, o_ref, buf, sem, *, rows, n_steps):
    def copies(step, r, slot):
        return [pltpu.make_async_copy(src_hbm.at[pl.ds(idx_ref[step * rows + r], 1), :],
                                      buf.at[slot, pl.ds(r, 1), :], sem.at[slot])]

    slot = _prefetched_rows(copies, rows, n_steps)
    o_ref[...] = _unpack_bf16_pairs(buf[slot]).astype(o_ref.dtype)


def _row_gather(src, idx, *, rows=256):
    n, d = idx.shape[0], src.shape[1]
    rows = _pick(n, rows, SUBLANES)
    n_steps = n // rows
    return pl.pallas_call(
        functools.partial(_row_gather_body, rows=rows, n_steps=n_steps),
        grid_spec=pltpu.PrefetchScalarGridSpec(
            num_scalar_prefetch=1,
            grid=(n_steps,),
            in_specs=[pl.BlockSpec(memory_space=pl.ANY)],
            out_specs=pl.BlockSpec((rows, 2 * d), lambda i, idx_ref: (i, 0)),
            scratch_shapes=[pltpu.VMEM((2, rows, d), src.dtype), pltpu.SemaphoreType.DMA((2,))]),
        out_shape=jax.ShapeDtypeStruct((n, 2 * d), BF16),
        compiler_params=_params("arbitrary"),
        name="moe_gather",
    )(idx, src)


def _grouped_body(te_ref, na_ref, a_ref, *refs, n_b, n_extra, epilogue):
    b_refs = refs[:n_b]
    extra_refs = refs[n_b:n_b + n_extra]
    o_ref = refs[n_b + n_extra]
    bf_refs = refs[n_b + n_extra + 1:]
    t = pl.program_id(1)
    active = t < na_ref[0]
    fresh = jnp.logical_or(t == 0, te_ref[t] != te_ref[jnp.maximum(t - 1, 0)])

    @pl.when(jnp.logical_and(active, fresh))
    def _():
        for b, bf in zip(b_refs, bf_refs):
            bf[...] = b[...].astype(BF16)

    @pl.when(active)
    def _():
        a = a_ref[...]
        accs = [jnp.dot(a, bf[...], preferred_element_type=F32) for bf in bf_refs]
        o_ref[...] = epilogue(accs, extra_refs).astype(o_ref.dtype)

    @pl.when(jnp.logical_not(active))
    def _():
        o_ref[...] = jnp.zeros_like(o_ref)


def _grouped_matmul(name, a, ws, layer, tile_expert, n_active, *, tile, tn, out_dtype, extras=(), epilogue=None):
    p, k = a.shape
    n = ws[0].shape[3]
    if epilogue is None:
        epilogue = lambda accs, ex: accs[0]
    out_tn = tn // 2 if out_dtype == jnp.uint32 else tn
    row = lambda j, t, te, na: jnp.minimum(t, na[0] - 1)
    return pl.pallas_call(
        functools.partial(_grouped_body, n_b=len(ws), n_extra=len(extras), epilogue=epilogue),
        grid_spec=pltpu.PrefetchScalarGridSpec(
            num_scalar_prefetch=2,
            grid=(n // tn, p // tile),
            in_specs=[pl.BlockSpec((tile, k), lambda j, t, te, na: (row(j, t, te, na), 0)),
                      *[pl.BlockSpec((None, None, k, tn), lambda j, t, te, na: (layer, te[t], 0, j)) for _ in ws],
                      *[pl.BlockSpec((tile, e.shape[1]), lambda j, t, te, na: (row(j, t, te, na), 0))
                        for e in extras]],
            out_specs=pl.BlockSpec((tile, out_tn), lambda j, t, te, na: (t, j)),
            scratch_shapes=[pltpu.VMEM((k, tn), BF16) for _ in ws]),
        out_shape=jax.ShapeDtypeStruct((p, n // tn * out_tn), out_dtype),
        compiler_params=_params("arbitrary", "arbitrary"),
        name=name,
    )(tile_expert, n_active, a, *ws, *extras)


def _combine_body(pos_ref, y_hbm, x_ref, g_ref, sel_ref, ng_ref, o_ref, buf, sem, *,
                  rows, n_tok, n_steps, pack_block, final_norm):
    def copies(step, r, slot):
        return [pltpu.make_async_copy(y_hbm.at[pl.ds(pos_ref[k * n_tok + step * rows + r], 1), :],
                                      buf.at[slot, k, pl.ds(r, 1), :], sem.at[slot]) for k in range(TOP_K)]

    slot = _prefetched_rows(copies, rows, n_steps)

    def expert_rows(k):
        w = buf[slot, k]
        half = pack_block // 2
        return jnp.concatenate([_unpack_bf16_pairs(w[:, b * half:(b + 1) * half])
                                for b in range(w.shape[1] // half)], axis=1)

    sel = sel_ref[...]
    y = sel[:, 2:3] * expert_rows(0) + sel[:, 3:4] * expert_rows(1)
    out = x_ref[...] + g_ref[...] * y
    o_ref[...] = _norm_rows(out, ng_ref[...]) if final_norm else out


def _moe_combine(y, dest, sel, x, gate_f, norm_g, *, pack_block, final_norm, rows=256):
    l, d = x.shape
    rows = _pick(l, rows, SUBLANES)
    n_steps = l // rows
    vec = pl.BlockSpec((1, d), lambda i, pos: (0, 0))
    return pl.pallas_call(
        functools.partial(_combine_body, rows=rows, n_tok=l, n_steps=n_steps, pack_block=pack_block,
                          final_norm=final_norm),
        grid_spec=pltpu.PrefetchScalarGridSpec(
            num_scalar_prefetch=1,
            grid=(n_steps,),
            in_specs=[pl.BlockSpec(memory_space=pl.ANY),
                      pl.BlockSpec((rows, d), lambda i, pos: (i, 0)),
                      vec,
                      pl.BlockSpec((rows, LANES), lambda i, pos: (i, 0)),
                      vec],
            out_specs=pl.BlockSpec((rows, d), lambda i, pos: (i, 0)),
            scratch_shapes=[pltpu.VMEM((2, TOP_K, rows, d // 2), jnp.uint32), pltpu.SemaphoreType.DMA((2,))]),
        out_shape=jax.ShapeDtypeStruct((l, d), F32),
        compiler_params=_params("arbitrary"),
        name="moe_combine",
    )(dest, y, x, gate_f, sel, norm_g)


def _moe(h, sel, w_gate, w_up, w_down, layer, x, gate_f, norm_g, final_norm):
    _, n_e, d, fe = w_gate.shape
    tile = MOE_TILE
    dest, tok, tile_expert, n_active = _moe_plan(sel, n_e, tile)
    hs = _row_gather(h, tok)
    hid = _grouped_matmul(
        "moe_up", hs, [w_gate, w_up], layer, tile_expert, n_active, tile=tile, tn=_pick(fe, 256),
        out_dtype=BF16, epilogue=lambda accs, ex: _silu(accs[0]) * accs[1])
    tn = _pick(d, 1024)
    y = _grouped_matmul("moe_down", hid, [w_down], layer, tile_expert, n_active, tile=tile, tn=tn,
                        out_dtype=jnp.uint32, epilogue=lambda accs, ex: _pack_bf16_pairs(accs[0]))
    return _moe_combine(y, dest, sel, x, gate_f, norm_g, pack_block=tn, final_norm=final_norm)


def _rope_tables(n_tokens):
    pairs = DA_HEAD_DIM // 4
    rows = n_tokens // GRID_W
    row = jnp.repeat(jnp.arange(rows, dtype=F32), GRID_W)
    col = jnp.tile(jnp.arange(GRID_W, dtype=F32), rows)
    inv = ROPE_THETA ** (-jnp.arange(pairs, dtype=F32) / pairs)
    ang = jnp.concatenate([row[:, None] * inv, col[:, None] * inv], axis=-1)
    cos, sin = jnp.cos(ang), jnp.sin(ang)
    return jnp.concatenate([cos, cos], axis=-1), jnp.concatenate([-sin, sin], axis=-1)


def _qkv_rope(h, w_qkv, cos2, sin2, *, tm=1024, tn=512):
    m, d = h.shape
    n = w_qkv.shape[1]
    tm, tn = _pick(m, tm), _pick(d, tn)
    n_q = d // tn
    half = DA_HEAD_DIM // 2

    def epi(accs, ex):
        acc = accs[0]
        j = pl.program_id(1)
        cos, sin = ex[0][...], ex[1][...]
        heads = []
        for t in range(tn // DA_HEAD_DIM):
            xh = acc[:, t * DA_HEAD_DIM:(t + 1) * DA_HEAD_DIM]
            heads.append(xh * cos + pltpu.roll(xh, half, 1) * sin)
        roped = jnp.concatenate(heads, axis=1)
        qscale = jnp.where(j < n_q, DA_HEAD_DIM ** -0.5 * math.log2(math.e), 1.0).astype(F32)
        return jnp.where(j < 2 * n_q, roped * qscale, acc)

    tab = pl.BlockSpec((tm, DA_HEAD_DIM), lambda i, j, kk: (i, 0))
    return _matmul(
        "qkv_rope", h, [w_qkv], m=m, n=n, k=d, tm=tm, tn=tn, tk=d,
        out_shape=jax.ShapeDtypeStruct((m, n), BF16),
        out_spec=pl.BlockSpec((tm, tn), lambda i, j, kk: (i, j)),
        extras=[cos2, sin2], extra_specs=[tab, tab], epilogue=epi)


def _flash_body(lam_ref, g_ref, q_ref, kt_ref, v_ref, o_ref,
                m_sc, l_sc, acc_sc, p0_sc, p1_sc, a0_sc, a1_sc, *, tk, nkv, lambda_init):
    n_val = v_ref.shape[1] // LANES
    n_lane_tiles = tk // LANES
    m_sc[...] = jnp.full_like(m_sc, -jnp.inf)
    l_sc[...] = jnp.zeros_like(l_sc)
    acc_sc[...] = jnp.zeros_like(acc_sc)

    def accumulate(j, p_r, a_r):
        v = v_ref[pl.ds(pl.multiple_of(j * tk, tk), tk), :]
        for c in range(2):
            alpha = jnp.concatenate([a_r[c]] * n_val, axis=1)
            acc_sc[c] = acc_sc[c] * alpha + jnp.dot(p_r[c], v, preferred_element_type=F32)

    def score(j, p_w, a_w):
        q = q_ref[...]
        kt = kt_ref[j]
        for c in range(2):
            qc = q[:, c * DA_HEAD_DIM:(c + 1) * DA_HEAD_DIM]
            kc = kt[c * DA_HEAD_DIM:(c + 1) * DA_HEAD_DIM, :]
            s = jnp.dot(qc, kc, preferred_element_type=F32)
            tiles = [s[:, t * LANES:(t + 1) * LANES] for t in range(n_lane_tiles)]
            m_prev = m_sc[c]
            m_next = jnp.maximum(m_prev, jnp.max(functools.reduce(jnp.maximum, tiles), axis=1, keepdims=True))
            alpha = jnp.exp2(m_prev - m_next)
            ps = [jnp.exp2(t - m_next) for t in tiles]
            l_sc[c] = alpha * l_sc[c] + functools.reduce(jnp.add, ps)
            m_sc[c] = m_next
            a_w[c] = alpha
            p_w[c] = jnp.concatenate(ps, axis=1).astype(BF16)

    even, odd = (p0_sc, a0_sc), (p1_sc, a1_sc)
    score(0, *even)

    def pair(i, carry):
        j = 2 * i + 1
        score(j, *odd)
        accumulate(j - 1, *even)
        score(j + 1, *even)
        accumulate(j, *odd)
        return carry

    lax.fori_loop(0, (nkv - 1) // 2, pair, 0)
    if nkv % 2 == 0:
        score(nkv - 1, *odd)
        accumulate(nkv - 2, *even)
        accumulate(nkv - 1, *odd)
    else:
        accumulate(nkv - 1, *even)

    lam = lam_ref[...]
    lam_full = (jnp.exp(jnp.sum(lam[0:1] * lam[1:2], axis=-1, keepdims=True))
                - jnp.exp(jnp.sum(lam[2:3] * lam[3:4], axis=-1, keepdims=True)) + lambda_init)
    l0 = jnp.sum(l_sc[0], axis=1, keepdims=True)
    l1 = jnp.sum(l_sc[1], axis=1, keepdims=True)
    o = acc_sc[0] / l0 - lam_full * (acc_sc[1] / l1)
    o_ref[...] = (_norm_rows(o, g_ref[...]) * (1.0 - lambda_init)).astype(o_ref.dtype)


def _diff_flash(q, k_all, v_all, lam, subln_g, lambda_init, *, tq=1024, tk=768):
    l = q.shape[0]
    t, d = v_all.shape
    hw = 2 * DA_HEAD_DIM
    n_heads = d // hw
    tq = _pick(l, tq)
    tk = _pick(t, tk)
    nkv = t // tk
    kt = k_all.reshape(nkv, tk, n_heads, hw).transpose(2, 0, 3, 1)
    return pl.pallas_call(
        functools.partial(_flash_body, tk=tk, nkv=nkv, lambda_init=lambda_init),
        grid=(n_heads, l // tq),
        in_specs=[pl.BlockSpec(lam.shape, lambda h, i: (0, 0)),
                  pl.BlockSpec((1, hw), lambda h, i: (0, 0)),
                  pl.BlockSpec((tq, hw), lambda h, i: (i, h)),
                  pl.BlockSpec((None, nkv, hw, tk), lambda h, i: (h, 0, 0, 0)),
                  pl.BlockSpec((t, hw), lambda h, i: (0, h))],
        out_specs=pl.BlockSpec((tq, hw), lambda h, i: (i, h)),
        out_shape=jax.ShapeDtypeStruct((l, d), BF16),
        scratch_shapes=[pltpu.VMEM((2, tq, LANES), F32), pltpu.VMEM((2, tq, LANES), F32),
                        pltpu.VMEM((2, tq, hw), F32),
                        pltpu.VMEM((2, tq, tk), BF16), pltpu.VMEM((2, tq, tk), BF16),
                        pltpu.VMEM((2, tq, LANES), F32), pltpu.VMEM((2, tq, LANES), F32)],
        compiler_params=_params("parallel", "parallel"),
        name="diff_flash",
    )(lam, subln_g, q, kt, v_all)


def _diff_attention(h, hc, w_qkv, w_o, lam, subln_g, lambda_init, x, gate_a):
    l, d = h.shape
    cos2, sin2 = _rope_tables(l)
    qkv = _qkv_rope(h, w_qkv, cos2, sin2)
    c_len = hc.shape[0]
    tmc = _pick(c_len, 256)
    kvc = _matmul(
        "ctx_kv", hc, [w_qkv], m=c_len, n=2 * d, k=d, tm=tmc, tn=512, tk=d,
        out_shape=jax.ShapeDtypeStruct((c_len, 2 * d), BF16),
        out_spec=pl.BlockSpec((tmc, 512), lambda i, j, kk: (i, j)),
        b_specs=[pl.BlockSpec((d, 512), lambda i, j, kk: (0, j + d // 512))])
    k_all = jnp.concatenate([qkv[:, d:2 * d], kvc[:, :d]], axis=0)
    v_all = jnp.concatenate([qkv[:, 2 * d:], kvc[:, d:]], axis=0)
    o = _diff_flash(qkv, k_all, v_all, lam, subln_g, lambda_init)
    return _proj_residual("attn_out", o, w_o, x, gate_a)


def _angles(num, den):
    return (2.0 * math.pi / den) * (num % den).astype(F32)


def _cis(num, den):
    th = _angles(num, den)
    return jnp.cos(th), jnp.sin(th)


def _twiddle_grid(s, f, n):
    ar = lambda m: jnp.arange(m, dtype=jnp.int32)
    ac, asn = _cis(ar(s)[:, None] * ar(f)[None, :], n)
    bc, bsn = _cis(ar(f)[:, None] * ar(f)[None, :], f)
    cos = ac[:, None, :] * bc[None, :, :] - asn[:, None, :] * bsn[None, :, :]
    sin = asn[:, None, :] * bc[None, :, :] + ac[:, None, :] * bsn[None, :, :]
    return cos, sin


def _page_matrix(base):
    nr, nc = base.shape
    r = jnp.arange(nr * SUBLANES, dtype=jnp.int32)[:, None]
    c = jnp.arange(nc * SUBLANES, dtype=jnp.int32)[None, :]
    expand_r = (r // SUBLANES == jnp.arange(nr, dtype=jnp.int32)[None, :]).astype(F32)
    expand_c = (jnp.arange(nc, dtype=jnp.int32)[:, None] == c // SUBLANES).astype(F32)
    full = jnp.dot(jnp.dot(expand_r, base), expand_c)
    return jnp.where(r % SUBLANES == c % SUBLANES, full, 0.0).astype(BF16)


PAGE = 2 * SUBLANES


def _page_contract(mat, x):
    dc = x.shape[-1]
    x3 = x.astype(F32).reshape(-1, PAGE, dc)
    halves = []
    for t in range(PAGE // SUBLANES):
        xt = x3[:, t * SUBLANES:(t + 1) * SUBLANES, :].reshape(-1, dc).astype(BF16)
        halves.append(jnp.dot(mat, xt, preferred_element_type=F32).reshape(-1, SUBLANES, dc))
    return jnp.concatenate(halves, axis=1)


def _page_mm_body(m_ref, x_ref, o_ref):
    o_ref[...] = _page_contract(m_ref[...], x_ref[...]).reshape(o_ref.shape).astype(o_ref.dtype)


def _page_mm(mat, x4, ko, po, *, dc=512):
    s, pi, f, d = x4.shape
    dc = _pick(d, dc)
    return pl.pallas_call(
        _page_mm_body,
        grid=(f // PAGE, d // dc),
        in_specs=[pl.BlockSpec(mat.shape, lambda a, b: (0, 0)),
                  pl.BlockSpec((s, pi, PAGE, dc), lambda a, b: (0, 0, a, b))],
        out_specs=pl.BlockSpec((ko, po, PAGE, dc), lambda a, b: (0, 0, a, b)),
        out_shape=jax.ShapeDtypeStruct((ko, po, f, d), BF16),
        compiler_params=_params("parallel", "parallel"),
        name="dft_lead",
    )(mat, x4)


def _short_conv_body(u0_ref, u1_ref, u2_ref, w0_ref, w1_ref, w2_ref, b0_ref, b1_ref, b2_ref,
                     x0_ref, p_ref, *, rows):
    n_rows = u0_ref.shape[0]
    n_chunks = n_rows // rows

    def conv(u_ref, w_ref, b_ref, r0, c):
        cur = u_ref[pl.ds(r0, rows), :]
        prev = u_ref[pl.ds(jnp.maximum(r0 - 1, 0), 1), :] * jnp.where(c > 0, 1.0, 0.0)
        nxt = u_ref[pl.ds(jnp.minimum(r0 + rows, n_rows - 1), 1), :] * jnp.where(c < n_chunks - 1, 1.0, 0.0)
        ridx = lax.broadcasted_iota(jnp.int32, cur.shape, 0)
        up = jnp.where(ridx == 0, prev, pltpu.roll(cur, 1, 0))
        down = jnp.where(ridx == rows - 1, nxt, pltpu.roll(cur, rows - 1, 0))
        w = w_ref[...]
        return w[0:1] * up + w[1:2] * cur + w[2:3] * down + b_ref[...]

    def step(c, carry):
        r0 = pl.multiple_of(c * rows, rows)
        x0_ref[pl.ds(r0, rows), :] = conv(u0_ref, w0_ref, b0_ref, r0, c)
        p_ref[pl.ds(r0, rows), :] = conv(u2_ref, w2_ref, b2_ref, r0, c) * conv(u1_ref, w1_ref, b1_ref, r0, c)
        return carry

    lax.fori_loop(0, n_chunks, step, 0)


def _short_conv_gate(u, conv_w, conv_b, *, dc=128, rows=512):
    l, d3 = u.shape
    d = d3 // 3
    nb = d // dc
    rows = _pick(l, rows, SUBLANES)
    us = [pl.BlockSpec((l, dc), lambda j, s=s: (0, j + s * nb)) for s in range(3)]
    ws = [pl.BlockSpec((3, dc), lambda j, s=s: (0, j + s * nb)) for s in range(3)]
    bs = [pl.BlockSpec((1, dc), lambda j, s=s: (0, j + s * nb)) for s in range(3)]
    out = pl.BlockSpec((l, dc), lambda j: (0, j))
    return pl.pallas_call(
        functools.partial(_short_conv_body, rows=rows),
        grid=(nb,),
        in_specs=[*us, *ws, *bs],
        out_specs=[out, out],
        out_shape=[jax.ShapeDtypeStruct((l, d), F32), jax.ShapeDtypeStruct((l, d), F32)],
        compiler_params=_params("parallel"),
        name="hyena_short_conv",
    )(u, u, u, conv_w, conv_w, conv_w, conv_b[None], conv_b[None], conv_b[None])


def _filter_body(z_ref, w_in_ref, b_in_ref, w_mid_ref, b_mid_ref, fq_ref, w_out_ref, dl_ref,
                 k_ref, ss_ref, *, n_inner):
    first = jnp.logical_and(pl.program_id(0) == 0, pl.program_id(1) == 0)

    @pl.when(first)
    def _():
        ss_ref[...] = jnp.zeros_like(ss_ref)

    z = z_ref[...]
    fq = fq_ref[...]
    hdn = jnp.sin(fq * (jnp.dot(z, w_in_ref[...], precision=HIGHEST, preferred_element_type=F32)
                        + b_in_ref[...]))
    for j in range(n_inner):
        hdn = jnp.sin(fq * (jnp.dot(hdn, w_mid_ref[j], precision=HIGHEST, preferred_element_type=F32)
                            + b_mid_ref[j:j + 1]))
    filt = _dot_split(hdn, w_out_ref[...])
    t = z[:, 0:1]
    valid = z[:, LANES - 1:LANES]
    kern = filt * jnp.exp(-t * dl_ref[...]) * valid
    k_ref[...] = kern
    ss_ref[...] += jnp.sum(kern * kern, axis=0, keepdims=True)


def _hyena_kernel(l, d, pe_w_in, pe_b_in, pe_w_mid, pe_b_mid, pe_w_out, sin_freq, *, tl=256):
    bands = (HY_EMB - 1) // 2
    hidden = pe_w_in.shape[1]
    n_inner = pe_w_mid.shape[0]
    pos = jnp.concatenate([jnp.arange(l), l - jnp.arange(l)]).astype(F32)
    valid = jnp.ones((2 * l,), F32).at[l].set(0.0)
    tt = pos / (l - 1)
    fr = jnp.linspace(1e-4, bands - 1, bands, dtype=F32)
    wpos = 2.0 * math.pi * pos[:, None] / l
    z = jnp.concatenate([tt[:, None], jnp.cos(fr * wpos), -jnp.sin(fr * wpos)], axis=-1)
    z = jnp.concatenate([z, jnp.zeros((2 * l, LANES - HY_EMB - 1), F32), valid[:, None]], axis=-1)
    w_in = jnp.zeros((LANES, hidden), F32).at[:HY_EMB].set(pe_w_in)
    max_decay = math.log(HY_DECAY_TARGET) / HY_FAST_DECAY
    min_decay = math.log(HY_DECAY_TARGET) / HY_SLOW_DECAY
    deltas = jnp.abs(jnp.linspace(min_decay, max_decay, d, dtype=F32))[None]
    tl = _pick(l, tl, SUBLANES)
    nt = l // tl
    const = lambda shape: pl.BlockSpec(shape, lambda a, i: tuple(0 for _ in shape))
    return pl.pallas_call(
        functools.partial(_filter_body, n_inner=n_inner),
        grid=(2, nt),
        in_specs=[pl.BlockSpec((tl, LANES), lambda a, i: (a * nt + i, 0)),
                  const((LANES, hidden)), const((1, hidden)),
                  const((n_inner, hidden, hidden)), const((n_inner, hidden)), const((1, hidden)),
                  pl.BlockSpec((hidden, d), lambda a, i: (0, a)),
                  const((1, d))],
        out_specs=[pl.BlockSpec((tl, d), lambda a, i: (a * nt + i, 0)),
                   pl.BlockSpec((1, d), lambda a, i: (0, 0))],
        out_shape=[jax.ShapeDtypeStruct((2 * l, d), F32), jax.ShapeDtypeStruct((1, d), F32)],
        compiler_params=_params("arbitrary", "arbitrary"),
        name="hyena_filter",
    )(z, w_in, pe_b_in[None], pe_w_mid, pe_b_mid, sin_freq[None], pe_w_out, deltas)


def _conv_tables(l):
    n = 2 * l
    f = DFT_F
    s = n // f
    ar = jnp.arange(s, dtype=jnp.int32)
    c, sn = _cis(ar[:, None] * ar[None, :], s)
    fwd = jnp.stack([c, -sn], axis=1)
    p1_full = _page_matrix(fwd.reshape(2 * s, s))
    p1_half = _page_matrix(fwd[:, :, :s // 2].reshape(2 * s, s // 2))
    inv = jnp.stack([c, -sn], axis=-1)[:s // 2] / n
    p2 = _page_matrix(inv.reshape(s // 2, 2 * s))
    cos, sin = _twiddle_grid(s, f, n)
    g1 = jnp.concatenate([jnp.concatenate([cos, sin], axis=2),
                          jnp.concatenate([-sin, cos], axis=2)], axis=1).astype(BF16)
    cos_t, sin_t = jnp.swapaxes(cos, 1, 2), jnp.swapaxes(sin, 1, 2)
    g2 = jnp.concatenate([jnp.concatenate([cos_t, -sin_t], axis=2),
                          jnp.concatenate([sin_t, cos_t], axis=2)], axis=1).astype(BF16)
    return p1_half, p1_full, p2, g1, g2


def _spectral_conv_body(g1_ref, g2_ref, a_ref, ka_ref, o_ref):
    g1 = g1_ref[...]
    x = jnp.dot(g1, a_ref[...].astype(BF16), preferred_element_type=F32)
    kf = jnp.dot(g1, ka_ref[...].astype(BF16), preferred_element_type=F32)
    f = x.shape[0] // 2
    xr, xi, kr, ki = x[:f], x[f:], kf[:f], kf[f:]
    y = jnp.concatenate([xr * kr - xi * ki, xr * ki + xi * kr], axis=0).astype(BF16)
    o_ref[...] = jnp.dot(g2_ref[...], y, preferred_element_type=F32).astype(o_ref.dtype)


def _batched_stage(name, body, mats, arrays, *, dc=2048):
    nb, rows, d = arrays[0].shape
    dc = _pick(d, dc)
    mspecs = [pl.BlockSpec((None,) + m.shape[1:], lambda b, j: (b, 0, 0)) for m in mats]
    aspecs = [pl.BlockSpec((None, rows, dc), lambda b, j: (b, 0, j)) for _ in arrays]
    return pl.pallas_call(
        body,
        grid=(nb, d // dc),
        in_specs=[*mspecs, *aspecs],
        out_specs=pl.BlockSpec((None, mats[-1].shape[1], dc), lambda b, j: (b, 0, j)),
        out_shape=jax.ShapeDtypeStruct((nb, mats[-1].shape[1], d), BF16),
        compiler_params=_params("parallel", "parallel"),
        name=name,
    )(*mats, *arrays)


def _conv_out_body(m_ref, b_ref, p_ref, x0_ref, ss_ref, bias_ref, o_ref):
    y = _page_contract(m_ref[...], b_ref[...])
    z = y * lax.rsqrt(ss_ref[...] + EPS) + p_ref[...] * bias_ref[...]
    o_ref[...] = (x0_ref[...] * z).astype(o_ref.dtype)


def _hyena(h, w_in, conv_w, conv_b, pe_w_in, pe_b_in, pe_w_mid, pe_b_mid, pe_w_out, sin_freq, bias,
           w_out, x, gate_a):
    l, d = h.shape
    f = DFT_F
    s = 2 * l // f
    tm, tn = _pick(l, 1024), _pick(3 * d, 512)
    u = _matmul("hyena_in", h, [w_in], m=l, n=3 * d, k=d, tm=tm, tn=tn, tk=d,
                out_shape=jax.ShapeDtypeStruct((l, 3 * d), F32),
                out_spec=pl.BlockSpec((tm, tn), lambda i, j, kk: (i, j)))
    x0c, p = _short_conv_gate(u, conv_w, conv_b)
    kern, ss = _hyena_kernel(l, d, pe_w_in, pe_b_in, pe_w_mid, pe_b_mid, pe_w_out, sin_freq)
    p1_half, p1_full, p2, g1, g2 = _conv_tables(l)
    ka = _page_mm(p1_full, kern.reshape(s, 1, f, d), s, 2)
    pa = _page_mm(p1_half, p.reshape(s // 2, 1, f, d), s, 2)
    pb = _batched_stage("hyena_spectral_conv", _spectral_conv_body, [g1, g2],
                        [pa.reshape(s, 2 * f, d), ka.reshape(s, 2 * f, d)])
    dc = _pick(d, 512)
    page = pl.BlockSpec((s // 2, PAGE, dc), lambda a, b: (0, a, b))
    chan = pl.BlockSpec((1, dc), lambda a, b: (0, b))
    y = pl.pallas_call(
        _conv_out_body,
        grid=(f // PAGE, d // dc),
        in_specs=[pl.BlockSpec(p2.shape, lambda a, b: (0, 0)),
                  pl.BlockSpec((s, 2, PAGE, dc), lambda a, b: (0, 0, a, b)),
                  page, page, chan, chan],
        out_specs=page,
        out_shape=jax.ShapeDtypeStruct((s // 2, f, d), BF16),
        compiler_params=_params("parallel", "parallel"),
        name="hyena_conv_out",
    )(p2, pb.reshape(s, 2, f, d), p.reshape(s // 2, f, d), x0c.reshape(s // 2, f, d), ss, bias[None])
    return _proj_residual("hyena_out", y.reshape(l, d), w_out, x, gate_a)


def _pool_body(h_ref, o_ref, pad_ref, *, rows, per_group, halo):
    n_rows = h_ref.shape[0]
    dc = h_ref.shape[1]
    pad_ref[pl.ds(0, halo), :] = jnp.zeros((halo, dc), F32)
    pad_ref[pl.ds(halo + n_rows, halo), :] = jnp.zeros((halo, dc), F32)
    pad_ref[pl.ds(halo, n_rows), :] = h_ref[...]
    group = pl.program_id(0) // per_group

    for g, w in enumerate(POOL_SIZES):
        @pl.when(group == g)
        def _(w=w):
            before, after = w // 2, w - w // 2

            def step(c, carry):
                r0 = pl.multiple_of(c * rows, rows)
                ext = pad_ref[pl.ds(r0, rows + 2 * halo), :]
                tot = ext[halo - before:halo - before + rows]
                for o in range(1 - before, after):
                    tot = tot + ext[halo + o:halo + o + rows]
                t = r0 + lax.broadcasted_iota(jnp.int32, (rows, dc), 0)
                cnt = jnp.minimum(t + after, n_rows) - jnp.maximum(t - before, 0)
                cur = ext[halo:halo + rows]
                o_ref[pl.ds(r0, rows), :] = (tot / cnt.astype(F32) - cur).astype(o_ref.dtype)
                return carry

            lax.fori_loop(0, n_rows // rows, step, 0)


def _pool_features(h, *, dc=256, rows=256):
    l, d = h.shape
    group = d // len(POOL_SIZES)
    dc = _pick(group, dc)
    rows = _pick(l, rows, SUBLANES)
    halo = max(POOL_SIZES) // 2
    return pl.pallas_call(
        functools.partial(_pool_body, rows=rows, per_group=group // dc, halo=halo),
        grid=(d // dc,),
        in_specs=[pl.BlockSpec((l, dc), lambda j: (0, j))],
        out_specs=pl.BlockSpec((l, dc), lambda j: (0, j)),
        out_shape=jax.ShapeDtypeStruct((l, d), BF16),
        scratch_shapes=[pltpu.VMEM((l + 2 * halo, dc), F32)],
        compiler_params=_params("parallel"),
        name="pool_features",
    )(h)


def _pool_mixer(h, w_groups, scale, x, gate_a, *, tm=1024, tn=512):
    l, d = h.shape
    n_g, group, _ = w_groups.shape
    pooled = _pool_features(h)
    tm, tn = _pick(l, tm), _pick(group, tn)
    per = group // tn
    return _matmul(
        "pool_proj", pooled, [w_groups], m=l, n=d, k=group, tm=tm, tn=tn, tk=group,
        out_shape=jax.ShapeDtypeStruct((l, d), F32),
        out_spec=pl.BlockSpec((tm, tn), lambda i, j, kk: (i, j)),
        a_spec=pl.BlockSpec((tm, group), lambda i, j, kk: (i, j // per)),
        b_specs=[pl.BlockSpec((None, group, tn), lambda i, j, kk: (j // per, 0, j % per))],
        extras=[x, gate_a, scale],
        extra_specs=[pl.BlockSpec((tm, tn), lambda i, j, kk: (i, j)),
                     pl.BlockSpec((1, tn), lambda i, j, kk: (0, j)),
                     pl.BlockSpec((1, tn), lambda i, j, kk: (0, j))],
        epilogue=lambda accs, ex: ex[0][...] + ex[1][...] * (accs[0] * ex[2][...]))


def _fourier_tables(l, group):
    f = DFT_F
    s = l // f
    ci = jnp.arange(group, dtype=jnp.int32)[:, None]
    co = jnp.arange(2 * group, dtype=jnp.int32)[None, :]
    thc = _angles(ci * (co % group), group)
    wc = jnp.where(co < group, jnp.cos(thc), -jnp.sin(thc)).astype(BF16)
    ar = jnp.arange(s, dtype=jnp.int32)
    c, sn = _cis(ar[:, None] * ar[None, :], s)
    stage1 = jnp.stack([jnp.stack([c, sn], axis=-1), jnp.stack([-sn, c], axis=-1)], axis=1)
    p1 = _page_matrix(stage1.reshape(2 * s, 2 * s))
    cos, sin = _twiddle_grid(s, f, l)
    g = jnp.concatenate([cos, sin], axis=2).reshape(s // SUBLANES, SUBLANES, f, 2 * f)
    rows = g.transpose(0, 2, 1, 3).reshape(s // SUBLANES, f * SUBLANES, 2 * f)
    row = jnp.arange(f * SUBLANES, dtype=jnp.int32)[None, :, None]
    col = jnp.arange(SUBLANES * 2 * f, dtype=jnp.int32)[None, None, :]
    scat = jnp.where(row % SUBLANES == col // (2 * f), jnp.tile(rows, (1, 1, SUBLANES)), 0.0).astype(BF16)
    return wc, p1, scat


def _fourier_out_body(m_ref, a_ref, o_ref, *, scale):
    y = jnp.dot(m_ref[...], a_ref[...].astype(BF16), preferred_element_type=F32) * scale
    o_ref[...] = y.reshape(o_ref.shape)


def _fourier_mixer(h, w_out, x, gate_a, *, tm=1024, tn=512, dc=1024):
    l, d = h.shape
    group = d // FN_GROUPS
    f = DFT_F
    s = l // f
    wc, p1, scat = _fourier_tables(l, group)
    tm, tn = _pick(l, tm, f), _pick(group, tn)
    per = group // tn
    z = _matmul(
        "fourier_chan", h, [wc], m=l, n=2 * d, k=group, tm=tm, tn=tn, tk=group,
        out_shape=jax.ShapeDtypeStruct((s, 2, f, d), BF16),
        out_spec=pl.BlockSpec((tm // f, None, f, tn),
                              lambda i, j, kk: (i, (j // per) % 2, 0, (j // (2 * per)) * per + j % per)),
        a_spec=pl.BlockSpec((tm, group), lambda i, j, kk: (i, j // (2 * per))),
        b_specs=[pl.BlockSpec((group, tn), lambda i, j, kk: (0, j % (2 * per)))])
    a2 = _page_mm(p1, z, s, 2)
    dc = _pick(d, dc)
    q = s // SUBLANES
    mixed = pl.pallas_call(
        functools.partial(_fourier_out_body, scale=1.0 / math.sqrt(l * group)),
        grid=(q, d // dc),
        in_specs=[pl.BlockSpec((None,) + scat.shape[1:], lambda a, b: (a, 0, 0)),
                  pl.BlockSpec((None, SUBLANES * 2 * f, dc), lambda a, b: (a, 0, b))],
        out_specs=pl.BlockSpec((f, None, SUBLANES, dc), lambda a, b: (0, a, 0, b)),
        out_shape=jax.ShapeDtypeStruct((f, q, SUBLANES, d), F32),
        compiler_params=_params("parallel", "parallel"),
        name="fourier_seq_out",
    )(scat, a2.reshape(q, SUBLANES * 2 * f, d))
    return _proj_residual("fourier_out", mixed.reshape(l, d).astype(BF16), w_out, x, gate_a)


def kernel(x, c, ctx, c_ctx, ada_down, ada_up, ada_b, norm_g, final_g, da_w_qkv, da_w_o, da_lambda, da_subln_g, hy_w_in, hy_conv_w, hy_conv_b, hy_pe_w_in, hy_pe_b_in, hy_pe_w_mid, hy_pe_b_mid, hy_pe_w_out, hy_sin_freq, hy_bias, hy_w_out, pool_w, pool_scale, fn_w_out, ffn_w_gate, ffn_w_up, ffn_w_down, moe_router, moe_router_b, moe_w_gate, moe_w_up, moe_w_down):
    batch, l, d = x.shape
    depth = ada_down.shape[0]
    assert batch == 1
    xs = x[0]
    mod = _adaln_all(c, c_ctx, ada_down, ada_up, ada_b)
    ffn_w_down_bf = ffn_w_down.astype(BF16)
    for i in range(depth):
        shift_a, scale_a, gate_a, shift_f, scale_f, gate_f = [
            mod[i, 0:1, n * d:(n + 1) * d] for n in range(N_MOD)]
        g_a, g_f = norm_g[i, 0][None], norm_g[i, 1][None]
        mixer = i % 4
        if mixer == 0:
            h = _norm_mod(xs, g_a, shift_a, scale_a, BF16)
            hc = _norm_mod(ctx[0], g_a, mod[i, 1:2, 0:d], mod[i, 1:2, d:2 * d], BF16)
            xs = _diff_attention(h, hc, da_w_qkv, da_w_o, da_lambda, da_subln_g[None],
                                 0.8 - 0.6 * math.exp(-0.3 * i), xs, gate_a)
        elif mixer == 1:
            h = _norm_mod(xs, g_a, shift_a, scale_a, BF16)
            xs = _hyena(h, hy_w_in, hy_conv_w, hy_conv_b, hy_pe_w_in, hy_pe_b_in, hy_pe_w_mid,
                        hy_pe_b_mid, hy_pe_w_out, hy_sin_freq, hy_bias, hy_w_out, xs, gate_a)
        elif mixer == 2:
            h = _norm_mod(xs, g_a, shift_a, scale_a, F32)
            xs = _pool_mixer(h, pool_w, pool_scale[None], xs, gate_a)
        else:
            h = _norm_mod(xs, g_a, shift_a, scale_a, BF16)
            xs = _fourier_mixer(h, fn_w_out, xs, gate_a)
        if i % 2 == 0:
            h = _norm_mod(xs, g_f, shift_f, scale_f, BF16)
            xs = _swiglu(h, ffn_w_gate, ffn_w_up, ffn_w_down_bf, i // 2, xs, gate_f)
        else:
            j = i // 2
            h, sel = _norm_router(xs, g_f, shift_f, scale_f, moe_router[j], moe_router_b[j])
            last = i == depth - 1
            xs = _moe(h, sel, moe_w_gate, moe_w_up, moe_w_down, j, xs, gate_f, final_g[None], last)
    if depth % 2 == 1:
        xs = _rmsnorm(xs, final_g[None])
    return xs[None]
```

```python
import functools
import math

import jax
import jax.numpy as jnp
from jax import lax
from jax.experimental import pallas as pl
from jax.experimental.pallas import tpu as pltpu

F32 = jnp.float32
BF16 = jnp.bfloat16
HIGHEST = lax.Precision.HIGHEST

EPS = 1e-6
GRID_W = 64
N_MOD = 6
DA_HEADS = 16
DA_HEAD_DIM = 128
ROPE_THETA = 10000.0
HY_EMB = 33
HY_DECAY_TARGET = 1e-2
HY_FAST_DECAY = 0.3
HY_SLOW_DECAY = 1.5
POOL_SIZES = (2, 4, 8, 16)
FN_GROUPS = 4
TOP_K = 2

LANES = 128
SUBLANES = 8
DFT_F = 128
VMEM_LIMIT = 56 * 1024 * 1024


def _params(*sem):
    return pltpu.CompilerParams(dimension_semantics=sem, vmem_limit_bytes=VMEM_LIMIT)


def _pick(dim, pref, align=LANES):
    if dim <= pref:
        return dim
    t = (pref // align) * align
    while t >= align:
        if dim % t == 0:
            return t
        t -= align
    raise ValueError(f"no {align}-aligned tile of {dim} below {pref}")


def _mm_body(*refs, n_b, n_extra, nk, epilogue):
    a_ref = refs[0]
    b_refs = refs[1:1 + n_b]
    extra_refs = refs[1 + n_b:1 + n_b + n_extra]
    o_ref = refs[1 + n_b + n_extra]
    acc_refs = refs[2 + n_b + n_extra:]
    a = a_ref[...].astype(BF16)
    if nk == 1:
        accs = [jnp.dot(a, b[...].astype(BF16), preferred_element_type=F32) for b in b_refs]
        o_ref[...] = epilogue(accs, extra_refs).reshape(o_ref.shape).astype(o_ref.dtype)
        return
    k = pl.program_id(2)

    @pl.when(k == 0)
    def _():
        for acc in acc_refs:
            acc[...] = jnp.zeros_like(acc)

    for acc, b in zip(acc_refs, b_refs):
        acc[...] += jnp.dot(a, b[...].astype(BF16), preferred_element_type=F32)

    @pl.when(k == nk - 1)
    def _():
        accs = [acc[...] for acc in acc_refs]
        o_ref[...] = epilogue(accs, extra_refs).reshape(o_ref.shape).astype(o_ref.dtype)


def _matmul(name, a, bs, *, m, n, k, tm, tn, tk, out_shape, out_spec, a_spec=None, b_specs=None,
            extras=(), extra_specs=(), epilogue=None):
    nk = k // tk
    if epilogue is None:
        epilogue = lambda accs, ex: accs[0]
    if a_spec is None:
        a_spec = pl.BlockSpec((tm, tk), lambda i, j, kk: (i, kk))
    if b_specs is None:
        b_specs = [pl.BlockSpec((tk, tn), lambda i, j, kk: (kk, j)) for _ in bs]
    scratch = [pltpu.VMEM((tm, tn), F32) for _ in bs] if nk > 1 else []
    body = functools.partial(_mm_body, n_b=len(bs), n_extra=len(extras), nk=nk, epilogue=epilogue)
    return pl.pallas_call(
        body,
        grid=(m // tm, n // tn, nk),
        in_specs=[a_spec, *b_specs, *extra_specs],
        out_specs=out_spec,
        out_shape=out_shape,
        scratch_shapes=scratch,
        compiler_params=_params("parallel", "parallel", "arbitrary"),
        name=name,
    )(a, *bs, *extras)


def _residual_epilogue(accs, ex):
    return ex[0][...] + ex[1][...] * accs[0]


def _proj_residual(name, a, w, x, gate, *, tm=1024, tn=512):
    m, k = a.shape
    n = w.shape[1]
    tm, tn = _pick(m, tm), _pick(n, tn)
    return _matmul(
        name, a, [w], m=m, n=n, k=k, tm=tm, tn=tn, tk=k,
        out_shape=jax.ShapeDtypeStruct((m, n), F32),
        out_spec=pl.BlockSpec((tm, tn), lambda i, j, kk: (i, j)),
        extras=[x, gate],
        extra_specs=[pl.BlockSpec((tm, tn), lambda i, j, kk: (i, j)),
                     pl.BlockSpec((1, tn), lambda i, j, kk: (0, j))],
        epilogue=_residual_epilogue)


def _dot_split(a, b):
    a_hi, b_hi = a.astype(BF16), b.astype(BF16)
    a_lo = (a - a_hi.astype(F32)).astype(BF16)
    b_lo = (b - b_hi.astype(F32)).astype(BF16)
    dot = functools.partial(jnp.dot, preferred_element_type=F32)
    return dot(a_hi, b_hi) + (dot(a_hi, b_lo) + dot(a_lo, b_hi))


def _ada_body(a_ref, w_ref, b_ref, o_ref, *, silu):
    a = a_ref[...]
    if silu:
        a = a * jax.nn.sigmoid(a)
    o_ref[...] = _dot_split(a, w_ref[...]) + b_ref[...]


def _adaln_all(c, c_ctx, ada_down, ada_up, ada_b):
    depth, d, r = ada_down.shape
    width = ada_up.shape[2]
    rows = jnp.zeros((SUBLANES, d), F32).at[0].set(c[0]).at[1].set(c_ctx)
    tn1 = _pick(r, 256)
    t = pl.pallas_call(
        functools.partial(_ada_body, silu=True),
        grid=(depth, r // tn1),
        in_specs=[pl.BlockSpec((SUBLANES, d), lambda l, j: (0, 0)),
                  pl.BlockSpec((None, d, tn1), lambda l, j: (l, 0, j)),
                  pl.BlockSpec((1, tn1), lambda l, j: (0, j))],
        out_specs=pl.BlockSpec((None, SUBLANES, tn1), lambda l, j: (l, 0, j)),
        out_shape=jax.ShapeDtypeStruct((depth, SUBLANES, r), F32),
        compiler_params=_params("parallel", "parallel"),
        name="adaln_down",
    )(rows, ada_down, jnp.zeros((1, r), F32))
    tn2 = _pick(width, 2048)
    return pl.pallas_call(
        functools.partial(_ada_body, silu=False),
        grid=(depth, width // tn2),
        in_specs=[pl.BlockSpec((None, SUBLANES, r), lambda l, j: (l, 0, 0)),
                  pl.BlockSpec((None, r, tn2), lambda l, j: (l, 0, j)),
                  pl.BlockSpec((None, 1, tn2), lambda l, j: (l, 0, j))],
        out_specs=pl.BlockSpec((None, SUBLANES, tn2), lambda l, j: (l, 0, j)),
        out_shape=jax.ShapeDtypeStruct((depth, SUBLANES, width), F32),
        compiler_params=_params("parallel", "parallel"),
        name="adaln_up",
    )(t, ada_up, ada_b[:, None, :])


def _norm_rows(x, g):
    return x * lax.rsqrt(jnp.mean(x * x, axis=-1, keepdims=True) + EPS) * g


def _norm_mod_body(x_ref, g_ref, shift_ref, scale_ref, o_ref):
    y = _norm_rows(x_ref[...], g_ref[...])
    o_ref[...] = (y * (1.0 + scale_ref[...]) + shift_ref[...]).astype(o_ref.dtype)


def _norm_mod(x, g, shift, scale, out_dtype, *, tm=256):
    m, d = x.shape
    tm = _pick(m, tm, SUBLANES)
    row = pl.BlockSpec((1, d), lambda i: (0, 0))
    return pl.pallas_call(
        _norm_mod_body,
        grid=(m // tm,),
        in_specs=[pl.BlockSpec((tm, d), lambda i: (i, 0)), row, row, row],
        out_specs=pl.BlockSpec((tm, d), lambda i: (i, 0)),
        out_shape=jax.ShapeDtypeStruct((m, d), out_dtype),
        compiler_params=_params("parallel"),
        name="norm_mod",
    )(x, g, shift, scale)


def _rmsnorm_body(x_ref, g_ref, o_ref):
    o_ref[...] = _norm_rows(x_ref[...], g_ref[...])


def _rmsnorm(x, g, *, tm=256):
    m, d = x.shape
    tm = _pick(m, tm, SUBLANES)
    return pl.pallas_call(
        _rmsnorm_body,
        grid=(m // tm,),
        in_specs=[pl.BlockSpec((tm, d), lambda i: (i, 0)), pl.BlockSpec((1, d), lambda i: (0, 0))],
        out_specs=pl.BlockSpec((tm, d), lambda i: (i, 0)),
        out_shape=jax.ShapeDtypeStruct((m, d), F32),
        compiler_params=_params("parallel"),
        name="final_norm",
    )(x, g)


def _pack_bf16_pairs(x):
    half = x.shape[1] // 2
    bits = pltpu.bitcast(x.astype(BF16).astype(F32), jnp.uint32)
    return (bits[:, half:] & jnp.uint32(0xFFFF0000)) | (bits[:, :half] >> 16)


def _unpack_bf16_pairs(w):
    lo = pltpu.bitcast(w << 16, F32)
    hi = pltpu.bitcast(w & jnp.uint32(0xFFFF0000), F32)
    return jnp.concatenate([lo, hi], axis=1)


def _norm_router_body(x_ref, g_ref, shift_ref, scale_ref, rw_ref, rb_ref, h_ref, sel_ref):
    h = _norm_rows(x_ref[...], g_ref[...]) * (1.0 + scale_ref[...]) + shift_ref[...]
    h_ref[...] = _pack_bf16_pairs(h)
    logits = jnp.dot(h, rw_ref[...], precision=HIGHEST, preferred_element_type=F32) + rb_ref[...]
    lane = lax.broadcasted_iota(jnp.int32, logits.shape, 1)
    m1 = jnp.max(logits, axis=-1, keepdims=True)
    i1 = jnp.min(jnp.where(logits == m1, lane, LANES), axis=-1, keepdims=True)
    rest = jnp.where(lane == i1, -jnp.inf, logits)
    m2 = jnp.max(rest, axis=-1, keepdims=True)
    i2 = jnp.min(jnp.where(rest == m2, lane, LANES), axis=-1, keepdims=True)
    e = jnp.exp(m2 - m1)
    w1 = 1.0 / (1.0 + e)
    w2 = e * w1
    sel_ref[...] = jnp.where(lane == 0, i1.astype(F32),
                             jnp.where(lane == 1, i2.astype(F32),
                                       jnp.where(lane == 2, w1, jnp.where(lane == 3, w2, 0.0))))


def _norm_router(x, g, shift, scale, router, router_b, *, tm=256):
    m, d = x.shape
    n_e = router.shape[1]
    tm = _pick(m, tm, SUBLANES)
    rw = jnp.zeros((d, LANES), F32).at[:, :n_e].set(router)
    rb = jnp.full((1, LANES), -1e30, F32).at[0, :n_e].set(router_b)
    row = pl.BlockSpec((1, d), lambda i: (0, 0))
    return pl.pallas_call(
        _norm_router_body,
        grid=(m // tm,),
        in_specs=[pl.BlockSpec((tm, d), lambda i: (i, 0)), row, row, row,
                  pl.BlockSpec((d, LANES), lambda i: (0, 0)),
                  pl.BlockSpec((1, LANES), lambda i: (0, 0))],
        out_specs=[pl.BlockSpec((tm, d // 2), lambda i: (i, 0)),
                   pl.BlockSpec((tm, LANES), lambda i: (i, 0))],
        out_shape=[jax.ShapeDtypeStruct((m, d // 2), jnp.uint32),
                   jax.ShapeDtypeStruct((m, LANES), F32)],
        compiler_params=_params("parallel"),
        name="norm_router",
    )(x, g, shift, scale, rw, rb)


def _silu(a):
    return a * jax.nn.sigmoid(a)


def _swiglu(h, w_gate, w_up, w_down, layer, x, gate_f):
    m, d = h.shape
    f = w_gate.shape[2]
    tm = _pick(m, 2048)
    tn = _pick(f, 256)
    wspec = pl.BlockSpec((None, d, tn), lambda i, j, kk: (layer, 0, j))
    hid = _matmul(
        "swiglu_up", h, [w_gate, w_up], m=m, n=f, k=d, tm=tm, tn=tn, tk=d,
        out_shape=jax.ShapeDtypeStruct((m, f), BF16),
        out_spec=pl.BlockSpec((tm, tn), lambda i, j, kk: (i, j)),
        a_spec=pl.BlockSpec((tm, d), lambda i, j, kk: (i, 0), pipeline_mode=pl.Buffered(1)),
        b_specs=[wspec, wspec],
        epilogue=lambda accs, ex: _silu(accs[0]) * accs[1])
    tm2 = _pick(m, 1024)
    tn2 = _pick(d, 256)
    return _matmul(
        "swiglu_down", hid, [w_down], m=m, n=d, k=f, tm=tm2, tn=tn2, tk=f,
        out_shape=jax.ShapeDtypeStruct((m, d), F32),
        out_spec=pl.BlockSpec((tm2, tn2), lambda i, j, kk: (i, j)),
        a_spec=pl.BlockSpec((tm2, f), lambda i, j, kk: (i, 0), pipeline_mode=pl.Buffered(1)),
        b_specs=[pl.BlockSpec((None, f, tn2), lambda i, j, kk: (layer, 0, j))],
        extras=[x, gate_f],
        extra_specs=[pl.BlockSpec((tm2, tn2), lambda i, j, kk: (i, j)),
                     pl.BlockSpec((1, tn2), lambda i, j, kk: (0, j))],
        epilogue=_residual_epilogue)


MOE_TILE = 512


def _moe_plan(sel, n_e, tile):
    l = sel.shape[0]
    e = jnp.concatenate([sel[:, 0], sel[:, 1]]).astype(jnp.int32)
    onehot = (e[:, None] == jnp.arange(n_e, dtype=jnp.int32)[None, :]).astype(jnp.int32)
    incl = jnp.cumsum(onehot, axis=0)
    counts = incl[-1]
    padded = ((counts + tile - 1) // tile) * tile
    ends = jnp.cumsum(padded)
    starts = ends - padded
    dest = jnp.sum(onehot * (starts[None, :] + incl - 1), axis=1)
    n_rows = 2 * l + n_e * tile
    n_tiles = n_rows // tile
    token = jnp.tile(jnp.arange(l, dtype=jnp.int32), 2)
    tok = jnp.zeros((n_rows,), jnp.int32).at[dest].set(token)
    n_active = ends[-1] // tile
    t_clamped = jnp.minimum(jnp.arange(n_tiles, dtype=jnp.int32), n_active - 1)
    tile_expert = jnp.sum((t_clamped[:, None] * tile >= ends[None, :]).astype(jnp.int32), axis=1)
    return dest, tok, tile_expert, n_active.reshape(1).astype(jnp.int32)


def _prefetched_rows(copies, n_rows, n_steps):
    i = pl.program_id(0)
    slot = i % 2

    def issue(step, into):
        def body(r, carry):
            for cp in copies(step, r, into):
                cp.start()
            return carry
        lax.fori_loop(0, n_rows, body, 0, unroll=8)

    def drain(r, carry):
        for cp in copies(i, r, slot):
            cp.wait()
        return carry

    @pl.when(i == 0)
    def _():
        issue(0, 0)

    @pl.when(i + 1 < n_steps)
    def _():
        issue(i + 1, 1 - slot)

    lax.fori_loop(0, n_rows, drain, 0, unroll=8)
    return slot


def _row_gather_body(idx_ref, src_hbm, o_ref, buf, sem, *, rows, n_steps):
    def copies(step, r, slot):
        return [pltpu.make_async_copy(src_hbm.at[pl.ds(idx_ref[step * rows + r], 1), :],
                                      buf.at[slot, pl.ds(r, 1), :], sem.at[slot])]

    slot = _prefetched_rows(copies, rows, n_steps)
    o_ref[...] = _unpack_bf16_pairs(buf[slot]).astype(o_ref.dtype)


def _row_gather(src, idx, *, rows=256):
    n, d = idx.shape[0], src.shape[1]
    rows = _pick(n, rows, SUBLANES)
    n_steps = n // rows
    return pl.pallas_call(
        functools.partial(_row_gather_body, rows=rows, n_steps=n_steps),
        grid_spec=pltpu.PrefetchScalarGridSpec(
            num_scalar_prefetch=1,
            grid=(n_steps,),
            in_specs=[pl.BlockSpec(memory_space=pl.ANY)],
            out_specs=pl.BlockSpec((rows, 2 * d), lambda i, idx_ref: (i, 0)),
            scratch_shapes=[pltpu.VMEM((2, rows, d), src.dtype), pltpu.SemaphoreType.DMA((2,))]),
        out_shape=jax.ShapeDtypeStruct((n, 2 * d), BF16),
        compiler_params=_params("arbitrary"),
        name="moe_gather",
    )(idx, src)


def _grouped_body(te_ref, na_ref, a_ref, *refs, n_b, n_extra, epilogue):
    b_refs = refs[:n_b]
    extra_refs = refs[n_b:n_b + n_extra]
    o_ref = refs[n_b + n_extra]
    bf_refs = refs[n_b + n_extra + 1:]
    t = pl.program_id(1)
    active = t < na_ref[0]
    fresh = jnp.logical_or(t == 0, te_ref[t] != te_ref[jnp.maximum(t - 1, 0)])

    @pl.when(jnp.logical_and(active, fresh))
    def _():
        for b, bf in zip(b_refs, bf_refs):
            bf[...] = b[...].astype(BF16)

    @pl.when(active)
    def _():
        a = a_ref[...]
        accs = [jnp.dot(a, bf[...], preferred_element_type=F32) for bf in bf_refs]
        o_ref[...] = epilogue(accs, extra_refs).astype(o_ref.dtype)

    @pl.when(jnp.logical_not(active))
    def _():
        o_ref[...] = jnp.zeros_like(o_ref)


def _grouped_matmul(name, a, ws, layer, tile_expert, n_active, *, tile, tn, out_dtype, extras=(), epilogue=None):
    p, k = a.shape
    n = ws[0].shape[3]
    if epilogue is None:
        epilogue = lambda accs, ex: accs[0]
    out_tn = tn // 2 if out_dtype == jnp.uint32 else tn
    row = lambda j, t, te, na: jnp.minimum(t, na[0] - 1)
    return pl.pallas_call(
        functools.partial(_grouped_body, n_b=len(ws), n_extra=len(extras), epilogue=epilogue),
        grid_spec=pltpu.PrefetchScalarGridSpec(
            num_scalar_prefetch=2,
            grid=(n // tn, p // tile),
            in_specs=[pl.BlockSpec((tile, k), lambda j, t, te, na: (row(j, t, te, na), 0)),
                      *[pl.BlockSpec((None, None, k, tn), lambda j, t, te, na: (layer, te[t], 0, j)) for _ in ws],
                      *[pl.BlockSpec((tile, e.shape[1]), lambda j, t, te, na: (row(j, t, te, na), 0))
                        for e in extras]],
            out_specs=pl.BlockSpec((tile, out_tn), lambda j, t, te, na: (t, j)),
            scratch_shapes=[pltpu.VMEM((k, tn), BF16) for _ in ws]),
        out_shape=jax.ShapeDtypeStruct((p, n // tn * out_tn), out_dtype),
        compiler_params=_params("arbitrary", "arbitrary"),
        name=name,
    )(tile_expert, n_active, a, *ws, *extras)


def _combine_body(pos_ref, y_hbm, x_ref, g_ref, sel_ref, ng_ref, o_ref, buf, sem, *,
                  rows, n_tok, n_steps, pack_block, final_norm):
    def copies(step, r, slot):
        return [pltpu.make_async_copy(y_hbm.at[pl.ds(pos_ref[k * n_tok + step * rows + r], 1), :],
                                      buf.at[slot, k, pl.ds(r, 1), :], sem.at[slot]) for k in range(TOP_K)]

    slot = _prefetched_rows(copies, rows, n_steps)

    def expert_rows(k):
        w = buf[slot, k]
        half = pack_block // 2
        return jnp.concatenate([_unpack_bf16_pairs(w[:, b * half:(b + 1) * half])
                                for b in range(w.shape[1] // half)], axis=1)

    sel = sel_ref[...]
    y = sel[:, 2:3] * expert_rows(0) + sel[:, 3:4] * expert_rows(1)
    out = x_ref[...] + g_ref[...] * y
    o_ref[...] = _norm_rows(out, ng_ref[...]) if final_norm else out


def _moe_combine(y, dest, sel, x, gate_f, norm_g, *, pack_block, final_norm, rows=256):
    l, d = x.shape
    rows = _pick(l, rows, SUBLANES)
    n_steps = l // rows
    vec = pl.BlockSpec((1, d), lambda i, pos: (0, 0))
    return pl.pallas_call(
        functools.partial(_combine_body, rows=rows, n_tok=l, n_steps=n_steps, pack_block=pack_block,
                          final_norm=final_norm),
        grid_spec=pltpu.PrefetchScalarGridSpec(
            num_scalar_prefetch=1,
            grid=(n_steps,),
            in_specs=[pl.BlockSpec(memory_space=pl.ANY),
                      pl.BlockSpec((rows, d), lambda i, pos: (i, 0)),
                      vec,
                      pl.BlockSpec((rows, LANES), lambda i, pos: (i, 0)),
                      vec],
            out_specs=pl.BlockSpec((rows, d), lambda i, pos: (i, 0)),
            scratch_shapes=[pltpu.VMEM((2, TOP_K, rows, d // 2), jnp.uint32), pltpu.SemaphoreType.DMA((2,))]),
        out_shape=jax.ShapeDtypeStruct((l, d), F32),
        compiler_params=_params("arbitrary"),
        name="moe_combine",
    )(dest, y, x, gate_f, sel, norm_g)


def _moe(h, sel, w_gate, w_up, w_down, layer, x, gate_f, norm_g, final_norm):
    _, n_e, d, fe = w_gate.shape
    tile = MOE_TILE
    dest, tok, tile_expert, n_active = _moe_plan(sel, n_e, tile)
    hs = _row_gather(h, tok)
    hid = _grouped_matmul(
        "moe_up", hs, [w_gate, w_up], layer, tile_expert, n_active, tile=tile, tn=_pick(fe, 256),
        out_dtype=BF16, epilogue=lambda accs, ex: _silu(accs[0]) * accs[1])
    tn = _pick(d, 2048)
    y = _grouped_matmul("moe_down", hid, [w_down], layer, tile_expert, n_active, tile=tile, tn=tn,
                        out_dtype=jnp.uint32, epilogue=lambda accs, ex: _pack_bf16_pairs(accs[0]))
    return _moe_combine(y, dest, sel, x, gate_f, norm_g, pack_block=tn, final_norm=final_norm)


def _rope_tables(n_tokens):
    pairs = DA_HEAD_DIM // 4
    rows = n_tokens // GRID_W
    row = jnp.repeat(jnp.arange(rows, dtype=F32), GRID_W)
    col = jnp.tile(jnp.arange(GRID_W, dtype=F32), rows)
    inv = ROPE_THETA ** (-jnp.arange(pairs, dtype=F32) / pairs)
    ang = jnp.concatenate([row[:, None] * inv, col[:, None] * inv], axis=-1)
    cos, sin = jnp.cos(ang), jnp.sin(ang)
    return jnp.concatenate([cos, cos], axis=-1), jnp.concatenate([-sin, sin], axis=-1)


def _qkv_rope(h, w_qkv, cos2, sin2, *, tm=1024, tn=512):
    m, d = h.shape
    n = w_qkv.shape[1]
    tm, tn = _pick(m, tm), _pick(d, tn)
    n_q = d // tn
    half = DA_HEAD_DIM // 2

    def epi(accs, ex):
        acc = accs[0]
        j = pl.program_id(1)
        cos, sin = ex[0][...], ex[1][...]
        heads = []
        for t in range(tn // DA_HEAD_DIM):
            xh = acc[:, t * DA_HEAD_DIM:(t + 1) * DA_HEAD_DIM]
            heads.append(xh * cos + pltpu.roll(xh, half, 1) * sin)
        roped = jnp.concatenate(heads, axis=1)
        qscale = jnp.where(j < n_q, DA_HEAD_DIM ** -0.5 * math.log2(math.e), 1.0).astype(F32)
        return jnp.where(j < 2 * n_q, roped * qscale, acc)

    tab = pl.BlockSpec((tm, DA_HEAD_DIM), lambda i, j, kk: (i, 0))
    return _matmul(
        "qkv_rope", h, [w_qkv], m=m, n=n, k=d, tm=tm, tn=tn, tk=d,
        out_shape=jax.ShapeDtypeStruct((m, n), BF16),
        out_spec=pl.BlockSpec((tm, tn), lambda i, j, kk: (i, j)),
        extras=[cos2, sin2], extra_specs=[tab, tab], epilogue=epi)


def _flash_body(lam_ref, g_ref, q_ref, kt_ref, v_ref, o_ref,
                m_sc, l_sc, acc_sc, p0_sc, p1_sc, a0_sc, a1_sc, *, tk, nkv, lambda_init):
    n_val = v_ref.shape[1] // LANES
    n_lane_tiles = tk // LANES
    m_sc[...] = jnp.full_like(m_sc, -jnp.inf)
    l_sc[...] = jnp.zeros_like(l_sc)
    acc_sc[...] = jnp.zeros_like(acc_sc)

    def accumulate(j, p_r, a_r):
        v = v_ref[pl.ds(pl.multiple_of(j * tk, tk), tk), :]
        for c in range(2):
            alpha = jnp.concatenate([a_r[c]] * n_val, axis=1)
            acc_sc[c] = acc_sc[c] * alpha + jnp.dot(p_r[c], v, preferred_element_type=F32)

    def score(j, p_w, a_w):
        q = q_ref[...]
        kt = kt_ref[j]
        for c in range(2):
            qc = q[:, c * DA_HEAD_DIM:(c + 1) * DA_HEAD_DIM]
            kc = kt[c * DA_HEAD_DIM:(c + 1) * DA_HEAD_DIM, :]
            s = jnp.dot(qc, kc, preferred_element_type=F32)
            tiles = [s[:, t * LANES:(t + 1) * LANES] for t in range(n_lane_tiles)]
            m_prev = m_sc[c]
            m_next = jnp.maximum(m_prev, jnp.max(functools.reduce(jnp.maximum, tiles), axis=1, keepdims=True))
            alpha = jnp.exp2(m_prev - m_next)
            ps = [jnp.exp2(t - m_next) for t in tiles]
            l_sc[c] = alpha * l_sc[c] + functools.reduce(jnp.add, ps)
            m_sc[c] = m_next
            a_w[c] = alpha
            p_w[c] = jnp.concatenate(ps, axis=1).astype(BF16)

    even, odd = (p0_sc, a0_sc), (p1_sc, a1_sc)
    score(0, *even)

    def pair(i, carry):
        j = 2 * i + 1
        score(j, *odd)
        accumulate(j - 1, *even)
        score(j + 1, *even)
        accumulate(j, *odd)
        return carry

    lax.fori_loop(0, (nkv - 1) // 2, pair, 0)
    if nkv % 2 == 0:
        score(nkv - 1, *odd)
        accumulate(nkv - 2, *even)
        accumulate(nkv - 1, *odd)
    else:
        accumulate(nkv - 1, *even)

    lam = lam_ref[...]
    lam_full = (jnp.exp(jnp.sum(lam[0:1] * lam[1:2], axis=-1, keepdims=True))
                - jnp.exp(jnp.sum(lam[2:3] * lam[3:4], axis=-1, keepdims=True)) + lambda_init)
    l0 = jnp.sum(l_sc[0], axis=1, keepdims=True)
    l1 = jnp.sum(l_sc[1], axis=1, keepdims=True)
    o = acc_sc[0] / l0 - lam_full * (acc_sc[1] / l1)
    o_ref[...] = (_norm_rows(o, g_ref[...]) * (1.0 - lambda_init)).astype(o_ref.dtype)


def _diff_flash(q, k_all, v_all, lam, subln_g, lambda_init, *, tq=1024, tk=768):
    l = q.shape[0]
    t, d = v_all.shape
    hw = 2 * DA_HEAD_DIM
    n_heads = d // hw
    tq = _pick(l, tq)
    tk = _pick(t, tk)
    nkv = t // tk
    kt = k_all.reshape(nkv, tk, n_heads, hw).transpose(2, 0, 3, 1)
    return pl.pallas_call(
        functools.partial(_flash_body, tk=tk, nkv=nkv, lambda_init=lambda_init),
        grid=(n_heads, l // tq),
        in_specs=[pl.BlockSpec(lam.shape, lambda h, i: (0, 0)),
                  pl.BlockSpec((1, hw), lambda h, i: (0, 0)),
                  pl.BlockSpec((tq, hw), lambda h, i: (i, h)),
                  pl.BlockSpec((None, nkv, hw, tk), lambda h, i: (h, 0, 0, 0)),
                  pl.BlockSpec((t, hw), lambda h, i: (0, h))],
        out_specs=pl.BlockSpec((tq, hw), lambda h, i: (i, h)),
        out_shape=jax.ShapeDtypeStruct((l, d), BF16),
        scratch_shapes=[pltpu.VMEM((2, tq, LANES), F32), pltpu.VMEM((2, tq, LANES), F32),
                        pltpu.VMEM((2, tq, hw), F32),
                        pltpu.VMEM((2, tq, tk), BF16), pltpu.VMEM((2, tq, tk), BF16),
                        pltpu.VMEM((2, tq, LANES), F32), pltpu.VMEM((2, tq, LANES), F32)],
        compiler_params=_params("parallel", "parallel"),
        name="diff_flash",
    )(lam, subln_g, q, kt, v_all)


def _diff_attention(h, hc, w_qkv, w_o, lam, subln_g, lambda_init, x, gate_a):
    l, d = h.shape
    cos2, sin2 = _rope_tables(l)
    qkv = _qkv_rope(h, w_qkv, cos2, sin2)
    c_len = hc.shape[0]
    tmc = _pick(c_len, 256)
    kvc = _matmul(
        "ctx_kv", hc, [w_qkv], m=c_len, n=2 * d, k=d, tm=tmc, tn=512, tk=d,
        out_shape=jax.ShapeDtypeStruct((c_len, 2 * d), BF16),
        out_spec=pl.BlockSpec((tmc, 512), lambda i, j, kk: (i, j)),
        b_specs=[pl.BlockSpec((d, 512), lambda i, j, kk: (0, j + d // 512))])
    k_all = jnp.concatenate([qkv[:, d:2 * d], kvc[:, :d]], axis=0)
    v_all = jnp.concatenate([qkv[:, 2 * d:], kvc[:, d:]], axis=0)
    o = _diff_flash(qkv, k_all, v_all, lam, subln_g, lambda_init)
    return _proj_residual("attn_out", o, w_o, x, gate_a)


def _angles(num, den):
    return (2.0 * math.pi / den) * (num % den).astype(F32)


def _cis(num, den):
    th = _angles(num, den)
    return jnp.cos(th), jnp.sin(th)


def _twiddle_grid(s, f, n):
    ar = lambda m: jnp.arange(m, dtype=jnp.int32)
    ac, asn = _cis(ar(s)[:, None] * ar(f)[None, :], n)
    bc, bsn = _cis(ar(f)[:, None] * ar(f)[None, :], f)
    cos = ac[:, None, :] * bc[None, :, :] - asn[:, None, :] * bsn[None, :, :]
    sin = asn[:, None, :] * bc[None, :, :] + ac[:, None, :] * bsn[None, :, :]
    return cos, sin


def _page_matrix(base):
    nr, nc = base.shape
    r = jnp.arange(nr * SUBLANES, dtype=jnp.int32)[:, None]
    c = jnp.arange(nc * SUBLANES, dtype=jnp.int32)[None, :]
    expand_r = (r // SUBLANES == jnp.arange(nr, dtype=jnp.int32)[None, :]).astype(F32)
    expand_c = (jnp.arange(nc, dtype=jnp.int32)[:, None] == c // SUBLANES).astype(F32)
    full = jnp.dot(jnp.dot(expand_r, base), expand_c)
    return jnp.where(r % SUBLANES == c % SUBLANES, full, 0.0).astype(BF16)


PAGE = 2 * SUBLANES


def _page_contract(mat, x):
    dc = x.shape[-1]
    x3 = x.astype(F32).reshape(-1, PAGE, dc)
    halves = []
    for t in range(PAGE // SUBLANES):
        xt = x3[:, t * SUBLANES:(t + 1) * SUBLANES, :].reshape(-1, dc).astype(BF16)
        halves.append(jnp.dot(mat, xt, preferred_element_type=F32).reshape(-1, SUBLANES, dc))
    return jnp.concatenate(halves, axis=1)


def _page_mm_body(m_ref, x_ref, o_ref):
    o_ref[...] = _page_contract(m_ref[...], x_ref[...]).reshape(o_ref.shape).astype(o_ref.dtype)


def _page_mm(mat, x4, ko, po, *, dc=512):
    s, pi, f, d = x4.shape
    dc = _pick(d, dc)
    return pl.pallas_call(
        _page_mm_body,
        grid=(f // PAGE, d // dc),
        in_specs=[pl.BlockSpec(mat.shape, lambda a, b: (0, 0)),
                  pl.BlockSpec((s, pi, PAGE, dc), lambda a, b: (0, 0, a, b))],
        out_specs=pl.BlockSpec((ko, po, PAGE, dc), lambda a, b: (0, 0, a, b)),
        out_shape=jax.ShapeDtypeStruct((ko, po, f, d), BF16),
        compiler_params=_params("parallel", "parallel"),
        name="dft_lead",
    )(mat, x4)


def _short_conv_body(u0_ref, u1_ref, u2_ref, w0_ref, w1_ref, w2_ref, b0_ref, b1_ref, b2_ref,
                     x0_ref, p_ref, *, rows):
    n_rows = u0_ref.shape[0]
    n_chunks = n_rows // rows

    def conv(u_ref, w_ref, b_ref, r0, c):
        cur = u_ref[pl.ds(r0, rows), :]
        prev = u_ref[pl.ds(jnp.maximum(r0 - 1, 0), 1), :] * jnp.where(c > 0, 1.0, 0.0)
        nxt = u_ref[pl.ds(jnp.minimum(r0 + rows, n_rows - 1), 1), :] * jnp.where(c < n_chunks - 1, 1.0, 0.0)
        ridx = lax.broadcasted_iota(jnp.int32, cur.shape, 0)
        up = jnp.where(ridx == 0, prev, pltpu.roll(cur, 1, 0))
        down = jnp.where(ridx == rows - 1, nxt, pltpu.roll(cur, rows - 1, 0))
        w = w_ref[...]
        return w[0:1] * up + w[1:2] * cur + w[2:3] * down + b_ref[...]

    def step(c, carry):
        r0 = pl.multiple_of(c * rows, rows)
        x0_ref[pl.ds(r0, rows), :] = conv(u0_ref, w0_ref, b0_ref, r0, c)
        p_ref[pl.ds(r0, rows), :] = conv(u2_ref, w2_ref, b2_ref, r0, c) * conv(u1_ref, w1_ref, b1_ref, r0, c)
        return carry

    lax.fori_loop(0, n_chunks, step, 0)


def _short_conv_gate(u, conv_w, conv_b, *, dc=128, rows=512):
    l, d3 = u.shape
    d = d3 // 3
    nb = d // dc
    rows = _pick(l, rows, SUBLANES)
    us = [pl.BlockSpec((l, dc), lambda j, s=s: (0, j + s * nb)) for s in range(3)]
    ws = [pl.BlockSpec((3, dc), lambda j, s=s: (0, j + s * nb)) for s in range(3)]
    bs = [pl.BlockSpec((1, dc), lambda j, s=s: (0, j + s * nb)) for s in range(3)]
    out = pl.BlockSpec((l, dc), lambda j: (0, j))
    return pl.pallas_call(
        functools.partial(_short_conv_body, rows=rows),
        grid=(nb,),
        in_specs=[*us, *ws, *bs],
        out_specs=[out, out],
        out_shape=[jax.ShapeDtypeStruct((l, d), F32), jax.ShapeDtypeStruct((l, d), F32)],
        compiler_params=_params("parallel"),
        name="hyena_short_conv",
    )(u, u, u, conv_w, conv_w, conv_w, conv_b[None], conv_b[None], conv_b[None])


def _filter_body(z_ref, w_in_ref, b_in_ref, w_mid_ref, b_mid_ref, fq_ref, w_out_ref, dl_ref,
                 k_ref, ss_ref, *, n_inner):
    first = jnp.logical_and(pl.program_id(0) == 0, pl.program_id(1) == 0)

    @pl.when(first)
    def _():
        ss_ref[...] = jnp.zeros_like(ss_ref)

    z = z_ref[...]
    fq = fq_ref[...]
    hdn = jnp.sin(fq * (jnp.dot(z, w_in_ref[...], precision=HIGHEST, preferred_element_type=F32)
                        + b_in_ref[...]))
    for j in range(n_inner):
        hdn = jnp.sin(fq * (jnp.dot(hdn, w_mid_ref[j], precision=HIGHEST, preferred_element_type=F32)
                            + b_mid_ref[j:j + 1]))
    filt = _dot_split(hdn, w_out_ref[...])
    t = z[:, 0:1]
    valid = z[:, LANES - 1:LANES]
    kern = filt * jnp.exp(-t * dl_ref[...]) * valid
    k_ref[...] = kern
    ss_ref[...] += jnp.sum(kern * kern, axis=0, keepdims=True)


def _hyena_kernel(l, d, pe_w_in, pe_b_in, pe_w_mid, pe_b_mid, pe_w_out, sin_freq, *, tl=256):
    bands = (HY_EMB - 1) // 2
    hidden = pe_w_in.shape[1]
    n_inner = pe_w_mid.shape[0]
    pos = jnp.concatenate([jnp.arange(l), l - jnp.arange(l)]).astype(F32)
    valid = jnp.ones((2 * l,), F32).at[l].set(0.0)
    tt = pos / (l - 1)
    fr = jnp.linspace(1e-4, bands - 1, bands, dtype=F32)
    wpos = 2.0 * math.pi * pos[:, None] / l
    z = jnp.concatenate([tt[:, None], jnp.cos(fr * wpos), -jnp.sin(fr * wpos)], axis=-1)
    z = jnp.concatenate([z, jnp.zeros((2 * l, LANES - HY_EMB - 1), F32), valid[:, None]], axis=-1)
    w_in = jnp.zeros((LANES, hidden), F32).at[:HY_EMB].set(pe_w_in)
    max_decay = math.log(HY_DECAY_TARGET) / HY_FAST_DECAY
    min_decay = math.log(HY_DECAY_TARGET) / HY_SLOW_DECAY
    deltas = jnp.abs(jnp.linspace(min_decay, max_decay, d, dtype=F32))[None]
    tl = _pick(l, tl, SUBLANES)
    nt = l // tl
    const = lambda shape: pl.BlockSpec(shape, lambda a, i: tuple(0 for _ in shape))
    return pl.pallas_call(
        functools.partial(_filter_body, n_inner=n_inner),
        grid=(2, nt),
        in_specs=[pl.BlockSpec((tl, LANES), lambda a, i: (a * nt + i, 0)),
                  const((LANES, hidden)), const((1, hidden)),
                  const((n_inner, hidden, hidden)), const((n_inner, hidden)), const((1, hidden)),
                  pl.BlockSpec((hidden, d), lambda a, i: (0, a)),
                  const((1, d))],
        out_specs=[pl.BlockSpec((tl, d), lambda a, i: (a * nt + i, 0)),
                   pl.BlockSpec((1, d), lambda a, i: (0, 0))],
        out_shape=[jax.ShapeDtypeStruct((2 * l, d), F32), jax.ShapeDtypeStruct((1, d), F32)],
        compiler_params=_params("arbitrary", "arbitrary"),
        name="hyena_filter",
    )(z, w_in, pe_b_in[None], pe_w_mid, pe_b_mid, sin_freq[None], pe_w_out, deltas)


def _conv_tables(l):
    n = 2 * l
    f = DFT_F
    s = n // f
    ar = jnp.arange(s, dtype=jnp.int32)
    c, sn = _cis(ar[:, None] * ar[None, :], s)
    fwd = jnp.stack([c, -sn], axis=1)
    p1_full = _page_matrix(fwd.reshape(2 * s, s))
    p1_half = _page_matrix(fwd[:, :, :s // 2].reshape(2 * s, s // 2))
    inv = jnp.stack([c, -sn], axis=-1)[:s // 2] / n
    p2 = _page_matrix(inv.reshape(s // 2, 2 * s))
    cos, sin = _twiddle_grid(s, f, n)
    g1 = jnp.concatenate([jnp.concatenate([cos, sin], axis=2),
                          jnp.concatenate([-sin, cos], axis=2)], axis=1).astype(BF16)
    cos_t, sin_t = jnp.swapaxes(cos, 1, 2), jnp.swapaxes(sin, 1, 2)
    g2 = jnp.concatenate([jnp.concatenate([cos_t, -sin_t], axis=2),
                          jnp.concatenate([sin_t, cos_t], axis=2)], axis=1).astype(BF16)
    return p1_half, p1_full, p2, g1, g2


def _spectral_conv_body(g1_ref, g2_ref, a_ref, ka_ref, o_ref):
    g1 = g1_ref[...]
    x = jnp.dot(g1, a_ref[...].astype(BF16), preferred_element_type=F32)
    kf = jnp.dot(g1, ka_ref[...].astype(BF16), preferred_element_type=F32)
    f = x.shape[0] // 2
    xr, xi, kr, ki = x[:f], x[f:], kf[:f], kf[f:]
    y = jnp.concatenate([xr * kr - xi * ki, xr * ki + xi * kr], axis=0).astype(BF16)
    o_ref[...] = jnp.dot(g2_ref[...], y, preferred_element_type=F32).astype(o_ref.dtype)


def _batched_stage(name, body, mats, arrays, *, dc=2048):
    nb, rows, d = arrays[0].shape
    dc = _pick(d, dc)
    mspecs = [pl.BlockSpec((None,) + m.shape[1:], lambda b, j: (b, 0, 0)) for m in mats]
    aspecs = [pl.BlockSpec((None, rows, dc), lambda b, j: (b, 0, j)) for _ in arrays]
    return pl.pallas_call(
        body,
        grid=(nb, d // dc),
        in_specs=[*mspecs, *aspecs],
        out_specs=pl.BlockSpec((None, mats[-1].shape[1], dc), lambda b, j: (b, 0, j)),
        out_shape=jax.ShapeDtypeStruct((nb, mats[-1].shape[1], d), BF16),
        compiler_params=_params("parallel", "parallel"),
        name=name,
    )(*mats, *arrays)


def _conv_out_body(m_ref, b_ref, p_ref, x0_ref, ss_ref, bias_ref, o_ref):
    y = _page_contract(m_ref[...], b_ref[...])
    z = y * lax.rsqrt(ss_ref[...] + EPS) + p_ref[...] * bias_ref[...]
    o_ref[...] = (x0_ref[...] * z).astype(o_ref.dtype)


def _hyena(h, w_in, conv_w, conv_b, pe_w_in, pe_b_in, pe_w_mid, pe_b_mid, pe_w_out, sin_freq, bias,
           w_out, x, gate_a):
    l, d = h.shape
    f = DFT_F
    s = 2 * l // f
    tm, tn = _pick(l, 1024), _pick(3 * d, 512)
    u = _matmul("hyena_in", h, [w_in], m=l, n=3 * d, k=d, tm=tm, tn=tn, tk=d,
                out_shape=jax.ShapeDtypeStruct((l, 3 * d), F32),
                out_spec=pl.BlockSpec((tm, tn), lambda i, j, kk: (i, j)))
    x0c, p = _short_conv_gate(u, conv_w, conv_b)
    kern, ss = _hyena_kernel(l, d, pe_w_in, pe_b_in, pe_w_mid, pe_b_mid, pe_w_out, sin_freq)
    p1_half, p1_full, p2, g1, g2 = _conv_tables(l)
    ka = _page_mm(p1_full, kern.reshape(s, 1, f, d), s, 2)
    pa = _page_mm(p1_half, p.reshape(s // 2, 1, f, d), s, 2)
    pb = _batched_stage("hyena_spectral_conv", _spectral_conv_body, [g1, g2],
                        [pa.reshape(s, 2 * f, d), ka.reshape(s, 2 * f, d)], dc=4096)
    dc = _pick(d, 512)
    page = pl.BlockSpec((s // 2, PAGE, dc), lambda a, b: (0, a, b))
    chan = pl.BlockSpec((1, dc), lambda a, b: (0, b))
    y = pl.pallas_call(
        _conv_out_body,
        grid=(f // PAGE, d // dc),
        in_specs=[pl.BlockSpec(p2.shape, lambda a, b: (0, 0)),
                  pl.BlockSpec((s, 2, PAGE, dc), lambda a, b: (0, 0, a, b)),
                  page, page, chan, chan],
        out_specs=page,
        out_shape=jax.ShapeDtypeStruct((s // 2, f, d), BF16),
        compiler_params=_params("parallel", "parallel"),
        name="hyena_conv_out",
    )(p2, pb.reshape(s, 2, f, d), p.reshape(s // 2, f, d), x0c.reshape(s // 2, f, d), ss, bias[None])
    return _proj_residual("hyena_out", y.reshape(l, d), w_out, x, gate_a)


def _pool_body(h_ref, o_ref, pad_ref, *, rows, per_group, halo):
    n_rows = h_ref.shape[0]
    dc = h_ref.shape[1]
    pad_ref[pl.ds(0, halo), :] = jnp.zeros((halo, dc), F32)
    pad_ref[pl.ds(halo + n_rows, halo), :] = jnp.zeros((halo, dc), F32)
    pad_ref[pl.ds(halo, n_rows), :] = h_ref[...]
    group = pl.program_id(0) // per_group

    for g, w in enumerate(POOL_SIZES):
        @pl.when(group == g)
        def _(w=w):
            before, after = w // 2, w - w // 2

            def step(c, carry):
                r0 = pl.multiple_of(c * rows, rows)
                ext = pad_ref[pl.ds(r0, rows + 2 * halo), :]
                tot = ext[halo - before:halo - before + rows]
                for o in range(1 - before, after):
                    tot = tot + ext[halo + o:halo + o + rows]
                t = r0 + lax.broadcasted_iota(jnp.int32, (rows, dc), 0)
                cnt = jnp.minimum(t + after, n_rows) - jnp.maximum(t - before, 0)
                cur = ext[halo:halo + rows]
                o_ref[pl.ds(r0, rows), :] = (tot / cnt.astype(F32) - cur).astype(o_ref.dtype)
                return carry

            lax.fori_loop(0, n_rows // rows, step, 0)


def _pool_features(h, *, dc=256, rows=256):
    l, d = h.shape
    group = d // len(POOL_SIZES)
    dc = _pick(group, dc)
    rows = _pick(l, rows, SUBLANES)
    halo = max(POOL_SIZES) // 2
    return pl.pallas_call(
        functools.partial(_pool_body, rows=rows, per_group=group // dc, halo=halo),
        grid=(d // dc,),
        in_specs=[pl.BlockSpec((l, dc), lambda j: (0, j))],
        out_specs=pl.BlockSpec((l, dc), lambda j: (0, j)),
        out_shape=jax.ShapeDtypeStruct((l, d), BF16),
        scratch_shapes=[pltpu.VMEM((l + 2 * halo, dc), F32)],
        compiler_params=_params("parallel"),
        name="pool_features",
    )(h)


def _pool_mixer(h, w_groups, scale, x, gate_a, *, tm=1024, tn=512):
    l, d = h.shape
    n_g, group, _ = w_groups.shape
    pooled = _pool_features(h)
    tm, tn = _pick(l, tm), _pick(group, tn)
    per = group // tn
    return _matmul(
        "pool_proj", pooled, [w_groups], m=l, n=d, k=group, tm=tm, tn=tn, tk=group,
        out_shape=jax.ShapeDtypeStruct((l, d), F32),
        out_spec=pl.BlockSpec((tm, tn), lambda i, j, kk: (i, j)),
        a_spec=pl.BlockSpec((tm, group), lambda i, j, kk: (i, j // per)),
        b_specs=[pl.BlockSpec((None, group, tn), lambda i, j, kk: (j // per, 0, j % per))],
        extras=[x, gate_a, scale],
        extra_specs=[pl.BlockSpec((tm, tn), lambda i, j, kk: (i, j)),
                     pl.BlockSpec((1, tn), lambda i, j, kk: (0, j)),
                     pl.BlockSpec((1, tn), lambda i, j, kk: (0, j))],
        epilogue=lambda accs, ex: ex[0][...] + ex[1][...] * (accs[0] * ex[2][...]))


def _fourier_tables(l, group):
    f = DFT_F
    s = l // f
    ci = jnp.arange(group, dtype=jnp.int32)[:, None]
    co = jnp.arange(2 * group, dtype=jnp.int32)[None, :]
    thc = _angles(ci * (co % group), group)
    wc = jnp.where(co < group, jnp.cos(thc), -jnp.sin(thc)).astype(BF16)
    ar = jnp.arange(s, dtype=jnp.int32)
    c, sn = _cis(ar[:, None] * ar[None, :], s)
    stage1 = jnp.stack([jnp.stack([c, sn], axis=-1), jnp.stack([-sn, c], axis=-1)], axis=1)
    p1 = _page_matrix(stage1.reshape(2 * s, 2 * s))
    cos, sin = _twiddle_grid(s, f, l)
    g = jnp.concatenate([cos, sin], axis=2).reshape(s // SUBLANES, SUBLANES, f, 2 * f)
    rows = g.transpose(0, 2, 1, 3).reshape(s // SUBLANES, f * SUBLANES, 2 * f)
    row = jnp.arange(f * SUBLANES, dtype=jnp.int32)[None, :, None]
    col = jnp.arange(SUBLANES * 2 * f, dtype=jnp.int32)[None, None, :]
    scat = jnp.where(row % SUBLANES == col // (2 * f), jnp.tile(rows, (1, 1, SUBLANES)), 0.0).astype(BF16)
    return wc, p1, scat


def _fourier_out_body(m_ref, a_ref, o_ref, *, scale):
    y = jnp.dot(m_ref[...], a_ref[...].astype(BF16), preferred_element_type=F32) * scale
    o_ref[...] = y.reshape(o_ref.shape)


def _fourier_mixer(h, w_out, x, gate_a, *, tm=1024, tn=512, dc=1024):
    l, d = h.shape
    group = d // FN_GROUPS
    f = DFT_F
    s = l // f
    wc, p1, scat = _fourier_tables(l, group)
    tm, tn = _pick(l, tm, f), _pick(group, tn)
    per = group // tn
    z = _matmul(
        "fourier_chan", h, [wc], m=l, n=2 * d, k=group, tm=tm, tn=tn, tk=group,
        out_shape=jax.ShapeDtypeStruct((s, 2, f, d), BF16),
        out_spec=pl.BlockSpec((tm // f, None, f, tn),
                              lambda i, j, kk: (i, (j // per) % 2, 0, (j // (2 * per)) * per + j % per)),
        a_spec=pl.BlockSpec((tm, group), lambda i, j, kk: (i, j // (2 * per))),
        b_specs=[pl.BlockSpec((group, tn), lambda i, j, kk: (0, j % (2 * per)))])
    a2 = _page_mm(p1, z, s, 2)
    dc = _pick(d, dc)
    q = s // SUBLANES
    mixed = pl.pallas_call(
        functools.partial(_fourier_out_body, scale=1.0 / math.sqrt(l * group)),
        grid=(q, d // dc),
        in_specs=[pl.BlockSpec((None,) + scat.shape[1:], lambda a, b: (a, 0, 0)),
                  pl.BlockSpec((None, SUBLANES * 2 * f, dc), lambda a, b: (a, 0, b))],
        out_specs=pl.BlockSpec((f, None, SUBLANES, dc), lambda a, b: (0, a, 0, b)),
        out_shape=jax.ShapeDtypeStruct((f, q, SUBLANES, d), F32),
        compiler_params=_params("parallel", "parallel"),
        name="fourier_seq_out",
    )(scat, a2.reshape(q, SUBLANES * 2 * f, d))
    return _proj_residual("fourier_out", mixed.reshape(l, d).astype(BF16), w_out, x, gate_a)


def kernel(x, c, ctx, c_ctx, ada_down, ada_up, ada_b, norm_g, final_g, da_w_qkv, da_w_o, da_lambda, da_subln_g, hy_w_in, hy_conv_w, hy_conv_b, hy_pe_w_in, hy_pe_b_in, hy_pe_w_mid, hy_pe_b_mid, hy_pe_w_out, hy_sin_freq, hy_bias, hy_w_out, pool_w, pool_scale, fn_w_out, ffn_w_gate, ffn_w_up, ffn_w_down, moe_router, moe_router_b, moe_w_gate, moe_w_up, moe_w_down):
    batch, l, d = x.shape
    depth = ada_down.shape[0]
    assert batch == 1
    xs = x[0]
    mod = _adaln_all(c, c_ctx, ada_down, ada_up, ada_b)
    ffn_w_down_bf = ffn_w_down.astype(BF16)
    for i in range(depth):
        shift_a, scale_a, gate_a, shift_f, scale_f, gate_f = [
            mod[i, 0:1, n * d:(n + 1) * d] for n in range(N_MOD)]
        g_a, g_f = norm_g[i, 0][None], norm_g[i, 1][None]
        mixer = i % 4
        if mixer == 0:
            h = _norm_mod(xs, g_a, shift_a, scale_a, BF16)
            hc = _norm_mod(ctx[0], g_a, mod[i, 1:2, 0:d], mod[i, 1:2, d:2 * d], BF16)
            xs = _diff_attention(h, hc, da_w_qkv, da_w_o, da_lambda, da_subln_g[None],
                                 0.8 - 0.6 * math.exp(-0.3 * i), xs, gate_a)
        elif mixer == 1:
            h = _norm_mod(xs, g_a, shift_a, scale_a, BF16)
            xs = _hyena(h, hy_w_in, hy_conv_w, hy_conv_b, hy_pe_w_in, hy_pe_b_in, hy_pe_w_mid,
                        hy_pe_b_mid, hy_pe_w_out, hy_sin_freq, hy_bias, hy_w_out, xs, gate_a)
        elif mixer == 2:
            h = _norm_mod(xs, g_a, shift_a, scale_a, F32)
            xs = _pool_mixer(h, pool_w, pool_scale[None], xs, gate_a)
        else:
            h = _norm_mod(xs, g_a, shift_a, scale_a, BF16)
            xs = _fourier_mixer(h, fn_w_out, xs, gate_a)
        if i % 2 == 0:
            h = _norm_mod(xs, g_f, shift_f, scale_f, BF16)
            xs = _swiglu(h, ffn_w_gate, ffn_w_up, ffn_w_down_bf, i // 2, xs, gate_f)
        else:
            j = i // 2
            h, sel = _norm_router(xs, g_f, shift_f, scale_f, moe_router[j], moe_router_b[j])
            last = i == depth - 1
            xs = _moe(h, sel, moe_w_gate, moe_w_up, moe_w_down, j, xs, gate_f, final_g[None], last)
    if depth % 2 == 1:
        xs = _rmsnorm(xs, final_g[None])
    return xs[None]
```

```python
import functools
import math

import jax
import jax.numpy as jnp
from jax import lax
from jax.experimental import pallas as pl
from jax.experimental.pallas import tpu as pltpu

F32 = jnp.float32
BF16 = jnp.bfloat16
HIGHEST = lax.Precision.HIGHEST

EPS = 1e-6
GRID_W = 64
N_MOD = 6
DA_HEADS = 16
DA_HEAD_DIM = 128
ROPE_THETA = 10000.0
HY_EMB = 33
HY_DECAY_TARGET = 1e-2
HY_FAST_DECAY = 0.3
HY_SLOW_DECAY = 1.5
POOL_SIZES = (2, 4, 8, 16)
FN_GROUPS = 4
TOP_K = 2

LANES = 128
SUBLANES = 8
DFT_F = 128
VMEM_LIMIT = 56 * 1024 * 1024


def _params(*sem):
    return pltpu.CompilerParams(dimension_semantics=sem, vmem_limit_bytes=VMEM_LIMIT)


def _pick(dim, pref, align=LANES):
    if dim <= pref:
        return dim
    t = (pref // align) * align
    while t >= align:
        if dim % t == 0:
            return t
        t -= align
    raise ValueError(f"no {align}-aligned tile of {dim} below {pref}")


def _mm_body(*refs, n_b, n_extra, nk, epilogue):
    a_ref = refs[0]
    b_refs = refs[1:1 + n_b]
    extra_refs = refs[1 + n_b:1 + n_b + n_extra]
    o_ref = refs[1 + n_b + n_extra]
    acc_refs = refs[2 + n_b + n_extra:]
    a = a_ref[...].astype(BF16)
    if nk == 1:
        accs = [jnp.dot(a, b[...].astype(BF16), preferred_element_type=F32) for b in b_refs]
        o_ref[...] = epilogue(accs, extra_refs).reshape(o_ref.shape).astype(o_ref.dtype)
        return
    k = pl.program_id(2)

    @pl.when(k == 0)
    def _():
        for acc in acc_refs:
            acc[...] = jnp.zeros_like(acc)

    for acc, b in zip(acc_refs, b_refs):
        acc[...] += jnp.dot(a, b[...].astype(BF16), preferred_element_type=F32)

    @pl.when(k == nk - 1)
    def _():
        accs = [acc[...] for acc in acc_refs]
        o_ref[...] = epilogue(accs, extra_refs).reshape(o_ref.shape).astype(o_ref.dtype)


def _matmul(name, a, bs, *, m, n, k, tm, tn, tk, out_shape, out_spec, a_spec=None, b_specs=None,
            extras=(), extra_specs=(), epilogue=None):
    nk = k // tk
    if epilogue is None:
        epilogue = lambda accs, ex: accs[0]
    if a_spec is None:
        a_spec = pl.BlockSpec((tm, tk), lambda i, j, kk: (i, kk))
    if b_specs is None:
        b_specs = [pl.BlockSpec((tk, tn), lambda i, j, kk: (kk, j)) for _ in bs]
    scratch = [pltpu.VMEM((tm, tn), F32) for _ in bs] if nk > 1 else []
    body = functools.partial(_mm_body, n_b=len(bs), n_extra=len(extras), nk=nk, epilogue=epilogue)
    return pl.pallas_call(
        body,
        grid=(m // tm, n // tn, nk),
        in_specs=[a_spec, *b_specs, *extra_specs],
        out_specs=out_spec,
        out_shape=out_shape,
        scratch_shapes=scratch,
        compiler_params=_params("parallel", "parallel", "arbitrary"),
        name=name,
    )(a, *bs, *extras)


def _residual_epilogue(accs, ex):
    return ex[0][...] + ex[1][...] * accs[0]


def _proj_residual(name, a, w, x, gate, *, tm=1024, tn=512):
    m, k = a.shape
    n = w.shape[1]
    tm, tn = _pick(m, tm), _pick(n, tn)
    return _matmul(
        name, a, [w], m=m, n=n, k=k, tm=tm, tn=tn, tk=k,
        out_shape=jax.ShapeDtypeStruct((m, n), F32),
        out_spec=pl.BlockSpec((tm, tn), lambda i, j, kk: (i, j)),
        extras=[x, gate],
        extra_specs=[pl.BlockSpec((tm, tn), lambda i, j, kk: (i, j)),
                     pl.BlockSpec((1, tn), lambda i, j, kk: (0, j))],
        epilogue=_residual_epilogue)


def _dot_split(a, b):
    a_hi, b_hi = a.astype(BF16), b.astype(BF16)
    a_lo = (a - a_hi.astype(F32)).astype(BF16)
    b_lo = (b - b_hi.astype(F32)).astype(BF16)
    dot = functools.partial(jnp.dot, preferred_element_type=F32)
    return dot(a_hi, b_hi) + (dot(a_hi, b_lo) + dot(a_lo, b_hi))


def _ada_body(a_ref, w_ref, b_ref, o_ref, *, silu):
    a = a_ref[...]
    if silu:
        a = a * jax.nn.sigmoid(a)
    o_ref[...] = _dot_split(a, w_ref[...]) + b_ref[...]


def _adaln_all(c, c_ctx, ada_down, ada_up, ada_b):
    depth, d, r = ada_down.shape
    width = ada_up.shape[2]
    rows = jnp.zeros((SUBLANES, d), F32).at[0].set(c[0]).at[1].set(c_ctx)
    tn1 = _pick(r, 256)
    t = pl.pallas_call(
        functools.partial(_ada_body, silu=True),
        grid=(depth, r // tn1),
        in_specs=[pl.BlockSpec((SUBLANES, d), lambda l, j: (0, 0)),
                  pl.BlockSpec((None, d, tn1), lambda l, j: (l, 0, j)),
                  pl.BlockSpec((1, tn1), lambda l, j: (0, j))],
        out_specs=pl.BlockSpec((None, SUBLANES, tn1), lambda l, j: (l, 0, j)),
        out_shape=jax.ShapeDtypeStruct((depth, SUBLANES, r), F32),
        compiler_params=_params("parallel", "parallel"),
        name="adaln_down",
    )(rows, ada_down, jnp.zeros((1, r), F32))
    tn2 = _pick(width, 2048)
    return pl.pallas_call(
        functools.partial(_ada_body, silu=False),
        grid=(depth, width // tn2),
        in_specs=[pl.BlockSpec((None, SUBLANES, r), lambda l, j: (l, 0, 0)),
                  pl.BlockSpec((None, r, tn2), lambda l, j: (l, 0, j)),
                  pl.BlockSpec((None, 1, tn2), lambda l, j: (l, 0, j))],
        out_specs=pl.BlockSpec((None, SUBLANES, tn2), lambda l, j: (l, 0, j)),
        out_shape=jax.ShapeDtypeStruct((depth, SUBLANES, width), F32),
        compiler_params=_params("parallel", "parallel"),
        name="adaln_up",
    )(t, ada_up, ada_b[:, None, :])


def _norm_rows(x, g):
    return x * lax.rsqrt(jnp.mean(x * x, axis=-1, keepdims=True) + EPS) * g


def _norm_mod_body(x_ref, g_ref, shift_ref, scale_ref, o_ref):
    y = _norm_rows(x_ref[...], g_ref[...])
    o_ref[...] = (y * (1.0 + scale_ref[...]) + shift_ref[...]).astype(o_ref.dtype)


def _norm_mod(x, g, shift, scale, out_dtype, *, tm=512):
    m, d = x.shape
    tm = _pick(m, tm, SUBLANES)
    row = pl.BlockSpec((1, d), lambda i: (0, 0))
    return pl.pallas_call(
        _norm_mod_body,
        grid=(m // tm,),
        in_specs=[pl.BlockSpec((tm, d), lambda i: (i, 0)), row, row, row],
        out_specs=pl.BlockSpec((tm, d), lambda i: (i, 0)),
        out_shape=jax.ShapeDtypeStruct((m, d), out_dtype),
        compiler_params=_params("parallel"),
        name="norm_mod",
    )(x, g, shift, scale)


def _rmsnorm_body(x_ref, g_ref, o_ref):
    o_ref[...] = _norm_rows(x_ref[...], g_ref[...])


def _rmsnorm(x, g, *, tm=256):
    m, d = x.shape
    tm = _pick(m, tm, SUBLANES)
    return pl.pallas_call(
        _rmsnorm_body,
        grid=(m // tm,),
        in_specs=[pl.BlockSpec((tm, d), lambda i: (i, 0)), pl.BlockSpec((1, d), lambda i: (0, 0))],
        out_specs=pl.BlockSpec((tm, d), lambda i: (i, 0)),
        out_shape=jax.ShapeDtypeStruct((m, d), F32),
        compiler_params=_params("parallel"),
        name="final_norm",
    )(x, g)


def _pack_bf16_pairs(x):
    half = x.shape[1] // 2
    bits = pltpu.bitcast(x.astype(BF16).astype(F32), jnp.uint32)
    return (bits[:, half:] & jnp.uint32(0xFFFF0000)) | (bits[:, :half] >> 16)


def _unpack_bf16_pairs(w):
    lo = pltpu.bitcast(w << 16, F32)
    hi = pltpu.bitcast(w & jnp.uint32(0xFFFF0000), F32)
    return jnp.concatenate([lo, hi], axis=1)


def _norm_router_body(x_ref, g_ref, shift_ref, scale_ref, rw_ref, rb_ref, h_ref, sel_ref):
    h = _norm_rows(x_ref[...], g_ref[...]) * (1.0 + scale_ref[...]) + shift_ref[...]
    h_ref[...] = _pack_bf16_pairs(h)
    logits = jnp.dot(h, rw_ref[...], precision=HIGHEST, preferred_element_type=F32) + rb_ref[...]
    lane = lax.broadcasted_iota(jnp.int32, logits.shape, 1)
    m1 = jnp.max(logits, axis=-1, keepdims=True)
    i1 = jnp.min(jnp.where(logits == m1, lane, LANES), axis=-1, keepdims=True)
    rest = jnp.where(lane == i1, -jnp.inf, logits)
    m2 = jnp.max(rest, axis=-1, keepdims=True)
    i2 = jnp.min(jnp.where(rest == m2, lane, LANES), axis=-1, keepdims=True)
    e = jnp.exp(m2 - m1)
    w1 = 1.0 / (1.0 + e)
    w2 = e * w1
    sel_ref[...] = jnp.where(lane == 0, i1.astype(F32),
                             jnp.where(lane == 1, i2.astype(F32),
                                       jnp.where(lane == 2, w1, jnp.where(lane == 3, w2, 0.0))))


def _norm_router(x, g, shift, scale, router, router_b, *, tm=256):
    m, d = x.shape
    n_e = router.shape[1]
    tm = _pick(m, tm, SUBLANES)
    rw = jnp.zeros((d, LANES), F32).at[:, :n_e].set(router)
    rb = jnp.full((1, LANES), -1e30, F32).at[0, :n_e].set(router_b)
    row = pl.BlockSpec((1, d), lambda i: (0, 0))
    return pl.pallas_call(
        _norm_router_body,
        grid=(m // tm,),
        in_specs=[pl.BlockSpec((tm, d), lambda i: (i, 0)), row, row, row,
                  pl.BlockSpec((d, LANES), lambda i: (0, 0)),
                  pl.BlockSpec((1, LANES), lambda i: (0, 0))],
        out_specs=[pl.BlockSpec((tm, d // 2), lambda i: (i, 0)),
                   pl.BlockSpec((tm, LANES), lambda i: (i, 0))],
        out_shape=[jax.ShapeDtypeStruct((m, d // 2), jnp.uint32),
                   jax.ShapeDtypeStruct((m, LANES), F32)],
        compiler_params=_params("parallel"),
        name="norm_router",
    )(x, g, shift, scale, rw, rb)


def _silu(a):
    return a * jax.nn.sigmoid(a)


def _swiglu(h, w_gate, w_up, w_down, layer, x, gate_f):
    m, d = h.shape
    f = w_gate.shape[2]
    tm = _pick(m, 2048)
    tn = _pick(f, 256)
    wspec = pl.BlockSpec((None, d, tn), lambda i, j, kk: (layer, 0, j))
    hid = _matmul(
        "swiglu_up", h, [w_gate, w_up], m=m, n=f, k=d, tm=tm, tn=tn, tk=d,
        out_shape=jax.ShapeDtypeStruct((m, f), BF16),
        out_spec=pl.BlockSpec((tm, tn), lambda i, j, kk: (i, j)),
        a_spec=pl.BlockSpec((tm, d), lambda i, j, kk: (i, 0), pipeline_mode=pl.Buffered(1)),
        b_specs=[wspec, wspec],
        epilogue=lambda accs, ex: _silu(accs[0]) * accs[1])
    tm2 = _pick(m, 1024)
    tn2 = _pick(d, 256)
    return _matmul(
        "swiglu_down", hid, [w_down], m=m, n=d, k=f, tm=tm2, tn=tn2, tk=f,
        out_shape=jax.ShapeDtypeStruct((m, d), F32),
        out_spec=pl.BlockSpec((tm2, tn2), lambda i, j, kk: (i, j)),
        a_spec=pl.BlockSpec((tm2, f), lambda i, j, kk: (i, 0), pipeline_mode=pl.Buffered(1)),
        b_specs=[pl.BlockSpec((None, f, tn2), lambda i, j, kk: (layer, 0, j))],
        extras=[x, gate_f],
        extra_specs=[pl.BlockSpec((tm2, tn2), lambda i, j, kk: (i, j)),
                     pl.BlockSpec((1, tn2), lambda i, j, kk: (0, j))],
        epilogue=_residual_epilogue)


MOE_TILE = 512


def _moe_plan(sel, n_e, tile):
    l = sel.shape[0]
    e = jnp.concatenate([sel[:, 0], sel[:, 1]]).astype(jnp.int32)
    onehot = (e[:, None] == jnp.arange(n_e, dtype=jnp.int32)[None, :]).astype(jnp.int32)
    incl = jnp.cumsum(onehot, axis=0)
    counts = incl[-1]
    padded = ((counts + tile - 1) // tile) * tile
    ends = jnp.cumsum(padded)
    starts = ends - padded
    dest = jnp.sum(onehot * (starts[None, :] + incl - 1), axis=1)
    n_rows = 2 * l + n_e * tile
    n_tiles = n_rows // tile
    token = jnp.tile(jnp.arange(l, dtype=jnp.int32), 2)
    tok = jnp.zeros((n_rows,), jnp.int32).at[dest].set(token)
    n_active = ends[-1] // tile
    t_clamped = jnp.minimum(jnp.arange(n_tiles, dtype=jnp.int32), n_active - 1)
    tile_expert = jnp.sum((t_clamped[:, None] * tile >= ends[None, :]).astype(jnp.int32), axis=1)
    return dest, tok, tile_expert, n_active.reshape(1).astype(jnp.int32)


def _prefetched_rows(copies, n_rows, n_steps):
    i = pl.program_id(0)
    slot = i % 2

    def issue(step, into):
        def body(r, carry):
            for cp in copies(step, r, into):
                cp.start()
            return carry
        lax.fori_loop(0, n_rows, body, 0, unroll=8)

    def drain(r, carry):
        for cp in copies(i, r, slot):
            cp.wait()
        return carry

    @pl.when(i == 0)
    def _():
        issue(0, 0)

    @pl.when(i + 1 < n_steps)
    def _():
        issue(i + 1, 1 - slot)

    lax.fori_loop(0, n_rows, drain, 0, unroll=8)
    return slot


def _row_gather_body(idx_ref, src_hbm, o_ref, buf, sem, *, rows, n_steps):
    def copies(step, r, slot):
        return [pltpu.make_async_copy(src_hbm.at[pl.ds(idx_ref[step * rows + r], 1), :],
                                      buf.at[slot, pl.ds(r, 1), :], sem.at[slot])]

    slot = _prefetched_rows(copies, rows, n_steps)
    o_ref[...] = _unpack_bf16_pairs(buf[slot]).astype(o_ref.dtype)


def _row_gather(src, idx, *, rows=256):
    n, d = idx.shape[0], src.shape[1]
    rows = _pick(n, rows, SUBLANES)
    n_steps = n // rows
    return pl.pallas_call(
        functools.partial(_row_gather_body, rows=rows, n_steps=n_steps),
        grid_spec=pltpu.PrefetchScalarGridSpec(
            num_scalar_prefetch=1,
            grid=(n_steps,),
            in_specs=[pl.BlockSpec(memory_space=pl.ANY)],
            out_specs=pl.BlockSpec((rows, 2 * d), lambda i, idx_ref: (i, 0)),
            scratch_shapes=[pltpu.VMEM((2, rows, d), src.dtype), pltpu.SemaphoreType.DMA((2,))]),
        out_shape=jax.ShapeDtypeStruct((n, 2 * d), BF16),
        compiler_params=_params("arbitrary"),
        name="moe_gather",
    )(idx, src)


def _grouped_body(te_ref, na_ref, a_ref, *refs, n_b, n_extra, epilogue):
    b_refs = refs[:n_b]
    extra_refs = refs[n_b:n_b + n_extra]
    o_ref = refs[n_b + n_extra]
    bf_refs = refs[n_b + n_extra + 1:]
    t = pl.program_id(1)
    active = t < na_ref[0]
    fresh = jnp.logical_or(t == 0, te_ref[t] != te_ref[jnp.maximum(t - 1, 0)])

    @pl.when(jnp.logical_and(active, fresh))
    def _():
        for b, bf in zip(b_refs, bf_refs):
            bf[...] = b[...].astype(BF16)

    @pl.when(active)
    def _():
        a = a_ref[...]
        accs = [jnp.dot(a, bf[...], preferred_element_type=F32) for bf in bf_refs]
        o_ref[...] = epilogue(accs, extra_refs).astype(o_ref.dtype)

    @pl.when(jnp.logical_not(active))
    def _():
        o_ref[...] = jnp.zeros_like(o_ref)


def _grouped_matmul(name, a, ws, layer, tile_expert, n_active, *, tile, tn, out_dtype, extras=(), epilogue=None):
    p, k = a.shape
    n = ws[0].shape[3]
    if epilogue is None:
        epilogue = lambda accs, ex: accs[0]
    out_tn = tn // 2 if out_dtype == jnp.uint32 else tn
    row = lambda j, t, te, na: jnp.minimum(t, na[0] - 1)
    return pl.pallas_call(
        functools.partial(_grouped_body, n_b=len(ws), n_extra=len(extras), epilogue=epilogue),
        grid_spec=pltpu.PrefetchScalarGridSpec(
            num_scalar_prefetch=2,
            grid=(n // tn, p // tile),
            in_specs=[pl.BlockSpec((tile, k), lambda j, t, te, na: (row(j, t, te, na), 0)),
                      *[pl.BlockSpec((None, None, k, tn), lambda j, t, te, na: (layer, te[t], 0, j)) for _ in ws],
                      *[pl.BlockSpec((tile, e.shape[1]), lambda j, t, te, na: (row(j, t, te, na), 0))
                        for e in extras]],
            out_specs=pl.BlockSpec((tile, out_tn), lambda j, t, te, na: (t, j)),
            scratch_shapes=[pltpu.VMEM((k, tn), BF16) for _ in ws]),
        out_shape=jax.ShapeDtypeStruct((p, n // tn * out_tn), out_dtype),
        compiler_params=_params("arbitrary", "arbitrary"),
        name=name,
    )(tile_expert, n_active, a, *ws, *extras)


def _combine_body(pos_ref, y_hbm, x_ref, g_ref, sel_ref, ng_ref, o_ref, buf, sem, *,
                  rows, n_tok, n_steps, pack_block, final_norm):
    def copies(step, r, slot):
        return [pltpu.make_async_copy(y_hbm.at[pl.ds(pos_ref[k * n_tok + step * rows + r], 1), :],
                                      buf.at[slot, k, pl.ds(r, 1), :], sem.at[slot]) for k in range(TOP_K)]

    slot = _prefetched_rows(copies, rows, n_steps)

    def expert_rows(k):
        w = buf[slot, k]
        half = pack_block // 2
        return jnp.concatenate([_unpack_bf16_pairs(w[:, b * half:(b + 1) * half])
                                for b in range(w.shape[1] // half)], axis=1)

    sel = sel_ref[...]
    y = sel[:, 2:3] * expert_rows(0) + sel[:, 3:4] * expert_rows(1)
    out = x_ref[...] + g_ref[...] * y
    o_ref[...] = _norm_rows(out, ng_ref[...]) if final_norm else out


def _moe_combine(y, dest, sel, x, gate_f, norm_g, *, pack_block, final_norm, rows=256):
    l, d = x.shape
    rows = _pick(l, rows, SUBLANES)
    n_steps = l // rows
    vec = pl.BlockSpec((1, d), lambda i, pos: (0, 0))
    return pl.pallas_call(
        functools.partial(_combine_body, rows=rows, n_tok=l, n_steps=n_steps, pack_block=pack_block,
                          final_norm=final_norm),
        grid_spec=pltpu.PrefetchScalarGridSpec(
            num_scalar_prefetch=1,
            grid=(n_steps,),
            in_specs=[pl.BlockSpec(memory_space=pl.ANY),
                      pl.BlockSpec((rows, d), lambda i, pos: (i, 0)),
                      vec,
                      pl.BlockSpec((rows, LANES), lambda i, pos: (i, 0)),
                      vec],
            out_specs=pl.BlockSpec((rows, d), lambda i, pos: (i, 0)),
            scratch_shapes=[pltpu.VMEM((2, TOP_K, rows, d // 2), jnp.uint32), pltpu.SemaphoreType.DMA((2,))]),
        out_shape=jax.ShapeDtypeStruct((l, d), F32),
        compiler_params=_params("arbitrary"),
        name="moe_combine",
    )(dest, y, x, gate_f, sel, norm_g)


def _moe(h, sel, w_gate, w_up, w_down, layer, x, gate_f, norm_g, final_norm):
    _, n_e, d, fe = w_gate.shape
    tile = MOE_TILE
    dest, tok, tile_expert, n_active = _moe_plan(sel, n_e, tile)
    hs = _row_gather(h, tok)
    hid = _grouped_matmul(
        "moe_up", hs, [w_gate, w_up], layer, tile_expert, n_active, tile=tile, tn=_pick(fe, 256),
        out_dtype=BF16, epilogue=lambda accs, ex: _silu(accs[0]) * accs[1])
    tn = _pick(d, 2048)
    y = _grouped_matmul("moe_down", hid, [w_down], layer, tile_expert, n_active, tile=tile, tn=tn,
                        out_dtype=jnp.uint32, epilogue=lambda accs, ex: _pack_bf16_pairs(accs[0]))
    return _moe_combine(y, dest, sel, x, gate_f, norm_g, pack_block=tn, final_norm=final_norm)


def _rope_tables(n_tokens):
    pairs = DA_HEAD_DIM // 4
    rows = n_tokens // GRID_W
    row = jnp.repeat(jnp.arange(rows, dtype=F32), GRID_W)
    col = jnp.tile(jnp.arange(GRID_W, dtype=F32), rows)
    inv = ROPE_THETA ** (-jnp.arange(pairs, dtype=F32) / pairs)
    ang = jnp.concatenate([row[:, None] * inv, col[:, None] * inv], axis=-1)
    cos, sin = jnp.cos(ang), jnp.sin(ang)
    return jnp.concatenate([cos, cos], axis=-1), jnp.concatenate([-sin, sin], axis=-1)


def _qkv_rope(h, w_qkv, cos2, sin2, *, tm=1024, tn=512):
    m, d = h.shape
    n = w_qkv.shape[1]
    tm, tn = _pick(m, tm), _pick(d, tn)
    n_q = d // tn
    half = DA_HEAD_DIM // 2

    def epi(accs, ex):
        acc = accs[0]
        j = pl.program_id(1)
        cos, sin = ex[0][...], ex[1][...]
        heads = []
        for t in range(tn // DA_HEAD_DIM):
            xh = acc[:, t * DA_HEAD_DIM:(t + 1) * DA_HEAD_DIM]
            heads.append(xh * cos + pltpu.roll(xh, half, 1) * sin)
        roped = jnp.concatenate(heads, axis=1)
        qscale = jnp.where(j < n_q, DA_HEAD_DIM ** -0.5 * math.log2(math.e), 1.0).astype(F32)
        return jnp.where(j < 2 * n_q, roped * qscale, acc)

    tab = pl.BlockSpec((tm, DA_HEAD_DIM), lambda i, j, kk: (i, 0))
    return _matmul(
        "qkv_rope", h, [w_qkv], m=m, n=n, k=d, tm=tm, tn=tn, tk=d,
        out_shape=jax.ShapeDtypeStruct((m, n), BF16),
        out_spec=pl.BlockSpec((tm, tn), lambda i, j, kk: (i, j)),
        extras=[cos2, sin2], extra_specs=[tab, tab], epilogue=epi)


def _flash_body(lam_ref, g_ref, q_ref, kt_ref, v_ref, o_ref,
                m_sc, l_sc, acc_sc, p0_sc, p1_sc, a0_sc, a1_sc, *, tk, nkv, lambda_init):
    n_val = v_ref.shape[1] // LANES
    n_lane_tiles = tk // LANES
    m_sc[...] = jnp.full_like(m_sc, -jnp.inf)
    l_sc[...] = jnp.zeros_like(l_sc)
    acc_sc[...] = jnp.zeros_like(acc_sc)

    def accumulate(j, p_r, a_r):
        v = v_ref[pl.ds(pl.multiple_of(j * tk, tk), tk), :]
        for c in range(2):
            alpha = jnp.concatenate([a_r[c]] * n_val, axis=1)
            acc_sc[c] = acc_sc[c] * alpha + jnp.dot(p_r[c], v, preferred_element_type=F32)

    def score(j, p_w, a_w):
        q = q_ref[...]
        kt = kt_ref[j]
        for c in range(2):
            qc = q[:, c * DA_HEAD_DIM:(c + 1) * DA_HEAD_DIM]
            kc = kt[c * DA_HEAD_DIM:(c + 1) * DA_HEAD_DIM, :]
            s = jnp.dot(qc, kc, preferred_element_type=F32)
            tiles = [s[:, t * LANES:(t + 1) * LANES] for t in range(n_lane_tiles)]
            m_prev = m_sc[c]
            m_next = jnp.maximum(m_prev, jnp.max(functools.reduce(jnp.maximum, tiles), axis=1, keepdims=True))
            alpha = jnp.exp2(m_prev - m_next)
            ps = [jnp.exp2(t - m_next) for t in tiles]
            l_sc[c] = alpha * l_sc[c] + functools.reduce(jnp.add, ps)
            m_sc[c] = m_next
            a_w[c] = alpha
            p_w[c] = jnp.concatenate(ps, axis=1).astype(BF16)

    even, odd = (p0_sc, a0_sc), (p1_sc, a1_sc)
    score(0, *even)

    def pair(i, carry):
        j = 2 * i + 1
        score(j, *odd)
        accumulate(j - 1, *even)
        score(j + 1, *even)
        accumulate(j, *odd)
        return carry

    lax.fori_loop(0, (nkv - 1) // 2, pair, 0)
    if nkv % 2 == 0:
        score(nkv - 1, *odd)
        accumulate(nkv - 2, *even)
        accumulate(nkv - 1, *odd)
    else:
        accumulate(nkv - 1, *even)

    lam = lam_ref[...]
    lam_full = (jnp.exp(jnp.sum(lam[0:1] * lam[1:2], axis=-1, keepdims=True))
                - jnp.exp(jnp.sum(lam[2:3] * lam[3:4], axis=-1, keepdims=True)) + lambda_init)
    l0 = jnp.sum(l_sc[0], axis=1, keepdims=True)
    l1 = jnp.sum(l_sc[1], axis=1, keepdims=True)
    o = acc_sc[0] / l0 - lam_full * (acc_sc[1] / l1)
    o_ref[...] = (_norm_rows(o, g_ref[...]) * (1.0 - lambda_init)).astype(o_ref.dtype)


def _diff_flash(q, k_all, v_all, lam, subln_g, lambda_init, *, tq=1024, tk=768):
    l = q.shape[0]
    t, d = v_all.shape
    hw = 2 * DA_HEAD_DIM
    n_heads = d // hw
    tq = _pick(l, tq)
    tk = _pick(t, tk)
    nkv = t // tk
    kt = k_all.reshape(nkv, tk, n_heads, hw).transpose(2, 0, 3, 1)
    return pl.pallas_call(
        functools.partial(_flash_body, tk=tk, nkv=nkv, lambda_init=lambda_init),
        grid=(n_heads, l // tq),
        in_specs=[pl.BlockSpec(lam.shape, lambda h, i: (0, 0)),
                  pl.BlockSpec((1, hw), lambda h, i: (0, 0)),
                  pl.BlockSpec((tq, hw), lambda h, i: (i, h)),
                  pl.BlockSpec((None, nkv, hw, tk), lambda h, i: (h, 0, 0, 0)),
                  pl.BlockSpec((t, hw), lambda h, i: (0, h))],
        out_specs=pl.BlockSpec((tq, hw), lambda h, i: (i, h)),
        out_shape=jax.ShapeDtypeStruct((l, d), BF16),
        scratch_shapes=[pltpu.VMEM((2, tq, LANES), F32), pltpu.VMEM((2, tq, LANES), F32),
                        pltpu.VMEM((2, tq, hw), F32),
                        pltpu.VMEM((2, tq, tk), BF16), pltpu.VMEM((2, tq, tk), BF16),
                        pltpu.VMEM((2, tq, LANES), F32), pltpu.VMEM((2, tq, LANES), F32)],
        compiler_params=_params("parallel", "parallel"),
        name="diff_flash",
    )(lam, subln_g, q, kt, v_all)


def _diff_attention(h, hc, w_qkv, w_o, lam, subln_g, lambda_init, x, gate_a):
    l, d = h.shape
    cos2, sin2 = _rope_tables(l)
    qkv = _qkv_rope(h, w_qkv, cos2, sin2)
    c_len = hc.shape[0]
    tmc = _pick(c_len, 256)
    kvc = _matmul(
        "ctx_kv", hc, [w_qkv], m=c_len, n=2 * d, k=d, tm=tmc, tn=512, tk=d,
        out_shape=jax.ShapeDtypeStruct((c_len, 2 * d), BF16),
        out_spec=pl.BlockSpec((tmc, 512), lambda i, j, kk: (i, j)),
        b_specs=[pl.BlockSpec((d, 512), lambda i, j, kk: (0, j + d // 512))])
    k_all = jnp.concatenate([qkv[:, d:2 * d], kvc[:, :d]], axis=0)
    v_all = jnp.concatenate([qkv[:, 2 * d:], kvc[:, d:]], axis=0)
    o = _diff_flash(qkv, k_all, v_all, lam, subln_g, lambda_init)
    return _proj_residual("attn_out", o, w_o, x, gate_a)


def _angles(num, den):
    return (2.0 * math.pi / den) * (num % den).astype(F32)


def _cis(num, den):
    th = _angles(num, den)
    return jnp.cos(th), jnp.sin(th)


def _twiddle_grid(s, f, n):
    ar = lambda m: jnp.arange(m, dtype=jnp.int32)
    ac, asn = _cis(ar(s)[:, None] * ar(f)[None, :], n)
    bc, bsn = _cis(ar(f)[:, None] * ar(f)[None, :], f)
    cos = ac[:, None, :] * bc[None, :, :] - asn[:, None, :] * bsn[None, :, :]
    sin = asn[:, None, :] * bc[None, :, :] + ac[:, None, :] * bsn[None, :, :]
    return cos, sin


def _page_matrix(base):
    nr, nc = base.shape
    r = jnp.arange(nr * SUBLANES, dtype=jnp.int32)[:, None]
    c = jnp.arange(nc * SUBLANES, dtype=jnp.int32)[None, :]
    expand_r = (r // SUBLANES == jnp.arange(nr, dtype=jnp.int32)[None, :]).astype(F32)
    expand_c = (jnp.arange(nc, dtype=jnp.int32)[:, None] == c // SUBLANES).astype(F32)
    full = jnp.dot(jnp.dot(expand_r, base), expand_c)
    return jnp.where(r % SUBLANES == c % SUBLANES, full, 0.0).astype(BF16)


PAGE = 2 * SUBLANES


def _page_contract(mat, x):
    dc = x.shape[-1]
    x3 = x.astype(F32).reshape(-1, PAGE, dc)
    halves = []
    for t in range(PAGE // SUBLANES):
        xt = x3[:, t * SUBLANES:(t + 1) * SUBLANES, :].reshape(-1, dc).astype(BF16)
        halves.append(jnp.dot(mat, xt, preferred_element_type=F32).reshape(-1, SUBLANES, dc))
    return jnp.concatenate(halves, axis=1)


def _page_mm_body(m_ref, x_ref, o_ref):
    o_ref[...] = _page_contract(m_ref[...], x_ref[...]).reshape(o_ref.shape).astype(o_ref.dtype)


def _page_mm(mat, x4, ko, po, *, dc=512):
    s, pi, f, d = x4.shape
    dc = _pick(d, dc)
    return pl.pallas_call(
        _page_mm_body,
        grid=(f // PAGE, d // dc),
        in_specs=[pl.BlockSpec(mat.shape, lambda a, b: (0, 0)),
                  pl.BlockSpec((s, pi, PAGE, dc), lambda a, b: (0, 0, a, b))],
        out_specs=pl.BlockSpec((ko, po, PAGE, dc), lambda a, b: (0, 0, a, b)),
        out_shape=jax.ShapeDtypeStruct((ko, po, f, d), BF16),
        compiler_params=_params("parallel", "parallel"),
        name="dft_lead",
    )(mat, x4)


def _short_conv_body(u0_ref, u1_ref, u2_ref, w0_ref, w1_ref, w2_ref, b0_ref, b1_ref, b2_ref,
                     x0_ref, p_ref, *, rows):
    n_rows = u0_ref.shape[0]
    n_chunks = n_rows // rows

    def conv(u_ref, w_ref, b_ref, r0, c):
        cur = u_ref[pl.ds(r0, rows), :]
        prev = u_ref[pl.ds(jnp.maximum(r0 - 1, 0), 1), :] * jnp.where(c > 0, 1.0, 0.0)
        nxt = u_ref[pl.ds(jnp.minimum(r0 + rows, n_rows - 1), 1), :] * jnp.where(c < n_chunks - 1, 1.0, 0.0)
        ridx = lax.broadcasted_iota(jnp.int32, cur.shape, 0)
        up = jnp.where(ridx == 0, prev, pltpu.roll(cur, 1, 0))
        down = jnp.where(ridx == rows - 1, nxt, pltpu.roll(cur, rows - 1, 0))
        w = w_ref[...]
        return w[0:1] * up + w[1:2] * cur + w[2:3] * down + b_ref[...]

    def step(c, carry):
        r0 = pl.multiple_of(c * rows, rows)
        x0_ref[pl.ds(r0, rows), :] = conv(u0_ref, w0_ref, b0_ref, r0, c)
        p_ref[pl.ds(r0, rows), :] = conv(u2_ref, w2_ref, b2_ref, r0, c) * conv(u1_ref, w1_ref, b1_ref, r0, c)
        return carry

    lax.fori_loop(0, n_chunks, step, 0)


def _short_conv_gate(u, conv_w, conv_b, *, dc=128, rows=512):
    l, d3 = u.shape
    d = d3 // 3
    nb = d // dc
    rows = _pick(l, rows, SUBLANES)
    us = [pl.BlockSpec((l, dc), lambda j, s=s: (0, j + s * nb)) for s in range(3)]
    ws = [pl.BlockSpec((3, dc), lambda j, s=s: (0, j + s * nb)) for s in range(3)]
    bs = [pl.BlockSpec((1, dc), lambda j, s=s: (0, j + s * nb)) for s in range(3)]
    out = pl.BlockSpec((l, dc), lambda j: (0, j))
    return pl.pallas_call(
        functools.partial(_short_conv_body, rows=rows),
        grid=(nb,),
        in_specs=[*us, *ws, *bs],
        out_specs=[out, out],
        out_shape=[jax.ShapeDtypeStruct((l, d), F32), jax.ShapeDtypeStruct((l, d), F32)],
        compiler_params=_params("parallel"),
        name="hyena_short_conv",
    )(u, u, u, conv_w, conv_w, conv_w, conv_b[None], conv_b[None], conv_b[None])


def _filter_body(z_ref, w_in_ref, b_in_ref, w_mid_ref, b_mid_ref, fq_ref, w_out_ref, dl_ref,
                 k_ref, ss_ref, *, n_inner):
    first = jnp.logical_and(pl.program_id(0) == 0, pl.program_id(1) == 0)

    @pl.when(first)
    def _():
        ss_ref[...] = jnp.zeros_like(ss_ref)

    z = z_ref[...]
    fq = fq_ref[...]
    hdn = jnp.sin(fq * (jnp.dot(z, w_in_ref[...], precision=HIGHEST, preferred_element_type=F32)
                        + b_in_ref[...]))
    for j in range(n_inner):
        hdn = jnp.sin(fq * (jnp.dot(hdn, w_mid_ref[j], precision=HIGHEST, preferred_element_type=F32)
                            + b_mid_ref[j:j + 1]))
    filt = _dot_split(hdn, w_out_ref[...])
    t = z[:, 0:1]
    valid = z[:, LANES - 1:LANES]
    kern = filt * jnp.exp(-t * dl_ref[...]) * valid
    k_ref[...] = kern
    ss_ref[...] += jnp.sum(kern * kern, axis=0, keepdims=True)


def _hyena_kernel(l, d, pe_w_in, pe_b_in, pe_w_mid, pe_b_mid, pe_w_out, sin_freq, *, tl=512):
    bands = (HY_EMB - 1) // 2
    hidden = pe_w_in.shape[1]
    n_inner = pe_w_mid.shape[0]
    pos = jnp.concatenate([jnp.arange(l), l - jnp.arange(l)]).astype(F32)
    valid = jnp.ones((2 * l,), F32).at[l].set(0.0)
    tt = pos / (l - 1)
    fr = jnp.linspace(1e-4, bands - 1, bands, dtype=F32)
    wpos = 2.0 * math.pi * pos[:, None] / l
    z = jnp.concatenate([tt[:, None], jnp.cos(fr * wpos), -jnp.sin(fr * wpos)], axis=-1)
    z = jnp.concatenate([z, jnp.zeros((2 * l, LANES - HY_EMB - 1), F32), valid[:, None]], axis=-1)
    w_in = jnp.zeros((LANES, hidden), F32).at[:HY_EMB].set(pe_w_in)
    max_decay = math.log(HY_DECAY_TARGET) / HY_FAST_DECAY
    min_decay = math.log(HY_DECAY_TARGET) / HY_SLOW_DECAY
    deltas = jnp.abs(jnp.linspace(min_decay, max_decay, d, dtype=F32))[None]
    tl = _pick(l, tl, SUBLANES)
    nt = l // tl
    const = lambda shape: pl.BlockSpec(shape, lambda a, i: tuple(0 for _ in shape))
    return pl.pallas_call(
        functools.partial(_filter_body, n_inner=n_inner),
        grid=(2, nt),
        in_specs=[pl.BlockSpec((tl, LANES), lambda a, i: (a * nt + i, 0)),
                  const((LANES, hidden)), const((1, hidden)),
                  const((n_inner, hidden, hidden)), const((n_inner, hidden)), const((1, hidden)),
                  pl.BlockSpec((hidden, d), lambda a, i: (0, a)),
                  const((1, d))],
        out_specs=[pl.BlockSpec((tl, d), lambda a, i: (a * nt + i, 0)),
                   pl.BlockSpec((1, d), lambda a, i: (0, 0))],
        out_shape=[jax.ShapeDtypeStruct((2 * l, d), F32), jax.ShapeDtypeStruct((1, d), F32)],
        compiler_params=_params("arbitrary", "arbitrary"),
        name="hyena_filter",
    )(z, w_in, pe_b_in[None], pe_w_mid, pe_b_mid, sin_freq[None], pe_w_out, deltas)


def _conv_tables(l):
    n = 2 * l
    f = DFT_F
    s = n // f
    ar = jnp.arange(s, dtype=jnp.int32)
    c, sn = _cis(ar[:, None] * ar[None, :], s)
    fwd = jnp.stack([c, -sn], axis=1)
    p1_full = _page_matrix(fwd.reshape(2 * s, s))
    p1_half = _page_matrix(fwd[:, :, :s // 2].reshape(2 * s, s // 2))
    inv = jnp.stack([c, -sn], axis=-1)[:s // 2] / n
    p2 = _page_matrix(inv.reshape(s // 2, 2 * s))
    cos, sin = _twiddle_grid(s, f, n)
    g1 = jnp.concatenate([jnp.concatenate([cos, sin], axis=2),
                          jnp.concatenate([-sin, cos], axis=2)], axis=1).astype(BF16)
    cos_t, sin_t = jnp.swapaxes(cos, 1, 2), jnp.swapaxes(sin, 1, 2)
    g2 = jnp.concatenate([jnp.concatenate([cos_t, -sin_t], axis=2),
                          jnp.concatenate([sin_t, cos_t], axis=2)], axis=1).astype(BF16)
    return p1_half, p1_full, p2, g1, g2


def _spectral_conv_body(g1_ref, g2_ref, a_ref, ka_ref, o_ref):
    g1 = g1_ref[...]
    x = jnp.dot(g1, a_ref[...].astype(BF16), preferred_element_type=F32)
    kf = jnp.dot(g1, ka_ref[...].astype(BF16), preferred_element_type=F32)
    f = x.shape[0] // 2
    xr, xi, kr, ki = x[:f], x[f:], kf[:f], kf[f:]
    y = jnp.concatenate([xr * kr - xi * ki, xr * ki + xi * kr], axis=0).astype(BF16)
    o_ref[...] = jnp.dot(g2_ref[...], y, preferred_element_type=F32).astype(o_ref.dtype)


def _batched_stage(name, body, mats, arrays, *, dc=2048):
    nb, rows, d = arrays[0].shape
    dc = _pick(d, dc)
    mspecs = [pl.BlockSpec((None,) + m.shape[1:], lambda b, j: (b, 0, 0)) for m in mats]
    aspecs = [pl.BlockSpec((None, rows, dc), lambda b, j: (b, 0, j)) for _ in arrays]
    return pl.pallas_call(
        body,
        grid=(nb, d // dc),
        in_specs=[*mspecs, *aspecs],
        out_specs=pl.BlockSpec((None, mats[-1].shape[1], dc), lambda b, j: (b, 0, j)),
        out_shape=jax.ShapeDtypeStruct((nb, mats[-1].shape[1], d), BF16),
        compiler_params=_params("parallel", "parallel"),
        name=name,
    )(*mats, *arrays)


def _conv_out_body(m_ref, b_ref, p_ref, x0_ref, ss_ref, bias_ref, o_ref):
    y = _page_contract(m_ref[...], b_ref[...])
    z = y * lax.rsqrt(ss_ref[...] + EPS) + p_ref[...] * bias_ref[...]
    o_ref[...] = (x0_ref[...] * z).astype(o_ref.dtype)


def _hyena(h, w_in, conv_w, conv_b, pe_w_in, pe_b_in, pe_w_mid, pe_b_mid, pe_w_out, sin_freq, bias,
           w_out, x, gate_a):
    l, d = h.shape
    f = DFT_F
    s = 2 * l // f
    tm, tn = _pick(l, 1024), _pick(3 * d, 512)
    u = _matmul("hyena_in", h, [w_in], m=l, n=3 * d, k=d, tm=tm, tn=tn, tk=d,
                out_shape=jax.ShapeDtypeStruct((l, 3 * d), F32),
                out_spec=pl.BlockSpec((tm, tn), lambda i, j, kk: (i, j)))
    x0c, p = _short_conv_gate(u, conv_w, conv_b)
    kern, ss = _hyena_kernel(l, d, pe_w_in, pe_b_in, pe_w_mid, pe_b_mid, pe_w_out, sin_freq)
    p1_half, p1_full, p2, g1, g2 = _conv_tables(l)
    ka = _page_mm(p1_full, kern.reshape(s, 1, f, d), s, 2)
    pa = _page_mm(p1_half, p.reshape(s // 2, 1, f, d), s, 2)
    pb = _batched_stage("hyena_spectral_conv", _spectral_conv_body, [g1, g2],
                        [pa.reshape(s, 2 * f, d), ka.reshape(s, 2 * f, d)], dc=4096)
    dc = _pick(d, 512)
    page = pl.BlockSpec((s // 2, PAGE, dc), lambda a, b: (0, a, b))
    chan = pl.BlockSpec((1, dc), lambda a, b: (0, b))
    y = pl.pallas_call(
        _conv_out_body,
        grid=(f // PAGE, d // dc),
        in_specs=[pl.BlockSpec(p2.shape, lambda a, b: (0, 0)),
                  pl.BlockSpec((s, 2, PAGE, dc), lambda a, b: (0, 0, a, b)),
                  page, page, chan, chan],
        out_specs=page,
        out_shape=jax.ShapeDtypeStruct((s // 2, f, d), BF16),
        compiler_params=_params("parallel", "parallel"),
        name="hyena_conv_out",
    )(p2, pb.reshape(s, 2, f, d), p.reshape(s // 2, f, d), x0c.reshape(s // 2, f, d), ss, bias[None])
    return _proj_residual("hyena_out", y.reshape(l, d), w_out, x, gate_a)


def _pool_body(h_ref, o_ref, pad_ref, *, rows, per_group, halo):
    n_rows = h_ref.shape[0]
    dc = h_ref.shape[1]
    pad_ref[pl.ds(0, halo), :] = jnp.zeros((halo, dc), F32)
    pad_ref[pl.ds(halo + n_rows, halo), :] = jnp.zeros((halo, dc), F32)
    pad_ref[pl.ds(halo, n_rows), :] = h_ref[...]
    group = pl.program_id(0) // per_group

    for g, w in enumerate(POOL_SIZES):
        @pl.when(group == g)
        def _(w=w):
            before, after = w // 2, w - w // 2

            def step(c, carry):
                r0 = pl.multiple_of(c * rows, rows)
                ext = pad_ref[pl.ds(r0, rows + 2 * halo), :]
                tot = ext[halo - before:halo - before + rows]
                for o in range(1 - before, after):
                    tot = tot + ext[halo + o:halo + o + rows]
                t = r0 + lax.broadcasted_iota(jnp.int32, (rows, dc), 0)
                cnt = jnp.minimum(t + after, n_rows) - jnp.maximum(t - before, 0)
                cur = ext[halo:halo + rows]
                o_ref[pl.ds(r0, rows), :] = (tot / cnt.astype(F32) - cur).astype(o_ref.dtype)
                return carry

            lax.fori_loop(0, n_rows // rows, step, 0)


def _pool_features(h, *, dc=256, rows=256):
    l, d = h.shape
    group = d // len(POOL_SIZES)
    dc = _pick(group, dc)
    rows = _pick(l, rows, SUBLANES)
    halo = max(POOL_SIZES) // 2
    return pl.pallas_call(
        functools.partial(_pool_body, rows=rows, per_group=group // dc, halo=halo),
        grid=(d // dc,),
        in_specs=[pl.BlockSpec((l, dc), lambda j: (0, j))],
        out_specs=pl.BlockSpec((l, dc), lambda j: (0, j)),
        out_shape=jax.ShapeDtypeStruct((l, d), BF16),
        scratch_shapes=[pltpu.VMEM((l + 2 * halo, dc), F32)],
        compiler_params=_params("parallel"),
        name="pool_features",
    )(h)


def _pool_mixer(h, w_groups, scale, x, gate_a, *, tm=1024, tn=512):
    l, d = h.shape
    n_g, group, _ = w_groups.shape
    pooled = _pool_features(h)
    tm, tn = _pick(l, tm), _pick(group, tn)
    per = group // tn
    return _matmul(
        "pool_proj", pooled, [w_groups], m=l, n=d, k=group, tm=tm, tn=tn, tk=group,
        out_shape=jax.ShapeDtypeStruct((l, d), F32),
        out_spec=pl.BlockSpec((tm, tn), lambda i, j, kk: (i, j)),
        a_spec=pl.BlockSpec((tm, group), lambda i, j, kk: (i, j // per)),
        b_specs=[pl.BlockSpec((None, group, tn), lambda i, j, kk: (j // per, 0, j % per))],
        extras=[x, gate_a, scale],
        extra_specs=[pl.BlockSpec((tm, tn), lambda i, j, kk: (i, j)),
                     pl.BlockSpec((1, tn), lambda i, j, kk: (0, j)),
                     pl.BlockSpec((1, tn), lambda i, j, kk: (0, j))],
        epilogue=lambda accs, ex: ex[0][...] + ex[1][...] * (accs[0] * ex[2][...]))


def _fourier_tables(l, group):
    f = DFT_F
    s = l // f
    ci = jnp.arange(group, dtype=jnp.int32)[:, None]
    co = jnp.arange(2 * group, dtype=jnp.int32)[None, :]
    thc = _angles(ci * (co % group), group)
    wc = jnp.where(co < group, jnp.cos(thc), -jnp.sin(thc)).astype(BF16)
    ar = jnp.arange(s, dtype=jnp.int32)
    c, sn = _cis(ar[:, None] * ar[None, :], s)
    stage1 = jnp.stack([jnp.stack([c, sn], axis=-1), jnp.stack([-sn, c], axis=-1)], axis=1)
    p1 = _page_matrix(stage1.reshape(2 * s, 2 * s))
    cos, sin = _twiddle_grid(s, f, l)
    g = jnp.concatenate([cos, sin], axis=2).reshape(s // SUBLANES, SUBLANES, f, 2 * f)
    rows = g.transpose(0, 2, 1, 3).reshape(s // SUBLANES, f * SUBLANES, 2 * f)
    row = jnp.arange(f * SUBLANES, dtype=jnp.int32)[None, :, None]
    col = jnp.arange(SUBLANES * 2 * f, dtype=jnp.int32)[None, None, :]
    scat = jnp.where(row % SUBLANES == col // (2 * f), jnp.tile(rows, (1, 1, SUBLANES)), 0.0).astype(BF16)
    return wc, p1, scat


def _fourier_out_body(m_ref, a_ref, o_ref, *, scale):
    y = jnp.dot(m_ref[...], a_ref[...].astype(BF16), preferred_element_type=F32) * scale
    o_ref[...] = y.reshape(o_ref.shape)


def _fourier_mixer(h, w_out, x, gate_a, *, tm=1024, tn=512, dc=1024):
    l, d = h.shape
    group = d // FN_GROUPS
    f = DFT_F
    s = l // f
    wc, p1, scat = _fourier_tables(l, group)
    tm, tn = _pick(l, tm, f), _pick(group, tn)
    per = group // tn
    z = _matmul(
        "fourier_chan", h, [wc], m=l, n=2 * d, k=group, tm=tm, tn=tn, tk=group,
        out_shape=jax.ShapeDtypeStruct((s, 2, f, d), BF16),
        out_spec=pl.BlockSpec((tm // f, None, f, tn),
                              lambda i, j, kk: (i, (j // per) % 2, 0, (j // (2 * per)) * per + j % per)),
        a_spec=pl.BlockSpec((tm, group), lambda i, j, kk: (i, j // (2 * per))),
        b_specs=[pl.BlockSpec((group, tn), lambda i, j, kk: (0, j % (2 * per)))])
    a2 = _page_mm(p1, z, s, 2)
    dc = _pick(d, dc)
    q = s // SUBLANES
    mixed = pl.pallas_call(
        functools.partial(_fourier_out_body, scale=1.0 / math.sqrt(l * group)),
        grid=(q, d // dc),
        in_specs=[pl.BlockSpec((None,) + scat.shape[1:], lambda a, b: (a, 0, 0)),
                  pl.BlockSpec((None, SUBLANES * 2 * f, dc), lambda a, b: (a, 0, b))],
        out_specs=pl.BlockSpec((f, None, SUBLANES, dc), lambda a, b: (0, a, 0, b)),
        out_shape=jax.ShapeDtypeStruct((f, q, SUBLANES, d), F32),
        compiler_params=_params("parallel", "parallel"),
        name="fourier_seq_out",
    )(scat, a2.reshape(q, SUBLANES * 2 * f, d))
    return _proj_residual("fourier_out", mixed.reshape(l, d).astype(BF16), w_out, x, gate_a)


def kernel(x, c, ctx, c_ctx, ada_down, ada_up, ada_b, norm_g, final_g, da_w_qkv, da_w_o, da_lambda, da_subln_g, hy_w_in, hy_conv_w, hy_conv_b, hy_pe_w_in, hy_pe_b_in, hy_pe_w_mid, hy_pe_b_mid, hy_pe_w_out, hy_sin_freq, hy_bias, hy_w_out, pool_w, pool_scale, fn_w_out, ffn_w_gate, ffn_w_up, ffn_w_down, moe_router, moe_router_b, moe_w_gate, moe_w_up, moe_w_down):
    batch, l, d = x.shape
    depth = ada_down.shape[0]
    assert batch == 1
    xs = x[0]
    mod = _adaln_all(c, c_ctx, ada_down, ada_up, ada_b)
    ffn_w_down_bf = ffn_w_down.astype(BF16)
    for i in range(depth):
        shift_a, scale_a, gate_a, shift_f, scale_f, gate_f = [
            mod[i, 0:1, n * d:(n + 1) * d] for n in range(N_MOD)]
        g_a, g_f = norm_g[i, 0][None], norm_g[i, 1][None]
        mixer = i % 4
        if mixer == 0:
            h = _norm_mod(xs, g_a, shift_a, scale_a, BF16)
            hc = _norm_mod(ctx[0], g_a, mod[i, 1:2, 0:d], mod[i, 1:2, d:2 * d], BF16)
            xs = _diff_attention(h, hc, da_w_qkv, da_w_o, da_lambda, da_subln_g[None],
                                 0.8 - 0.6 * math.exp(-0.3 * i), xs, gate_a)
        elif mixer == 1:
            h = _norm_mod(xs, g_a, shift_a, scale_a, BF16)
            xs = _hyena(h, hy_w_in, hy_conv_w, hy_conv_b, hy_pe_w_in, hy_pe_b_in, hy_pe_w_mid,
                        hy_pe_b_mid, hy_pe_w_out, hy_sin_freq, hy_bias, hy_w_out, xs, gate_a)
        elif mixer == 2:
            h = _norm_mod(xs, g_a, shift_a, scale_a, F32)
            xs = _pool_mixer(h, pool_w, pool_scale[None], xs, gate_a)
        else:
            h = _norm_mod(xs, g_a, shift_a, scale_a, BF16)
            xs = _fourier_mixer(h, fn_w_out, xs, gate_a)
        if i % 2 == 0:
            h = _norm_mod(xs, g_f, shift_f, scale_f, BF16)
            xs = _swiglu(h, ffn_w_gate, ffn_w_up, ffn_w_down_bf, i // 2, xs, gate_f)
        else:
            j = i // 2
            h, sel = _norm_router(xs, g_f, shift_f, scale_f, moe_router[j], moe_router_b[j])
            last = i == depth - 1
            xs = _moe(h, sel, moe_w_gate, moe_w_up, moe_w_down, j, xs, gate_f, final_g[None], last)
    if depth % 2 == 1:
        xs = _rmsnorm(xs, final_g[None])
    return xs[None]
```
